```python
import math
import jax, jax.numpy as jnp
from jax import lax
import numpy as np

D_MODEL = 1024
BATCH = 8
SEQ = 2048
DEPTH = 1
DEC_BATCH = 16
DEC_SEQ = 16
PAST_LEN = 4096

CHUNK = 64
Q_BLOCK = 128
MIX_WIDTH = D_MODEL
MLA_HEADS = 8
NOPE_DIM = 64
ROPE_DIM = 32
V_DIM = 64
MLA_WIDTH = MLA_HEADS * V_DIM
Q_LORA = 384
KV_LORA = 256
ROPE_THETA = 10000.0
ATTN_SCALE = 1.0 / math.sqrt(NOPE_DIM + ROPE_DIM)
CONV_CH = MIX_WIDTH - MLA_WIDTH
CONV_W = 31
CONV_STATE = CONV_W - 1
D_FF = -(-8 * D_MODEL // (3 * 256)) * 256
IN_WIDTH = Q_LORA + KV_LORA + ROPE_DIM + 2 * CONV_CH
EPS = 1e-6

kernel_name = "hybrid_mla_conformer_conv_stream_step"


def rms_norm(x, g):
    xf = x.astype(jnp.float32)
    y = xf * lax.rsqrt(jnp.mean(xf * xf, axis=-1, keepdims=True) + EPS)
    return y.astype(x.dtype) * g


def layer_norm(x, g, b):
    xf = x.astype(jnp.float32)
    mu = jnp.mean(xf, axis=-1, keepdims=True)
    xc = xf - mu
    y = xc * lax.rsqrt(jnp.mean(xc * xc, axis=-1, keepdims=True) + EPS)
    return y.astype(x.dtype) * g + b


def rope_angles(pos):
    inv = 1.0 / (ROPE_THETA ** (jnp.arange(0, ROPE_DIM, 2, dtype=jnp.float32) / ROPE_DIM))
    ang = pos.astype(jnp.float32)[:, None] * inv[None, :]
    return jnp.cos(ang), jnp.sin(ang)


def apply_rope(x, cos, sin):
    xf = x.astype(jnp.float32)
    x1, x2 = xf[..., : ROPE_DIM // 2], xf[..., ROPE_DIM // 2:]
    return jnp.concatenate([x1 * cos - x2 * sin, x2 * cos + x1 * sin], axis=-1).astype(x.dtype)


def latent_attend(q_lat, q_pe, c_kv, k_pe, mask):
    s = (jnp.einsum('bqhc,bkc->bhqk', q_lat, c_kv)
         + jnp.einsum('bqhr,bkr->bhqk', q_pe, k_pe)).astype(jnp.float32) * ATTN_SCALE
    if mask is not None:
        s = jnp.where(mask, s, -1e30)
    p = jax.nn.softmax(s, axis=-1).astype(c_kv.dtype)
    return jnp.einsum('bhqk,bkc->bqhc', p, c_kv)


def token_mixers(hn, pos, ckv_past, kpe_past, conv_past,
                 w_in, g_q, w_uq, g_kv, w_uk, w_uv, w_dw, b_dw, g_cn, b_cn, g_om, g_oc, w_out):
    B, S, _ = hn.shape
    proj = hn @ w_in
    cq, ckv, kpe, conv_in = jnp.split(
        proj, [Q_LORA, Q_LORA + KV_LORA, Q_LORA + KV_LORA + ROPE_DIM], axis=-1)
    cos, sin = rope_angles(pos)
    q = (rms_norm(cq, g_q) @ w_uq).reshape(B, S, MLA_HEADS, NOPE_DIM + ROPE_DIM)
    q_nope, q_pe = q[..., :NOPE_DIM], q[..., NOPE_DIM:]
    q_pe = apply_rope(q_pe, cos[:, None, :], sin[:, None, :])
    q_lat = jnp.einsum('bshd,chd->bshc', q_nope, w_uk)
    ckv = rms_norm(ckv, g_kv)
    kpe = apply_rope(kpe, cos, sin)
    a, gate = jnp.split(conv_in, 2, axis=-1)
    u = a * jax.nn.sigmoid(gate)
    if ckv_past is None:
        nb = S // Q_BLOCK
        k_chunk = jnp.arange(S) // CHUNK

        def block(args):
            qi, ql, qp = args
            q_chunk = (qi * Q_BLOCK + jnp.arange(Q_BLOCK)) // CHUNK
            mask = k_chunk[None, :] <= q_chunk[:, None]
            return latent_attend(ql, qp, ckv, kpe, mask)

        qlb = q_lat.reshape(B, nb, Q_BLOCK, MLA_HEADS, KV_LORA).swapaxes(0, 1)
        qpb = q_pe.reshape(B, nb, Q_BLOCK, MLA_HEADS, ROPE_DIM).swapaxes(0, 1)
        o_lat = lax.map(block, (jnp.arange(nb), qlb, qpb))
        o_lat = o_lat.swapaxes(0, 1).reshape(B, S, MLA_HEADS, KV_LORA)
        conv_full = jnp.pad(u, ((0, 0), (CONV_STATE, 0), (0, 0)))
    else:
        keys_c = jnp.concatenate([ckv_past, ckv], axis=1)
        keys_r = jnp.concatenate([kpe_past, kpe], axis=1)
        o_lat = latent_attend(q_lat, q_pe, keys_c, keys_r, None)
        conv_full = jnp.concatenate([conv_past, u], axis=1)
    o_mla = jnp.einsum('bshc,chd->bshd', o_lat, w_uv).reshape(B, S, MLA_WIDTH)
    dw = lax.conv_general_dilated(conv_full, w_dw[:, None, :], (1,), 'VALID',
                                  dimension_numbers=('NWC', 'WIO', 'NWC'),
                                  feature_group_count=CONV_CH) + b_dw
    conv_out = jax.nn.silu(layer_norm(dw, g_cn, b_cn))
    mixed = jnp.concatenate([rms_norm(o_mla, g_om), rms_norm(conv_out, g_oc)], axis=-1) @ w_out
    new_conv = conv_full[:, -CONV_STATE:]
    return mixed, ckv, kpe, new_conv


def encoder_layer(x, pos, ckv_past, kpe_past, conv_past, ln_mix, ln_ffn, w_gate, w_up, w_down, mix_params):
    mixed, ckv, kpe, new_conv = token_mixers(rms_norm(x, ln_mix), pos, ckv_past, kpe_past, conv_past, *mix_params)
    h = x + mixed
    f = rms_norm(h, ln_ffn)
    h = h + (jax.nn.silu(f @ w_gate) * (f @ w_up)) @ w_down
    return h, ckv, kpe, new_conv


def setup_inputs(seed: int = 0) -> dict:
    key = jax.random.key(seed)
    ks = jax.random.split(key, 32)
    n = lambda k, shape, s: jax.random.normal(k, shape, jnp.float32) * s
    gain = lambda k, shape: 1.0 + 0.05 * jax.random.normal(k, shape, jnp.float32)
    return {
        "x_prompt": n(ks[0], (BATCH, SEQ, D_MODEL), 1.0),
        "x_sample": n(ks[1], (DEC_BATCH, DEC_SEQ, D_MODEL), 1.0),
        "cache_kv_latent": n(ks[2], (DEPTH, DEC_BATCH, PAST_LEN, KV_LORA), 1.0),
        "cache_k_rope": n(ks[3], (DEPTH, DEC_BATCH, PAST_LEN, ROPE_DIM), 1.0),
        "state_conv": n(ks[4], (DEPTH, DEC_BATCH, CONV_STATE, CONV_CH), 0.5),
        "ln_mix": gain(ks[5], (DEPTH, D_MODEL)),
        "w_in": n(ks[6], (DEPTH, D_MODEL, IN_WIDTH), D_MODEL ** -0.5),
        "g_q": gain(ks[7], (DEPTH, Q_LORA)),
        "w_uq": n(ks[8], (DEPTH, Q_LORA, MLA_HEADS * (NOPE_DIM + ROPE_DIM)), Q_LORA ** -0.5),
        "g_kv": gain(ks[9], (DEPTH, KV_LORA)),
        "w_uk": n(ks[10], (DEPTH, KV_LORA, MLA_HEADS, NOPE_DIM), KV_LORA ** -0.5),
        "w_uv": n(ks[11], (DEPTH, KV_LORA, MLA_HEADS, V_DIM), KV_LORA ** -0.5),
        "w_dw": n(ks[12], (DEPTH, CONV_W, CONV_CH), CONV_W ** -0.5),
        "b_dw": n(ks[13], (DEPTH, CONV_CH), 0.02),
        "g_cn": gain(ks[14], (DEPTH, CONV_CH)),
        "b_cn": n(ks[15], (DEPTH, CONV_CH), 0.02),
        "g_om": gain(ks[16], (DEPTH, MLA_WIDTH)),
        "g_oc": gain(ks[17], (DEPTH, CONV_CH)),
        "w_out": n(ks[18], (DEPTH, MIX_WIDTH, D_MODEL), MIX_WIDTH ** -0.5),
        "ln_ffn": gain(ks[19], (DEPTH, D_MODEL)),
        "w_gate": n(ks[20], (DEPTH, D_MODEL, D_FF), D_MODEL ** -0.5),
        "w_up": n(ks[21], (DEPTH, D_MODEL, D_FF), D_MODEL ** -0.5),
        "w_down": n(ks[22], (DEPTH, D_FF, D_MODEL), D_FF ** -0.5),
        "g_final": gain(ks[23], (D_MODEL,)),
    }


def reference(x_prompt, x_sample, cache_kv_latent, cache_k_rope, state_conv,
              ln_mix, w_in, g_q, w_uq, g_kv, w_uk, w_uv, w_dw, b_dw, g_cn, b_cn, g_om, g_oc,
              w_out, ln_ffn, w_gate, w_up, w_down, g_final):
    past_len = cache_kv_latent.shape[2]
    pos_p = jnp.arange(x_prompt.shape[1])
    pos_s = past_len + jnp.arange(x_sample.shape[1])
    hp, hs = x_prompt, x_sample
    kv_p, kr_p, cv_p, kv_s, kr_s, cv_s = [], [], [], [], [], []
    for l in range(DEPTH):
        mix_params = (w_in[l], g_q[l], w_uq[l], g_kv[l], w_uk[l], w_uv[l], w_dw[l], b_dw[l],
                      g_cn[l], b_cn[l], g_om[l], g_oc[l], w_out[l])
        hp, a, b, c = encoder_layer(hp, pos_p, None, None, None, ln_mix[l], ln_ffn[l],
                                    w_gate[l], w_up[l], w_down[l], mix_params)
        kv_p.append(a); kr_p.append(b); cv_p.append(c)
        hs, a, b, c = encoder_layer(hs, pos_s, cache_kv_latent[l], cache_k_rope[l], state_conv[l],
                                    ln_mix[l], ln_ffn[l], w_gate[l], w_up[l], w_down[l], mix_params)
        kv_s.append(a); kr_s.append(b); cv_s.append(c)
    y_prompt = rms_norm(hp, g_final)
    y_sample = rms_norm(hs, g_final)
    return (y_prompt, y_sample,
            jnp.stack(kv_p), jnp.stack(kr_p), jnp.stack(cv_p),
            jnp.stack(kv_s), jnp.stack(kr_s), jnp.stack(cv_s))
```

```python
import functools
import math

import jax
import jax.numpy as jnp
from jax import lax
from jax.experimental import pallas as pl
from jax.experimental.pallas import tpu as pltpu

CHUNK = 64
MLA_HEADS = 8
NOPE_DIM = 64
ROPE_DIM = 32
V_DIM = 64
ROPE_THETA = 10000.0
EPS = 1e-6
CONV_W = 31
CONV_STATE = CONV_W - 1
ATTN_SCALE = 1.0 / math.sqrt(NOPE_DIM + ROPE_DIM)
MASK_VALUE = -1e30

LANES = 128
HALF_ROPE = ROPE_DIM // 2
ROPE_LO = NOPE_DIM
ROPE_MID = ROPE_LO + HALF_ROPE
ROPE_HI = ROPE_LO + ROPE_DIM
HALO_ROWS = 32
HALO_PAD = HALO_ROWS - CONV_STATE
Q_SCALE = ATTN_SCALE * math.log2(math.e)

ROW_TILE = 512
ATTN_TILE = 256
VMEM_LIMIT_BYTES = 56 * 1024 * 1024

_BF16 = jnp.bfloat16
_F32 = jnp.float32


def _rms(x, g):
    return x * lax.rsqrt(jnp.mean(x * x, axis=-1, keepdims=True) + EPS) * g


def _dot(a, b):
    return jnp.dot(a, b, preferred_element_type=_F32)


def _dot_nt(a, b):
    return lax.dot_general(a, b, (((1,), (1,)), ((), ())), preferred_element_type=_F32)


def _rope(x, cosf, sin_lo, sin_hi):
    return (x * cosf + pltpu.roll(x, LANES - HALF_ROPE, 1) * sin_lo
            + pltpu.roll(x, HALF_ROPE, 1) * sin_hi)


def _rope_table_kernel(pos_ref, inv_ref, cos_ref, sin_lo_ref, sin_hi_ref):
    lane = lax.broadcasted_iota(jnp.int32, cos_ref.shape, 1)
    ang = pos_ref[...] * inv_ref[...]
    c = jnp.cos(ang)
    s = jnp.sin(ang)
    cos_ref[...] = jnp.where(lane < ROPE_LO, 1.0, jnp.where(lane < ROPE_HI, c, 0.0))
    sin_lo_ref[...] = jnp.where((lane >= ROPE_LO) & (lane < ROPE_MID), -s, 0.0)
    sin_hi_ref[...] = jnp.where((lane >= ROPE_MID) & (lane < ROPE_HI), s, 0.0)


def _rope_tables(pos):
    inv = 1.0 / (ROPE_THETA ** (jnp.arange(0, ROPE_DIM, 2, dtype=_F32) / ROPE_DIM))
    inv_lanes = jnp.zeros((1, LANES), _F32).at[0, ROPE_LO:ROPE_HI].set(jnp.tile(inv, 2))
    out = jax.ShapeDtypeStruct((pos.shape[0], LANES), _F32)
    return pl.pallas_call(
        _rope_table_kernel,
        out_shape=(out, out, out),
        name="rope_table",
    )(pos.astype(_F32).reshape(-1, 1), inv_lanes)


def _in_proj(x_ref, ln_mix_ref, w_in_ref, widths):
    hn = _rms(x_ref[...], ln_mix_ref[...]).astype(_BF16)
    outs, start = [], 0
    for w in widths:
        outs.append(_dot(hn, w_in_ref[:, start:start + w]))
        start += w
    return outs


def _store_heads(ref, x):
    for h in range(MLA_HEADS):
        ref[h] = x[:, h * LANES:(h + 1) * LANES].astype(ref.dtype)


def _query(cq, g_q_ref, w_uq_ref, rope_tabs, q_ref):
    q = _dot(_rms(cq, g_q_ref[...]).astype(_BF16), w_uq_ref[...])
    for h in range(MLA_HEADS):
        qh = _rope(q[:, h * LANES:(h + 1) * LANES], *rope_tabs)
        q_ref[h] = (qh * Q_SCALE).astype(q_ref.dtype)


def _depthwise_conv(ubuf, rows, w_dw_ref, b_dw_ref):
    acc = None
    for k in range(CONV_W):
        term = ubuf[:, HALO_PAD + k:HALO_PAD + k + rows, :] * w_dw_ref[k:k + 1, :]
        acc = term if acc is None else acc + term
    return acc + b_dw_ref[...]


def _conv_branch(dw, g_cn_ref, b_cn_ref, g_oc_ref):
    mu = jnp.mean(dw, axis=-1, keepdims=True)
    xc = dw - mu
    y = xc * lax.rsqrt(jnp.mean(xc * xc, axis=-1, keepdims=True) + EPS) * g_cn_ref[...] + b_cn_ref[...]
    return _rms(y * jax.nn.sigmoid(y), g_oc_ref[...])


def _prompt_proj_kernel(x_ref, cos_ref, sin_lo_ref, sin_hi_ref, ln_mix_ref, w_in_ref, g_q_ref, w_uq_ref,
                        g_kv_ref, w_uk_ref, w_uv_ref, v_one_ref, w_dw_ref, b_dw_ref, g_cn_ref, b_cn_ref,
                        g_oc_ref,
                        q_ref, k_ref, v_ref, ckv_ref, kpe_ref, cn_ref, ncv_ref, ubuf, *, tiles_per_seq):
    rows = x_ref.shape[0]
    q_lora, kv_lora, conv_ch = g_q_ref.shape[1], g_kv_ref.shape[1], g_cn_ref.shape[1]
    cq, ckv, kpe, a, gate = _in_proj(x_ref, ln_mix_ref, w_in_ref, (q_lora, kv_lora, LANES, conv_ch, conv_ch))
    rope_tabs = (cos_ref[...], sin_lo_ref[...], sin_hi_ref[...])

    _query(cq, g_q_ref, w_uq_ref, rope_tabs, q_ref)

    ckv = _rms(ckv, g_kv_ref[...])
    ckv_ref[...] = ckv
    ckv_b = ckv.astype(_BF16)
    kpe = _rope(kpe, *rope_tabs)
    kpe_ref[...] = kpe
    k_nope = _dot(ckv_b, w_uk_ref[...])
    for h in range(MLA_HEADS):
        k_ref[h] = (k_nope[:, h * LANES:(h + 1) * LANES] + kpe).astype(k_ref.dtype)
    _store_heads(v_ref, _dot(ckv_b, w_uv_ref[...]) + v_one_ref[...])

    @pl.when(lax.rem(pl.program_id(0), tiles_per_seq) == 0)
    def _():
        ubuf[:, 0:HALO_ROWS, :] = jnp.zeros((1, HALO_ROWS, conv_ch), _F32)

    ubuf[0, HALO_ROWS:HALO_ROWS + rows, :] = a * jax.nn.sigmoid(gate)
    dw = _depthwise_conv(ubuf, rows, w_dw_ref, b_dw_ref)[0]
    cn_ref[...] = _conv_branch(dw, g_cn_ref, b_cn_ref, g_oc_ref).astype(cn_ref.dtype)
    history = ubuf[0, HALO_PAD + rows:HALO_ROWS + rows, :]
    ncv_ref[0] = history
    ubuf[0, HALO_PAD:HALO_ROWS, :] = history


def _sample_proj_kernel(x_ref, cos_ref, sin_lo_ref, sin_hi_ref, state_ref, ln_mix_ref, w_in_ref, g_q_ref,
                        w_uq_ref, g_kv_ref, w_dw_ref, b_dw_ref, g_cn_ref, b_cn_ref, g_oc_ref,
                        q_ref, ckv_ref, kpe_ref, cn_ref, ncv_ref, ubuf):
    segs, seg_rows = ubuf.shape[0], ubuf.shape[1] - HALO_ROWS
    q_lora, kv_lora, conv_ch = g_q_ref.shape[1], g_kv_ref.shape[1], g_cn_ref.shape[1]
    cq, ckv, kpe, a, gate = _in_proj(x_ref, ln_mix_ref, w_in_ref, (q_lora, kv_lora, LANES, conv_ch, conv_ch))
    rope_tabs = (cos_ref[...], sin_lo_ref[...], sin_hi_ref[...])

    _query(cq, g_q_ref, w_uq_ref, rope_tabs, q_ref)
    ckv_ref[...] = _rms(ckv, g_kv_ref[...])
    kpe_ref[...] = _rope(kpe, *rope_tabs)

    ubuf[:, HALO_PAD:HALO_ROWS, :] = state_ref[...]
    ubuf[:, HALO_ROWS:, :] = (a * jax.nn.sigmoid(gate)).reshape(segs, seg_rows, conv_ch)
    dw = _depthwise_conv(ubuf, seg_rows, w_dw_ref, b_dw_ref).reshape(segs * seg_rows, conv_ch)
    cn_ref[...] = _conv_branch(dw, g_cn_ref, b_cn_ref, g_oc_ref).astype(cn_ref.dtype)
    ncv_ref[...] = ubuf[:, HALO_PAD + seg_rows:, :]


def _whole(shape):
    zeros = (0,) * len(shape)
    return pl.BlockSpec(shape, lambda *_: zeros)


def _prompt_proj(x, tabs, wts, seq):
    n, d = x.shape
    tm = min(ROW_TILE, seq)
    tiles_per_seq = seq // tm
    conv_ch = wts["g_cn"].shape[1]
    kv_lora = wts["g_kv"].shape[1]
    row_block = lambda w: pl.BlockSpec((tm, w), lambda i: (i, 0))
    head_block = pl.BlockSpec((MLA_HEADS, tm, LANES), lambda i: (0, i, 0))
    tab_block = pl.BlockSpec((tm, LANES), lambda i: (lax.rem(i, tiles_per_seq), 0))
    names = ("ln_mix", "w_in", "g_q", "w_uq", "g_kv", "w_uk", "w_uv", "v_one", "w_dw", "b_dw", "g_cn",
             "b_cn", "g_oc")
    head_shape = jax.ShapeDtypeStruct((MLA_HEADS, n, LANES), _BF16)
    return pl.pallas_call(
        functools.partial(_prompt_proj_kernel, tiles_per_seq=tiles_per_seq),
        grid=(n // tm,),
        in_specs=[row_block(d), tab_block, tab_block, tab_block] + [_whole(wts[k].shape) for k in names],
        out_specs=(head_block, head_block, head_block, row_block(kv_lora), row_block(LANES),
                   row_block(conv_ch), pl.BlockSpec((1, CONV_STATE, conv_ch), lambda i: (i // tiles_per_seq, 0, 0))),
        out_shape=(head_shape, head_shape, head_shape,
                   jax.ShapeDtypeStruct((n, kv_lora), _F32), jax.ShapeDtypeStruct((n, LANES), _F32),
                   jax.ShapeDtypeStruct((n, conv_ch), _BF16),
                   jax.ShapeDtypeStruct((n // seq, CONV_STATE, conv_ch), _F32)),
        scratch_shapes=[pltpu.VMEM((1, HALO_ROWS + tm, conv_ch), _F32)],
        compiler_params=pltpu.CompilerParams(dimension_semantics=("arbitrary",),
                                             vmem_limit_bytes=VMEM_LIMIT_BYTES),
        name="prompt_proj",
    )(x, *tabs, *[wts[k] for k in names])


def _sample_proj(x, tabs, state, wts, seq):
    n, d = x.shape
    conv_ch = wts["g_cn"].shape[1]
    kv_lora = wts["g_kv"].shape[1]
    names = ("ln_mix", "w_in", "g_q", "w_uq", "g_kv", "w_dw", "b_dw", "g_cn", "b_cn", "g_oc")
    return pl.pallas_call(
        _sample_proj_kernel,
        out_shape=(jax.ShapeDtypeStruct((MLA_HEADS, n, LANES), _BF16),
                   jax.ShapeDtypeStruct((n, kv_lora), _F32), jax.ShapeDtypeStruct((n, LANES), _F32),
                   jax.ShapeDtypeStruct((n, conv_ch), _BF16),
                   jax.ShapeDtypeStruct((n // seq, CONV_STATE, conv_ch), _F32)),
        scratch_shapes=[pltpu.VMEM((n // seq, HALO_ROWS + seq, conv_ch), _F32)],
        compiler_params=pltpu.CompilerParams(vmem_limit_bytes=VMEM_LIMIT_BYTES),
        name="sample_proj",
    )(x, *tabs, state, *[wts[k] for k in names])


def _merge_heads(accs):
    lane = lax.broadcasted_iota(jnp.int32, accs[0].shape, 1)
    pairs = []
    for j in range(0, MLA_HEADS, 2):
        even, odd = accs[j], accs[j + 1]
        pairs.append(jnp.where(lane < V_DIM, even / even[:, V_DIM:V_DIM + 1], odd / odd[:, 0:1]))
    return jnp.concatenate(pairs, axis=-1)


def _prompt_attn_kernel(q_ref, k_ref, v_ref, g_om_ref, o_ref):
    tile = q_ref.shape[1]
    i = pl.program_id(1)
    row = lax.broadcasted_iota(jnp.int32, (tile, tile), 0)
    col = lax.broadcasted_iota(jnp.int32, (tile, tile), 1)
    chunk_of = lambda t: lax.shift_right_logical(t, CHUNK.bit_length() - 1)
    diag_mask = chunk_of(col) <= chunk_of(row)

    accs = []
    for h in range(MLA_HEADS):
        qh = q_ref[h]

        def kv_tile(j, h=h):
            rows = pl.ds(pl.multiple_of(j * tile, tile), tile)
            return k_ref[h, rows, :], v_ref[h, rows, :]

        kj, vj = kv_tile(i)
        s = jnp.where(diag_mask, _dot_nt(qh, kj), MASK_VALUE)
        m = jnp.max(s, axis=-1, keepdims=True)
        acc = _dot(jnp.exp2(s - m).astype(_BF16), vj)

        def body(j, carry, qh=qh, kv_tile=kv_tile):
            m, acc = carry
            kj, vj = kv_tile(j)
            s = _dot_nt(qh, kj)
            m_new = jnp.maximum(m, jnp.max(s, axis=-1, keepdims=True))
            acc = acc * jnp.exp2(m - m_new) + _dot(jnp.exp2(s - m_new).astype(_BF16), vj)
            return m_new, acc

        _, acc = lax.fori_loop(0, i, body, (m, acc))
        accs.append(acc)

    o_ref[...] = _rms(_merge_heads(accs), g_om_ref[...]).astype(o_ref.dtype)


def _prompt_attn(q, k, v, g_om, batch, seq):
    tile = min(ATTN_TILE, seq)
    nq = seq // tile
    width = g_om.shape[1]
    return pl.pallas_call(
        _prompt_attn_kernel,
        grid=(batch, nq),
        in_specs=[pl.BlockSpec((MLA_HEADS, tile, LANES), lambda b, i: (0, b * nq + i, 0)),
                  pl.BlockSpec((MLA_HEADS, seq, LANES), lambda b, i: (0, b, 0)),
                  pl.BlockSpec((MLA_HEADS, seq, LANES), lambda b, i: (0, b, 0)),
                  _whole(g_om.shape)],
        out_specs=pl.BlockSpec((tile, width), lambda b, i: (b * nq + i, 0)),
        out_shape=jax.ShapeDtypeStruct((batch * seq, width), _BF16),
        compiler_params=pltpu.CompilerParams(dimension_semantics=("arbitrary", "arbitrary"),
                                             vmem_limit_bytes=VMEM_LIMIT_BYTES),
        name="prompt_attn",
    )(q, k, v, g_om)


def _sample_attn_kernel(q_ref, ckv_new_ref, kpe_new_ref, ckv_past_ref, kpe_past_ref, w_ukt_ref, w_uvh_ref,
                        g_om_ref, o_ref):
    seq = q_ref.shape[1]
    q_all = jnp.concatenate([q_ref[h] for h in range(MLA_HEADS)], axis=0)
    q_lat = jnp.concatenate([_dot(q_ref[h], w_ukt_ref[h]) for h in range(MLA_HEADS)], axis=0).astype(_BF16)
    q_pe = q_all[:, ROPE_LO:ROPE_HI]
    c_past = ckv_past_ref[0].astype(_BF16)
    c_new = ckv_new_ref[...].astype(_BF16)
    s_past = _dot_nt(q_lat, c_past) + _dot_nt(q_pe, kpe_past_ref[0].astype(_BF16))
    s_new = _dot_nt(q_lat, c_new) + _dot_nt(q_all, kpe_new_ref[...].astype(_BF16))
    m = jnp.maximum(jnp.max(s_past, axis=-1, keepdims=True), jnp.max(s_new, axis=-1, keepdims=True))
    p_past = jnp.exp2(s_past - m)
    p_new = jnp.exp2(s_new - m)
    denom = jnp.sum(p_past, axis=-1, keepdims=True) + jnp.sum(p_new, axis=-1, keepdims=True)
    o_lat = ((_dot(p_past.astype(_BF16), c_past) + _dot(p_new.astype(_BF16), c_new)) / denom).astype(_BF16)
    o = sum(_dot(o_lat[h * seq:(h + 1) * seq], w_uvh_ref[h]) for h in range(MLA_HEADS))
    o_ref[...] = _rms(o, g_om_ref[...]).astype(o_ref.dtype)


def _sample_attn(q, ckv_new, kpe_new, ckv_past, kpe_past, w_ukt, w_uvh, g_om, batch, seq):
    past, kv_lora = ckv_past.shape[1:]
    width = g_om.shape[1]
    return pl.pallas_call(
        _sample_attn_kernel,
        grid=(batch,),
        in_specs=[pl.BlockSpec((MLA_HEADS, seq, LANES), lambda b: (0, b, 0)),
                  pl.BlockSpec((seq, kv_lora), lambda b: (b, 0)),
                  pl.BlockSpec((seq, LANES), lambda b: (b, 0)),
                  pl.BlockSpec((1, past, kv_lora), lambda b: (b, 0, 0)),
                  pl.BlockSpec((1, past, ROPE_DIM), lambda b: (b, 0, 0)),
                  _whole(w_ukt.shape), _whole(w_uvh.shape), _whole(g_om.shape)],
        out_specs=pl.BlockSpec((seq, width), lambda b: (b, 0)),
        out_shape=jax.ShapeDtypeStruct((batch * seq, width), _BF16),
        compiler_params=pltpu.CompilerParams(dimension_semantics=("arbitrary",),
                                             vmem_limit_bytes=VMEM_LIMIT_BYTES),
        name="sample_attn",
    )(q, ckv_new, kpe_new, ckv_past, kpe_past, w_ukt, w_uvh, g_om)


def _output_kernel(x_ref, an_ref, cn_ref, w_out_a_ref, w_out_c_ref, ln_ffn_ref, w_gate_ref, w_up_ref,
                   w_down_ref, g_final_ref, y_ref):
    h = x_ref[...] + _dot(an_ref[...], w_out_a_ref[...]) + _dot(cn_ref[...], w_out_c_ref[...])
    f = _rms(h, ln_ffn_ref[...]).astype(_BF16)
    gate = _dot(f, w_gate_ref[...])
    act = (gate * jax.nn.sigmoid(gate) * _dot(f, w_up_ref[...])).astype(_BF16)
    h = h + _dot(act, w_down_ref[...])
    y_ref[...] = _rms(h, g_final_ref[...])


def _output(x, an, cn, wts, name):
    n, d = x.shape
    tm = min(ROW_TILE, n)
    names = ("w_out_a", "w_out_c", "ln_ffn", "w_gate", "w_up", "w_down", "g_final")
    row_block = lambda w: pl.BlockSpec((tm, w), lambda i: (i, 0))
    resident = lambda shape: pl.BlockSpec(shape, lambda i: (0,) * len(shape), pipeline_mode=pl.Buffered(1))
    return pl.pallas_call(
        _output_kernel,
        grid=(n // tm,),
        in_specs=[row_block(d), row_block(an.shape[1]), row_block(cn.shape[1])]
                 + [resident(wts[k].shape) for k in names],
        out_specs=row_block(d),
        out_shape=jax.ShapeDtypeStruct((n, d), _F32),
        compiler_params=pltpu.CompilerParams(dimension_semantics=("arbitrary",),
                                             vmem_limit_bytes=VMEM_LIMIT_BYTES),
        name=name,
    )(x, an, cn, *[wts[k] for k in names])


def _pad_lanes(w, left):
    return jnp.pad(w, [(0, 0)] * (w.ndim - 1) + [(left, LANES - left - w.shape[-1])])


def _prepare_weights(ln_mix, w_in, g_q, w_uq, g_kv, w_uk, w_uv, w_dw, b_dw, g_cn, b_cn, g_om, g_oc, w_out,
                     ln_ffn, w_gate, w_up, w_down, g_final):
    q_lora, kv_lora = g_q.shape[0], g_kv.shape[0]
    d_model = w_in.shape[0]
    mla_width = g_om.shape[0]
    row = lambda v: v.reshape(1, -1)
    c0, c1, c2 = q_lora, q_lora + kv_lora, q_lora + kv_lora + ROPE_DIM
    w_in_p = jnp.concatenate([w_in[:, :c1], _pad_lanes(w_in[:, c1:c2], ROPE_LO), w_in[:, c2:]], axis=1)
    w_uq_p = _pad_lanes(w_uq.reshape(q_lora, MLA_HEADS, NOPE_DIM + ROPE_DIM), 0).reshape(q_lora, -1)
    w_uk_p = _pad_lanes(w_uk, 0).reshape(kv_lora, -1)
    w_uv_p = jnp.stack([_pad_lanes(w_uv[:, h], (h % 2) * V_DIM) for h in range(MLA_HEADS)], axis=1)
    v_one = jnp.stack([jnp.zeros((LANES,), _F32).at[V_DIM * (1 - h % 2)].set(1.0) for h in range(MLA_HEADS)])
    w_ukt = jnp.pad(jnp.transpose(w_uk, (1, 2, 0)), ((0, 0), (0, LANES - NOPE_DIM), (0, 0)))
    w_uvh = jnp.stack([jnp.pad(w_uv[:, h], ((0, 0), (h * V_DIM, (MLA_HEADS - 1 - h) * V_DIM)))
                       for h in range(MLA_HEADS)])
    return {
        "ln_mix": row(ln_mix), "w_in": w_in_p.astype(_BF16), "g_q": row(g_q), "w_uq": w_uq_p.astype(_BF16),
        "g_kv": row(g_kv), "w_uk": w_uk_p.astype(_BF16), "w_uv": w_uv_p.reshape(kv_lora, -1).astype(_BF16),
        "v_one": v_one.reshape(1, -1), "w_dw": w_dw, "b_dw": row(b_dw), "g_cn": row(g_cn), "b_cn": row(b_cn),
        "g_om": row(g_om), "g_oc": row(g_oc),
        "w_ukt": w_ukt.astype(_BF16), "w_uvh": w_uvh.astype(_BF16),
        "w_out_a": w_out[:mla_width].astype(_BF16), "w_out_c": w_out[mla_width:].astype(_BF16),
        "ln_ffn": row(ln_ffn), "w_gate": w_gate.astype(_BF16), "w_up": w_up.astype(_BF16),
        "w_down": w_down.astype(_BF16), "g_final": row(g_final),
    }


def _layer(x_prompt, x_sample, ckv_past, kpe_past, conv_past, wts):
    batch, seq, d = x_prompt.shape
    dec_batch, dec_seq, _ = x_sample.shape
    past = ckv_past.shape[1]

    xp = x_prompt.reshape(batch * seq, d)
    q, k, v, kv_p, kr_p, cn, cv_p = _prompt_proj(xp, _rope_tables(jnp.arange(seq)), wts, seq)
    an = _prompt_attn(q, k, v, wts["g_om"], batch, seq)
    y_p = _output(xp, an, cn, wts, "prompt_output")

    xs = x_sample.reshape(dec_batch * dec_seq, d)
    tabs = _rope_tables(jnp.tile(past + jnp.arange(dec_seq), dec_batch))
    q, kv_s, kr_s, cn, cv_s = _sample_proj(xs, tabs, conv_past, wts, dec_seq)
    an = _sample_attn(q, kv_s, kr_s, ckv_past, kpe_past, wts["w_ukt"], wts["w_uvh"], wts["g_om"],
                      dec_batch, dec_seq)
    y_s = _output(xs, an, cn, wts, "sample_output")

    rope = lambda t, b, s: t[:, ROPE_LO:ROPE_HI].reshape(b, s, ROPE_DIM)
    return (y_p.reshape(batch, seq, d), y_s.reshape(dec_batch, dec_seq, d),
            kv_p.reshape(batch, seq, -1), rope(kr_p, batch, seq), cv_p,
            kv_s.reshape(dec_batch, dec_seq, -1), rope(kr_s, dec_batch, dec_seq), cv_s)


def kernel(x_prompt, x_sample, cache_kv_latent, cache_k_rope, state_conv, ln_mix, w_in, g_q, w_uq, g_kv, w_uk, w_uv, w_dw, b_dw, g_cn, b_cn, g_om, g_oc, w_out, ln_ffn, w_gate, w_up, w_down, g_final):
    depth = w_in.shape[0]
    assert depth == 1, "the kernel implements the single-layer model of the problem"
    wts = _prepare_weights(ln_mix[0], w_in[0], g_q[0], w_uq[0], g_kv[0], w_uk[0], w_uv[0], w_dw[0], b_dw[0],
                           g_cn[0], b_cn[0], g_om[0], g_oc[0], w_out[0], ln_ffn[0], w_gate[0], w_up[0],
                           w_down[0], g_final)
    outs = _layer(x_prompt, x_sample, cache_kv_latent[0], cache_k_rope[0], state_conv[0], wts)
    y_p, y_s = outs[0], outs[1]
    return (y_p, y_s) + tuple(o[None] for o in outs[2:])
```

```python
import functools
import math

import jax
import jax.numpy as jnp
from jax import lax
from jax.experimental import pallas as pl
from jax.experimental.pallas import tpu as pltpu

CHUNK = 64
MLA_HEADS = 8
NOPE_DIM = 64
ROPE_DIM = 32
V_DIM = 64
ROPE_THETA = 10000.0
EPS = 1e-6
CONV_W = 31
CONV_STATE = CONV_W - 1
ATTN_SCALE = 1.0 / math.sqrt(NOPE_DIM + ROPE_DIM)
MASK_VALUE = -1e30

LANES = 128
HALF_ROPE = ROPE_DIM // 2
ROPE_LO = NOPE_DIM
ROPE_MID = ROPE_LO + HALF_ROPE
ROPE_HI = ROPE_LO + ROPE_DIM
HALO_ROWS = 32
HALO_PAD = HALO_ROWS - CONV_STATE
Q_SCALE = ATTN_SCALE * math.log2(math.e)

ROW_TILE = 512
ATTN_TILE = 256
VMEM_LIMIT_BYTES = 56 * 1024 * 1024

_BF16 = jnp.bfloat16
_F32 = jnp.float32


def _rms(x, g):
    return x * lax.rsqrt(jnp.mean(x * x, axis=-1, keepdims=True) + EPS) * g


def _dot(a, b):
    return jnp.dot(a, b, preferred_element_type=_F32)


def _dot_nt(a, b):
    return lax.dot_general(a, b, (((1,), (1,)), ((), ())), preferred_element_type=_F32)


def _rope(x, cosf, sin_lo, sin_hi):
    return (x * cosf + pltpu.roll(x, LANES - HALF_ROPE, 1) * sin_lo
            + pltpu.roll(x, HALF_ROPE, 1) * sin_hi)


def _rope_table_kernel(pos_ref, inv_ref, cos_ref, sin_lo_ref, sin_hi_ref):
    lane = lax.broadcasted_iota(jnp.int32, cos_ref.shape, 1)
    ang = pos_ref[...] * inv_ref[...]
    c = jnp.cos(ang)
    s = jnp.sin(ang)
    cos_ref[...] = jnp.where(lane < ROPE_LO, 1.0, jnp.where(lane < ROPE_HI, c, 0.0))
    sin_lo_ref[...] = jnp.where((lane >= ROPE_LO) & (lane < ROPE_MID), -s, 0.0)
    sin_hi_ref[...] = jnp.where((lane >= ROPE_MID) & (lane < ROPE_HI), s, 0.0)


def _rope_tables(pos):
    inv = 1.0 / (ROPE_THETA ** (jnp.arange(0, ROPE_DIM, 2, dtype=_F32) / ROPE_DIM))
    inv_lanes = jnp.zeros((1, LANES), _F32).at[0, ROPE_LO:ROPE_HI].set(jnp.tile(inv, 2))
    out = jax.ShapeDtypeStruct((pos.shape[0], LANES), _F32)
    return pl.pallas_call(
        _rope_table_kernel,
        out_shape=(out, out, out),
        name="rope_table",
    )(pos.astype(_F32).reshape(-1, 1), inv_lanes)


def _in_proj(x_ref, ln_mix_ref, w_in_ref, widths):
    hn = _rms(x_ref[...], ln_mix_ref[...]).astype(_BF16)
    outs, start = [], 0
    for w in widths:
        outs.append(_dot(hn, w_in_ref[:, start:start + w]))
        start += w
    return outs


def _store_heads(ref, x):
    for h in range(MLA_HEADS):
        ref[h] = x[:, h * LANES:(h + 1) * LANES].astype(ref.dtype)


def _query(cq, g_q_ref, w_uq_ref, rope_tabs, q_ref):
    q = _dot(_rms(cq, g_q_ref[...]).astype(_BF16), w_uq_ref[...])
    for h in range(MLA_HEADS):
        qh = _rope(q[:, h * LANES:(h + 1) * LANES], *rope_tabs)
        q_ref[h] = (qh * Q_SCALE).astype(q_ref.dtype)


def _depthwise_conv(ubuf, rows, w_dw_ref, b_dw_ref):
    acc = None
    for k in range(CONV_W):
        term = ubuf[:, HALO_PAD + k:HALO_PAD + k + rows, :] * w_dw_ref[k:k + 1, :]
        acc = term if acc is None else acc + term
    return acc + b_dw_ref[...]


def _conv_branch(dw, g_cn_ref, b_cn_ref, g_oc_ref):
    mu = jnp.mean(dw, axis=-1, keepdims=True)
    xc = dw - mu
    y = xc * lax.rsqrt(jnp.mean(xc * xc, axis=-1, keepdims=True) + EPS) * g_cn_ref[...] + b_cn_ref[...]
    return _rms(y * jax.nn.sigmoid(y), g_oc_ref[...])


def _prompt_proj_kernel(x_ref, cos_ref, sin_lo_ref, sin_hi_ref, ln_mix_ref, w_in_ref, g_q_ref, w_uq_ref,
                        g_kv_ref, w_uk_ref, w_uv_ref, v_one_ref, w_dw_ref, b_dw_ref, g_cn_ref, b_cn_ref,
                        g_oc_ref,
                        q_ref, k_ref, v_ref, ckv_ref, kpe_ref, cn_ref, ncv_ref, ubuf, *, tiles_per_seq):
    rows = x_ref.shape[0]
    q_lora, kv_lora, conv_ch = g_q_ref.shape[1], g_kv_ref.shape[1], g_cn_ref.shape[1]
    cq, ckv, kpe, a, gate = _in_proj(x_ref, ln_mix_ref, w_in_ref, (q_lora, kv_lora, LANES, conv_ch, conv_ch))
    rope_tabs = (cos_ref[...], sin_lo_ref[...], sin_hi_ref[...])

    _query(cq, g_q_ref, w_uq_ref, rope_tabs, q_ref)

    ckv = _rms(ckv, g_kv_ref[...])
    ckv_ref[...] = ckv
    ckv_b = ckv.astype(_BF16)
    kpe = _rope(kpe, *rope_tabs)
    kpe_ref[...] = kpe
    k_nope = _dot(ckv_b, w_uk_ref[...])
    for h in range(MLA_HEADS):
        k_ref[h] = (k_nope[:, h * LANES:(h + 1) * LANES] + kpe).astype(k_ref.dtype)
    _store_heads(v_ref, _dot(ckv_b, w_uv_ref[...]) + v_one_ref[...])

    @pl.when(lax.rem(pl.program_id(0), tiles_per_seq) == 0)
    def _():
        ubuf[:, 0:HALO_ROWS, :] = jnp.zeros((1, HALO_ROWS, conv_ch), _F32)

    ubuf[0, HALO_ROWS:HALO_ROWS + rows, :] = a * jax.nn.sigmoid(gate)
    dw = _depthwise_conv(ubuf, rows, w_dw_ref, b_dw_ref)[0]
    cn_ref[...] = _conv_branch(dw, g_cn_ref, b_cn_ref, g_oc_ref).astype(cn_ref.dtype)
    history = ubuf[0, HALO_PAD + rows:HALO_ROWS + rows, :]
    ncv_ref[0] = history
    ubuf[0, HALO_PAD:HALO_ROWS, :] = history


def _sample_proj_kernel(x_ref, cos_ref, sin_lo_ref, sin_hi_ref, state_ref, ln_mix_ref, w_in_ref, g_q_ref,
                        w_uq_ref, g_kv_ref, w_dw_ref, b_dw_ref, g_cn_ref, b_cn_ref, g_oc_ref,
                        q_ref, ckv_ref, kpe_ref, cn_ref, ncv_ref, ubuf):
    segs, seg_rows = ubuf.shape[0], ubuf.shape[1] - HALO_ROWS
    q_lora, kv_lora, conv_ch = g_q_ref.shape[1], g_kv_ref.shape[1], g_cn_ref.shape[1]
    cq, ckv, kpe, a, gate = _in_proj(x_ref, ln_mix_ref, w_in_ref, (q_lora, kv_lora, LANES, conv_ch, conv_ch))
    rope_tabs = (cos_ref[...], sin_lo_ref[...], sin_hi_ref[...])

    _query(cq, g_q_ref, w_uq_ref, rope_tabs, q_ref)
    ckv_ref[...] = _rms(ckv, g_kv_ref[...])
    kpe_ref[...] = _rope(kpe, *rope_tabs)

    ubuf[:, HALO_PAD:HALO_ROWS, :] = state_ref[...]
    ubuf[:, HALO_ROWS:, :] = (a * jax.nn.sigmoid(gate)).reshape(segs, seg_rows, conv_ch)
    dw = _depthwise_conv(ubuf, seg_rows, w_dw_ref, b_dw_ref).reshape(segs * seg_rows, conv_ch)
    cn_ref[...] = _conv_branch(dw, g_cn_ref, b_cn_ref, g_oc_ref).astype(cn_ref.dtype)
    ncv_ref[...] = ubuf[:, HALO_PAD + seg_rows:, :]


def _whole(shape):
    zeros = (0,) * len(shape)
    return pl.BlockSpec(shape, lambda *_: zeros)


def _prompt_proj(x, tabs, wts, seq):
    n, d = x.shape
    tm = min(ROW_TILE, seq)
    tiles_per_seq = seq // tm
    conv_ch = wts["g_cn"].shape[1]
    kv_lora = wts["g_kv"].shape[1]
    row_block = lambda w: pl.BlockSpec((tm, w), lambda i: (i, 0))
    head_block = pl.BlockSpec((MLA_HEADS, tm, LANES), lambda i: (0, i, 0))
    tab_block = pl.BlockSpec((tm, LANES), lambda i: (lax.rem(i, tiles_per_seq), 0))
    names = ("ln_mix", "w_in", "g_q", "w_uq", "g_kv", "w_uk", "w_uv", "v_one", "w_dw", "b_dw", "g_cn",
             "b_cn", "g_oc")
    head_shape = jax.ShapeDtypeStruct((MLA_HEADS, n, LANES), _BF16)
    return pl.pallas_call(
        functools.partial(_prompt_proj_kernel, tiles_per_seq=tiles_per_seq),
        grid=(n // tm,),
        in_specs=[row_block(d), tab_block, tab_block, tab_block] + [_whole(wts[k].shape) for k in names],
        out_specs=(head_block, head_block, head_block, row_block(kv_lora), row_block(LANES),
                   row_block(conv_ch), pl.BlockSpec((1, CONV_STATE, conv_ch), lambda i: (i // tiles_per_seq, 0, 0))),
        out_shape=(head_shape, head_shape, head_shape,
                   jax.ShapeDtypeStruct((n, kv_lora), _F32), jax.ShapeDtypeStruct((n, LANES), _F32),
                   jax.ShapeDtypeStruct((n, conv_ch), _BF16),
                   jax.ShapeDtypeStruct((n // seq, CONV_STATE, conv_ch), _F32)),
        scratch_shapes=[pltpu.VMEM((1, HALO_ROWS + tm, conv_ch), _F32)],
        compiler_params=pltpu.CompilerParams(dimension_semantics=("arbitrary",),
                                             vmem_limit_bytes=VMEM_LIMIT_BYTES),
        name="prompt_proj",
    )(x, *tabs, *[wts[k] for k in names])


def _sample_proj(x, tabs, state, wts, seq):
    n, d = x.shape
    conv_ch = wts["g_cn"].shape[1]
    kv_lora = wts["g_kv"].shape[1]
    names = ("ln_mix", "w_in", "g_q", "w_uq", "g_kv", "w_dw", "b_dw", "g_cn", "b_cn", "g_oc")
    return pl.pallas_call(
        _sample_proj_kernel,
        out_shape=(jax.ShapeDtypeStruct((MLA_HEADS, n, LANES), _BF16),
                   jax.ShapeDtypeStruct((n, kv_lora), _F32), jax.ShapeDtypeStruct((n, LANES), _F32),
                   jax.ShapeDtypeStruct((n, conv_ch), _BF16),
                   jax.ShapeDtypeStruct((n // seq, CONV_STATE, conv_ch), _F32)),
        scratch_shapes=[pltpu.VMEM((n // seq, HALO_ROWS + seq, conv_ch), _F32)],
        compiler_params=pltpu.CompilerParams(vmem_limit_bytes=VMEM_LIMIT_BYTES),
        name="sample_proj",
    )(x, *tabs, state, *[wts[k] for k in names])


def _merge_heads(accs):
    lane = lax.broadcasted_iota(jnp.int32, accs[0].shape, 1)
    pairs = []
    for j in range(0, MLA_HEADS, 2):
        even, odd = accs[j], accs[j + 1]
        pairs.append(jnp.where(lane < V_DIM, even / even[:, V_DIM:V_DIM + 1], odd / odd[:, 0:1]))
    return jnp.concatenate(pairs, axis=-1)


def _prompt_attn_kernel(q_ref, k_ref, v_ref, g_om_ref, o_ref):
    tile = q_ref.shape[1]
    i = pl.program_id(1)
    row = lax.broadcasted_iota(jnp.int32, (tile, tile), 0)
    col = lax.broadcasted_iota(jnp.int32, (tile, tile), 1)
    chunk_of = lambda t: lax.shift_right_logical(t, CHUNK.bit_length() - 1)
    diag_mask = chunk_of(col) <= chunk_of(row)

    def key_rows(j):
        return pl.ds(pl.multiple_of(j * tile, tile), tile)

    maxes, accs = [], []
    for h in range(MLA_HEADS):
        s = jnp.where(diag_mask, _dot_nt(q_ref[h], k_ref[h, key_rows(i), :]), MASK_VALUE)
        m = jnp.max(s, axis=-1, keepdims=True)
        maxes.append(m)
        accs.append(_dot(jnp.exp2(s - m).astype(_BF16), v_ref[h, key_rows(i), :]))

    def body(j, carry):
        maxes, accs = carry
        new_maxes, new_accs = [], []
        for h in range(MLA_HEADS):
            s = _dot_nt(q_ref[h], k_ref[h, key_rows(j), :])
            m_new = jnp.maximum(maxes[h], jnp.max(s, axis=-1, keepdims=True))
            p = jnp.exp2(s - m_new).astype(_BF16)
            new_accs.append(accs[h] * jnp.exp2(maxes[h] - m_new) + _dot(p, v_ref[h, key_rows(j), :]))
            new_maxes.append(m_new)
        return tuple(new_maxes), tuple(new_accs)

    _, accs = lax.fori_loop(0, i, body, (tuple(maxes), tuple(accs)))

    o_ref[...] = _rms(_merge_heads(accs), g_om_ref[...]).astype(o_ref.dtype)


def _prompt_attn(q, k, v, g_om, batch, seq):
    tile = min(ATTN_TILE, seq)
    nq = seq // tile
    width = g_om.shape[1]
    return pl.pallas_call(
        _prompt_attn_kernel,
        grid=(batch, nq),
        in_specs=[pl.BlockSpec((MLA_HEADS, tile, LANES), lambda b, i: (0, b * nq + i, 0)),
                  pl.BlockSpec((MLA_HEADS, seq, LANES), lambda b, i: (0, b, 0)),
                  pl.BlockSpec((MLA_HEADS, seq, LANES), lambda b, i: (0, b, 0)),
                  _whole(g_om.shape)],
        out_specs=pl.BlockSpec((tile, width), lambda b, i: (b * nq + i, 0)),
        out_shape=jax.ShapeDtypeStruct((batch * seq, width), _BF16),
        compiler_params=pltpu.CompilerParams(dimension_semantics=("arbitrary", "arbitrary"),
                                             vmem_limit_bytes=VMEM_LIMIT_BYTES),
        name="prompt_attn",
    )(q, k, v, g_om)


def _sample_attn_kernel(q_ref, ckv_new_ref, kpe_new_ref, ckv_past_ref, kpe_past_ref, w_ukt_ref, w_uvh_ref,
                        g_om_ref, o_ref):
    seq = q_ref.shape[1]
    q_all = jnp.concatenate([q_ref[h] for h in range(MLA_HEADS)], axis=0)
    q_lat = jnp.concatenate([_dot(q_ref[h], w_ukt_ref[h]) for h in range(MLA_HEADS)], axis=0).astype(_BF16)
    q_pe = q_all[:, ROPE_LO:ROPE_HI]
    c_past = ckv_past_ref[0].astype(_BF16)
    c_new = ckv_new_ref[...].astype(_BF16)
    s_past = _dot_nt(q_lat, c_past) + _dot_nt(q_pe, kpe_past_ref[0].astype(_BF16))
    s_new = _dot_nt(q_lat, c_new) + _dot_nt(q_all, kpe_new_ref[...].astype(_BF16))
    m = jnp.maximum(jnp.max(s_past, axis=-1, keepdims=True), jnp.max(s_new, axis=-1, keepdims=True))
    p_past = jnp.exp2(s_past - m)
    p_new = jnp.exp2(s_new - m)
    denom = jnp.sum(p_past, axis=-1, keepdims=True) + jnp.sum(p_new, axis=-1, keepdims=True)
    o_lat = ((_dot(p_past.astype(_BF16), c_past) + _dot(p_new.astype(_BF16), c_new)) / denom).astype(_BF16)
    o = sum(_dot(o_lat[h * seq:(h + 1) * seq], w_uvh_ref[h]) for h in range(MLA_HEADS))
    o_ref[...] = _rms(o, g_om_ref[...]).astype(o_ref.dtype)


def _sample_attn(q, ckv_new, kpe_new, ckv_past, kpe_past, w_ukt, w_uvh, g_om, batch, seq):
    past, kv_lora = ckv_past.shape[1:]
    width = g_om.shape[1]
    return pl.pallas_call(
        _sample_attn_kernel,
        grid=(batch,),
        in_specs=[pl.BlockSpec((MLA_HEADS, seq, LANES), lambda b: (0, b, 0)),
                  pl.BlockSpec((seq, kv_lora), lambda b: (b, 0)),
                  pl.BlockSpec((seq, LANES), lambda b: (b, 0)),
                  pl.BlockSpec((1, past, kv_lora), lambda b: (b, 0, 0)),
                  pl.BlockSpec((1, past, ROPE_DIM), lambda b: (b, 0, 0)),
                  _whole(w_ukt.shape), _whole(w_uvh.shape), _whole(g_om.shape)],
        out_specs=pl.BlockSpec((seq, width), lambda b: (b, 0)),
        out_shape=jax.ShapeDtypeStruct((batch * seq, width), _BF16),
        compiler_params=pltpu.CompilerParams(dimension_semantics=("arbitrary",),
                                             vmem_limit_bytes=VMEM_LIMIT_BYTES),
        name="sample_attn",
    )(q, ckv_new, kpe_new, ckv_past, kpe_past, w_ukt, w_uvh, g_om)


def _output_kernel(x_ref, an_ref, cn_ref, w_out_a_ref, w_out_c_ref, ln_ffn_ref, w_gate_ref, w_up_ref,
                   w_down_ref, g_final_ref, y_ref):
    h = x_ref[...] + _dot(an_ref[...], w_out_a_ref[...]) + _dot(cn_ref[...], w_out_c_ref[...])
    f = _rms(h, ln_ffn_ref[...]).astype(_BF16)
    gate = _dot(f, w_gate_ref[...])
    act = (gate * jax.nn.sigmoid(gate) * _dot(f, w_up_ref[...])).astype(_BF16)
    h = h + _dot(act, w_down_ref[...])
    y_ref[...] = _rms(h, g_final_ref[...])


def _output(x, an, cn, wts, name):
    n, d = x.shape
    tm = min(ROW_TILE, n)
    names = ("w_out_a", "w_out_c", "ln_ffn", "w_gate", "w_up", "w_down", "g_final")
    row_block = lambda w: pl.BlockSpec((tm, w), lambda i: (i, 0))
    resident = lambda shape: pl.BlockSpec(shape, lambda i: (0,) * len(shape), pipeline_mode=pl.Buffered(1))
    return pl.pallas_call(
        _output_kernel,
        grid=(n // tm,),
        in_specs=[row_block(d), row_block(an.shape[1]), row_block(cn.shape[1])]
                 + [resident(wts[k].shape) for k in names],
        out_specs=row_block(d),
        out_shape=jax.ShapeDtypeStruct((n, d), _F32),
        compiler_params=pltpu.CompilerParams(dimension_semantics=("arbitrary",),
                                             vmem_limit_bytes=VMEM_LIMIT_BYTES),
        name=name,
    )(x, an, cn, *[wts[k] for k in names])


def _pad_lanes(w, left):
    return jnp.pad(w, [(0, 0)] * (w.ndim - 1) + [(left, LANES - left - w.shape[-1])])


def _prepare_weights(ln_mix, w_in, g_q, w_uq, g_kv, w_uk, w_uv, w_dw, b_dw, g_cn, b_cn, g_om, g_oc, w_out,
                     ln_ffn, w_gate, w_up, w_down, g_final):
    q_lora, kv_lora = g_q.shape[0], g_kv.shape[0]
    d_model = w_in.shape[0]
    mla_width = g_om.shape[0]
    row = lambda v: v.reshape(1, -1)
    c0, c1, c2 = q_lora, q_lora + kv_lora, q_lora + kv_lora + ROPE_DIM
    w_in_p = jnp.concatenate([w_in[:, :c1], _pad_lanes(w_in[:, c1:c2], ROPE_LO), w_in[:, c2:]], axis=1)
    w_uq_p = _pad_lanes(w_uq.reshape(q_lora, MLA_HEADS, NOPE_DIM + ROPE_DIM), 0).reshape(q_lora, -1)
    w_uk_p = _pad_lanes(w_uk, 0).reshape(kv_lora, -1)
    w_uv_p = jnp.stack([_pad_lanes(w_uv[:, h], (h % 2) * V_DIM) for h in range(MLA_HEADS)], axis=1)
    v_one = jnp.stack([jnp.zeros((LANES,), _F32).at[V_DIM * (1 - h % 2)].set(1.0) for h in range(MLA_HEADS)])
    w_ukt = jnp.pad(jnp.transpose(w_uk, (1, 2, 0)), ((0, 0), (0, LANES - NOPE_DIM), (0, 0)))
    w_uvh = jnp.stack([jnp.pad(w_uv[:, h], ((0, 0), (h * V_DIM, (MLA_HEADS - 1 - h) * V_DIM)))
                       for h in range(MLA_HEADS)])
    return {
        "ln_mix": row(ln_mix), "w_in": w_in_p.astype(_BF16), "g_q": row(g_q), "w_uq": w_uq_p.astype(_BF16),
        "g_kv": row(g_kv), "w_uk": w_uk_p.astype(_BF16), "w_uv": w_uv_p.reshape(kv_lora, -1).astype(_BF16),
        "v_one": v_one.reshape(1, -1), "w_dw": w_dw, "b_dw": row(b_dw), "g_cn": row(g_cn), "b_cn": row(b_cn),
        "g_om": row(g_om), "g_oc": row(g_oc),
        "w_ukt": w_ukt.astype(_BF16), "w_uvh": w_uvh.astype(_BF16),
        "w_out_a": w_out[:mla_width].astype(_BF16), "w_out_c": w_out[mla_width:].astype(_BF16),
        "ln_ffn": row(ln_ffn), "w_gate": w_gate.astype(_BF16), "w_up": w_up.astype(_BF16),
        "w_down": w_down.astype(_BF16), "g_final": row(g_final),
    }


def _layer(x_prompt, x_sample, ckv_past, kpe_past, conv_past, wts):
    batch, seq, d = x_prompt.shape
    dec_batch, dec_seq, _ = x_sample.shape
    past = ckv_past.shape[1]

    xp = x_prompt.reshape(batch * seq, d)
    q, k, v, kv_p, kr_p, cn, cv_p = _prompt_proj(xp, _rope_tables(jnp.arange(seq)), wts, seq)
    an = _prompt_attn(q, k, v, wts["g_om"], batch, seq)
    y_p = _output(xp, an, cn, wts, "prompt_output")

    xs = x_sample.reshape(dec_batch * dec_seq, d)
    tabs = _rope_tables(jnp.tile(past + jnp.arange(dec_seq), dec_batch))
    q, kv_s, kr_s, cn, cv_s = _sample_proj(xs, tabs, conv_past, wts, dec_seq)
    an = _sample_attn(q, kv_s, kr_s, ckv_past, kpe_past, wts["w_ukt"], wts["w_uvh"], wts["g_om"],
                      dec_batch, dec_seq)
    y_s = _output(xs, an, cn, wts, "sample_output")

    rope = lambda t, b, s: t[:, ROPE_LO:ROPE_HI].reshape(b, s, ROPE_DIM)
    return (y_p.reshape(batch, seq, d), y_s.reshape(dec_batch, dec_seq, d),
            kv_p.reshape(batch, seq, -1), rope(kr_p, batch, seq), cv_p,
            kv_s.reshape(dec_batch, dec_seq, -1), rope(kr_s, dec_batch, dec_seq), cv_s)


def kernel(x_prompt, x_sample, cache_kv_latent, cache_k_rope, state_conv, ln_mix, w_in, g_q, w_uq, g_kv, w_uk, w_uv, w_dw, b_dw, g_cn, b_cn, g_om, g_oc, w_out, ln_ffn, w_gate, w_up, w_down, g_final):
    depth = w_in.shape[0]
    assert depth == 1, "the kernel implements the single-layer model of the problem"
    wts = _prepare_weights(ln_mix[0], w_in[0], g_q[0], w_uq[0], g_kv[0], w_uk[0], w_uv[0], w_dw[0], b_dw[0],
                           g_cn[0], b_cn[0], g_om[0], g_oc[0], w_out[0], ln_ffn[0], w_gate[0], w_up[0],
                           w_down[0], g_final)
    outs = _layer(x_prompt, x_sample, cache_kv_latent[0], cache_k_rope[0], state_conv[0], wts)
    y_p, y_s = outs[0], outs[1]
    return (y_p, y_s) + tuple(o[None] for o in outs[2:])
```

```python
import functools
import math

import jax
import jax.numpy as jnp
from jax import lax
from jax.experimental import pallas as pl
from jax.experimental.pallas import tpu as pltpu

CHUNK = 64
MLA_HEADS = 8
NOPE_DIM = 64
ROPE_DIM = 32
V_DIM = 64
ROPE_THETA = 10000.0
EPS = 1e-6
CONV_W = 31
CONV_STATE = CONV_W - 1
ATTN_SCALE = 1.0 / math.sqrt(NOPE_DIM + ROPE_DIM)
MASK_VALUE = -1e30

LANES = 128
SUBLANES = 8
HALF_ROPE = ROPE_DIM // 2
ROPE_LO = NOPE_DIM
ROPE_MID = ROPE_LO + HALF_ROPE
ROPE_HI = ROPE_LO + ROPE_DIM
VT_ROWS = 80
HALO_ROWS = 32
HALO_PAD = HALO_ROWS - CONV_STATE
Q_SCALE = ATTN_SCALE * math.log2(math.e)

ROW_TILE = 512
ATTN_TILE = 512
ATTN_DIAG_TILE = 256
VMEM_LIMIT_BYTES = 56 * 1024 * 1024

_BF16 = jnp.bfloat16
_F32 = jnp.float32


def _rms(x, g):
    return x * lax.rsqrt(jnp.mean(x * x, axis=-1, keepdims=True) + EPS) * g


def _dot(a, b):
    return jnp.dot(a, b, preferred_element_type=_F32)


def _dot_nt(a, b):
    return lax.dot_general(a, b, (((1,), (1,)), ((), ())), preferred_element_type=_F32)


def _rope(x, cosf, sin_lo, sin_hi):
    return (x * cosf + pltpu.roll(x, LANES - HALF_ROPE, 1) * sin_lo
            + pltpu.roll(x, HALF_ROPE, 1) * sin_hi)


def _rope_table_kernel(pos_ref, inv_ref, cos_ref, sin_lo_ref, sin_hi_ref):
    lane = lax.broadcasted_iota(jnp.int32, cos_ref.shape, 1)
    ang = pos_ref[...] * inv_ref[...]
    c = jnp.cos(ang)
    s = jnp.sin(ang)
    cos_ref[...] = jnp.where(lane < ROPE_LO, 1.0, jnp.where(lane < ROPE_HI, c, 0.0))
    sin_lo_ref[...] = jnp.where((lane >= ROPE_LO) & (lane < ROPE_MID), -s, 0.0)
    sin_hi_ref[...] = jnp.where((lane >= ROPE_MID) & (lane < ROPE_HI), s, 0.0)


def _rope_tables(pos):
    inv = 1.0 / (ROPE_THETA ** (jnp.arange(0, ROPE_DIM, 2, dtype=_F32) / ROPE_DIM))
    inv_lanes = jnp.zeros((1, LANES), _F32).at[0, ROPE_LO:ROPE_HI].set(jnp.tile(inv, 2))
    out = jax.ShapeDtypeStruct((pos.shape[0], LANES), _F32)
    return pl.pallas_call(
        _rope_table_kernel,
        out_shape=(out, out, out),
        name="rope_table",
    )(pos.astype(_F32).reshape(-1, 1), inv_lanes)


def _in_proj(x_ref, ln_mix_ref, w_in_ref, widths):
    hn = _rms(x_ref[...], ln_mix_ref[...]).astype(_BF16)
    outs, start = [], 0
    for w in widths:
        outs.append(_dot(hn, w_in_ref[:, start:start + w]))
        start += w
    return outs


def _query(cq, g_q_ref, w_uq_ref, rope_tabs, q_ref):
    q = _dot(_rms(cq, g_q_ref[...]).astype(_BF16), w_uq_ref[...])
    for h in range(MLA_HEADS):
        qh = _rope(q[:, h * LANES:(h + 1) * LANES], *rope_tabs)
        q_ref[h] = (qh * Q_SCALE).astype(q_ref.dtype)


def _depthwise_conv(ubuf, rows, w_dw_ref, b_dw_ref):
    out = None
    for phase in range(SUBLANES):
        ext = rows + (SUBLANES if phase else 0)
        partial = None
        for start in range(0, HALO_ROWS + 1, SUBLANES):
            k = start + phase - HALO_PAD
            if 0 <= k < CONV_W:
                term = ubuf[:, start:start + ext, :] * w_dw_ref[k:k + 1, :]
                partial = term if partial is None else partial + term
        shifted = partial[:, phase:phase + rows, :]
        out = shifted if out is None else out + shifted
    return out + b_dw_ref[...]


def _conv_branch(dw, g_cn_ref, b_cn_ref, g_oc_ref):
    mu = jnp.mean(dw, axis=-1, keepdims=True)
    xc = dw - mu
    y = xc * lax.rsqrt(jnp.mean(xc * xc, axis=-1, keepdims=True) + EPS) * g_cn_ref[...] + b_cn_ref[...]
    return _rms(y * jax.nn.sigmoid(y), g_oc_ref[...])


def _prompt_proj_kernel(x_ref, cos_ref, sin_lo_ref, sin_hi_ref, ln_mix_ref, w_in_ref, g_q_ref, w_uq_ref,
                        g_kv_ref, w_uk_ref, w_uvt_ref, v_one_ref, w_dw_ref, b_dw_ref, g_cn_ref, b_cn_ref,
                        g_oc_ref,
                        q_ref, k_ref, vt_ref, ckv_ref, kpe_ref, cn_ref, ncv_ref, ubuf, *, tiles_per_seq):
    rows = x_ref.shape[0]
    q_lora, kv_lora, conv_ch = g_q_ref.shape[1], g_kv_ref.shape[1], g_cn_ref.shape[1]
    cq, ckv, kpe, a, gate = _in_proj(x_ref, ln_mix_ref, w_in_ref, (q_lora, kv_lora, LANES, conv_ch, conv_ch))
    rope_tabs = (cos_ref[...], sin_lo_ref[...], sin_hi_ref[...])

    _query(cq, g_q_ref, w_uq_ref, rope_tabs, q_ref)

    ckv = _rms(ckv, g_kv_ref[...])
    ckv_ref[...] = ckv
    ckv_b = ckv.astype(_BF16)
    kpe = _rope(kpe, *rope_tabs)
    kpe_ref[...] = kpe
    k_nope = _dot(ckv_b, w_uk_ref[...])
    for h in range(MLA_HEADS):
        k_ref[h] = (k_nope[:, h * LANES:(h + 1) * LANES] + kpe).astype(k_ref.dtype)
    vt = _dot_nt(w_uvt_ref[...], ckv_b) + v_one_ref[...]
    key_tile = vt_ref.shape[3]
    for h in range(MLA_HEADS):
        for t in range(rows // key_tile):
            vt_ref[h, t] = vt[h * VT_ROWS:(h + 1) * VT_ROWS, t * key_tile:(t + 1) * key_tile].astype(vt_ref.dtype)

    @pl.when(lax.rem(pl.program_id(0), tiles_per_seq) == 0)
    def _():
        ubuf[:, 0:HALO_ROWS, :] = jnp.zeros((1, HALO_ROWS, conv_ch), _F32)

    ubuf[0, HALO_ROWS:HALO_ROWS + rows, :] = a * jax.nn.sigmoid(gate)
    dw = _depthwise_conv(ubuf, rows, w_dw_ref, b_dw_ref)[0]
    cn_ref[...] = _conv_branch(dw, g_cn_ref, b_cn_ref, g_oc_ref).astype(cn_ref.dtype)
    history = ubuf[0, HALO_PAD + rows:HALO_ROWS + rows, :]
    ncv_ref[0] = history
    ubuf[0, HALO_PAD:HALO_ROWS, :] = history


def _sample_proj_kernel(x_ref, cos_ref, sin_lo_ref, sin_hi_ref, state_ref, ln_mix_ref, w_in_ref, g_q_ref,
                        w_uq_ref, g_kv_ref, w_dw_ref, b_dw_ref, g_cn_ref, b_cn_ref, g_oc_ref,
                        q_ref, ckv_ref, kpe_ref, cn_ref, ncv_ref, ubuf):
    segs, seg_rows = ubuf.shape[0], ubuf.shape[1] - HALO_ROWS
    q_lora, kv_lora, conv_ch = g_q_ref.shape[1], g_kv_ref.shape[1], g_cn_ref.shape[1]
    cq, ckv, kpe, a, gate = _in_proj(x_ref, ln_mix_ref, w_in_ref, (q_lora, kv_lora, LANES, conv_ch, conv_ch))
    rope_tabs = (cos_ref[...], sin_lo_ref[...], sin_hi_ref[...])

    _query(cq, g_q_ref, w_uq_ref, rope_tabs, q_ref)
    ckv_ref[...] = _rms(ckv, g_kv_ref[...])
    kpe_ref[...] = _rope(kpe, *rope_tabs)

    ubuf[:, 0:HALO_PAD, :] = jnp.zeros((segs, HALO_PAD, conv_ch), _F32)
    ubuf[:, HALO_PAD:HALO_ROWS, :] = state_ref[...]
    ubuf[:, HALO_ROWS:, :] = (a * jax.nn.sigmoid(gate)).reshape(segs, seg_rows, conv_ch)
    dw = _depthwise_conv(ubuf, seg_rows, w_dw_ref, b_dw_ref).reshape(segs * seg_rows, conv_ch)
    cn_ref[...] = _conv_branch(dw, g_cn_ref, b_cn_ref, g_oc_ref).astype(cn_ref.dtype)
    ncv_ref[...] = ubuf[:, HALO_PAD + seg_rows:, :]


def _whole(shape):
    zeros = (0,) * len(shape)
    return pl.BlockSpec(shape, lambda *_: zeros)


def _prompt_proj(x, tabs, wts, seq):
    n, d = x.shape
    tm = min(ROW_TILE, seq)
    key_tile = min(ATTN_TILE, seq)
    assert tm % key_tile == 0 and seq % tm == 0
    tiles_per_seq = seq // tm
    conv_ch = wts["g_cn"].shape[1]
    kv_lora = wts["g_kv"].shape[1]
    row_block = lambda w: pl.BlockSpec((tm, w), lambda i: (i, 0))
    head_block = pl.BlockSpec((MLA_HEADS, tm, LANES), lambda i: (0, i, 0))
    vt_block = pl.BlockSpec((MLA_HEADS, tm // key_tile, VT_ROWS, key_tile), lambda i: (0, i, 0, 0))
    tab_block = pl.BlockSpec((tm, LANES), lambda i: (lax.rem(i, tiles_per_seq), 0))
    names = ("ln_mix", "w_in", "g_q", "w_uq", "g_kv", "w_uk", "w_uvt", "v_one", "w_dw", "b_dw", "g_cn",
             "b_cn", "g_oc")
    head_shape = jax.ShapeDtypeStruct((MLA_HEADS, n, LANES), _BF16)
    return pl.pallas_call(
        functools.partial(_prompt_proj_kernel, tiles_per_seq=tiles_per_seq),
        grid=(n // tm,),
        in_specs=[row_block(d), tab_block, tab_block, tab_block] + [_whole(wts[k].shape) for k in names],
        out_specs=(head_block, head_block, vt_block, row_block(kv_lora), row_block(LANES),
                   row_block(conv_ch), pl.BlockSpec((1, CONV_STATE, conv_ch), lambda i: (i // tiles_per_seq, 0, 0))),
        out_shape=(head_shape, head_shape,
                   jax.ShapeDtypeStruct((MLA_HEADS, n // key_tile, VT_ROWS, key_tile), _BF16),
                   jax.ShapeDtypeStruct((n, kv_lora), _F32), jax.ShapeDtypeStruct((n, LANES), _F32),
                   jax.ShapeDtypeStruct((n, conv_ch), _BF16),
                   jax.ShapeDtypeStruct((n // seq, CONV_STATE, conv_ch), _F32)),
        scratch_shapes=[pltpu.VMEM((1, HALO_ROWS + tm, conv_ch), _F32)],
        compiler_params=pltpu.CompilerParams(dimension_semantics=("arbitrary",),
                                             vmem_limit_bytes=VMEM_LIMIT_BYTES),
        name="prompt_proj",
    )(x, *tabs, *[wts[k] for k in names])


def _sample_proj(x, tabs, state, wts, seq):
    n, d = x.shape
    conv_ch = wts["g_cn"].shape[1]
    kv_lora = wts["g_kv"].shape[1]
    names = ("ln_mix", "w_in", "g_q", "w_uq", "g_kv", "w_dw", "b_dw", "g_cn", "b_cn", "g_oc")
    return pl.pallas_call(
        _sample_proj_kernel,
        out_shape=(jax.ShapeDtypeStruct((MLA_HEADS, n, LANES), _BF16),
                   jax.ShapeDtypeStruct((n, kv_lora), _F32), jax.ShapeDtypeStruct((n, LANES), _F32),
                   jax.ShapeDtypeStruct((n, conv_ch), _BF16),
                   jax.ShapeDtypeStruct((n // seq, CONV_STATE, conv_ch), _F32)),
        scratch_shapes=[pltpu.VMEM((n // seq, HALO_ROWS + seq, conv_ch), _F32)],
        compiler_params=pltpu.CompilerParams(vmem_limit_bytes=VMEM_LIMIT_BYTES),
        name="sample_proj",
    )(x, *tabs, state, *[wts[k] for k in names])


def _prompt_attn_kernel(q_ref, k_ref, vt_ref, g_om_ref, o_ref, *, sub):
    tile = q_ref.shape[1]
    i = pl.program_id(1)
    chunk_of = lambda t: lax.shift_right_logical(t, CHUNK.bit_length() - 1)

    def key_rows(j):
        return pl.ds(pl.multiple_of(j * tile, tile), tile)

    spans = []
    for r in range(tile // sub):
        n_keys = (r + 1) * sub
        key = lax.broadcasted_iota(jnp.int32, (n_keys, sub), 0)
        query = lax.broadcasted_iota(jnp.int32, (n_keys, sub), 1) + r * sub
        spans.append((slice(r * sub, (r + 1) * sub), n_keys, chunk_of(key) <= chunk_of(query)))
    maxes, accs = [], []
    for h in range(MLA_HEADS):
        sub_maxes, sub_accs = [], []
        for q_rows, n_keys, mask in spans:
            keys = pl.ds(pl.multiple_of(i * tile, tile), n_keys)
            s = jnp.where(mask, _dot_nt(k_ref[h, keys, :], q_ref[h, q_rows, :]), MASK_VALUE)
            m = jnp.max(s, axis=0, keepdims=True)
            sub_maxes.append(m)
            sub_accs.append(_dot(vt_ref[h, i, :, 0:n_keys], jnp.exp2(s - m).astype(_BF16)))
        maxes.append(jnp.concatenate(sub_maxes, axis=1))
        accs.append(jnp.concatenate(sub_accs, axis=1))

    def body(j, carry):
        maxes, accs = carry
        new_maxes, new_accs = [], []
        for h in range(MLA_HEADS):
            s = _dot_nt(k_ref[h, key_rows(j), :], q_ref[h])
            m_new = jnp.maximum(maxes[h], jnp.max(s, axis=0, keepdims=True))
            p = jnp.exp2(s - m_new).astype(_BF16)
            new_accs.append(accs[h] * jnp.exp2(maxes[h] - m_new) + _dot(vt_ref[h, j], p))
            new_maxes.append(m_new)
        return tuple(new_maxes), tuple(new_accs)

    _, accs = lax.fori_loop(0, i, body, (tuple(maxes), tuple(accs)))

    o_t = jnp.concatenate([acc[0:V_DIM] / acc[V_DIM:V_DIM + 1] for acc in accs], axis=0)
    o_ref[...] = _rms(o_t.T, g_om_ref[...]).astype(o_ref.dtype)


def _prompt_attn(q, k, vt, g_om, batch, seq):
    tile = vt.shape[3]
    nq = seq // tile
    width = g_om.shape[1]
    return pl.pallas_call(
        functools.partial(_prompt_attn_kernel, sub=min(ATTN_DIAG_TILE, tile)),
        grid=(batch, nq),
        in_specs=[pl.BlockSpec((MLA_HEADS, tile, LANES), lambda b, i: (0, b * nq + i, 0)),
                  pl.BlockSpec((MLA_HEADS, seq, LANES), lambda b, i: (0, b, 0)),
                  pl.BlockSpec((MLA_HEADS, nq, VT_ROWS, tile), lambda b, i: (0, b, 0, 0)),
                  _whole(g_om.shape)],
        out_specs=pl.BlockSpec((tile, width), lambda b, i: (b * nq + i, 0)),
        out_shape=jax.ShapeDtypeStruct((batch * seq, width), _BF16),
        compiler_params=pltpu.CompilerParams(dimension_semantics=("arbitrary", "arbitrary"),
                                             vmem_limit_bytes=VMEM_LIMIT_BYTES),
        name="prompt_attn",
    )(q, k, vt, g_om)


def _sample_attn_kernel(q_ref, ckv_new_ref, kpe_new_ref, ckv_past_ref, kpe_past_ref, w_ukt_ref, w_uvh_ref,
                        g_om_ref, o_ref):
    seq = q_ref.shape[1]
    q_all = jnp.concatenate([q_ref[h] for h in range(MLA_HEADS)], axis=0)
    q_lat = jnp.concatenate([_dot(q_ref[h], w_ukt_ref[h]) for h in range(MLA_HEADS)], axis=0).astype(_BF16)
    q_pe = q_all[:, ROPE_LO:ROPE_HI]
    c_past = ckv_past_ref[0].astype(_BF16)
    c_new = ckv_new_ref[...].astype(_BF16)
    s_past = _dot_nt(q_lat, c_past) + _dot_nt(q_pe, kpe_past_ref[0].astype(_BF16))
    s_new = _dot_nt(q_lat, c_new) + _dot_nt(q_all, kpe_new_ref[...].astype(_BF16))
    m = jnp.maximum(jnp.max(s_past, axis=-1, keepdims=True), jnp.max(s_new, axis=-1, keepdims=True))
    p_past = jnp.exp2(s_past - m)
    p_new = jnp.exp2(s_new - m)
    denom = jnp.sum(p_past, axis=-1, keepdims=True) + jnp.sum(p_new, axis=-1, keepdims=True)
    o_lat = ((_dot(p_past.astype(_BF16), c_past) + _dot(p_new.astype(_BF16), c_new)) / denom).astype(_BF16)
    o = sum(_dot(o_lat[h * seq:(h + 1) * seq], w_uvh_ref[h]) for h in range(MLA_HEADS))
    o_ref[...] = _rms(o, g_om_ref[...]).astype(o_ref.dtype)


def _sample_attn(q, ckv_new, kpe_new, ckv_past, kpe_past, w_ukt, w_uvh, g_om, batch, seq):
    past, kv_lora = ckv_past.shape[1:]
    width = g_om.shape[1]
    return pl.pallas_call(
        _sample_attn_kernel,
        grid=(batch,),
        in_specs=[pl.BlockSpec((MLA_HEADS, seq, LANES), lambda b: (0, b, 0)),
                  pl.BlockSpec((seq, kv_lora), lambda b: (b, 0)),
                  pl.BlockSpec((seq, LANES), lambda b: (b, 0)),
                  pl.BlockSpec((1, past, kv_lora), lambda b: (b, 0, 0)),
                  pl.BlockSpec((1, past, ROPE_DIM), lambda b: (b, 0, 0)),
                  _whole(w_ukt.shape), _whole(w_uvh.shape), _whole(g_om.shape)],
        out_specs=pl.BlockSpec((seq, width), lambda b: (b, 0)),
        out_shape=jax.ShapeDtypeStruct((batch * seq, width), _BF16),
        compiler_params=pltpu.CompilerParams(dimension_semantics=("arbitrary",),
                                             vmem_limit_bytes=VMEM_LIMIT_BYTES),
        name="sample_attn",
    )(q, ckv_new, kpe_new, ckv_past, kpe_past, w_ukt, w_uvh, g_om)


def _output_kernel(x_ref, an_ref, cn_ref, w_out_a_ref, w_out_c_ref, ln_ffn_ref, w_gate_ref, w_up_ref,
                   w_down_ref, g_final_ref, y_ref):
    h = x_ref[...] + _dot(an_ref[...], w_out_a_ref[...]) + _dot(cn_ref[...], w_out_c_ref[...])
    f = _rms(h, ln_ffn_ref[...]).astype(_BF16)
    gate = _dot(f, w_gate_ref[...])
    act = (gate * jax.nn.sigmoid(gate) * _dot(f, w_up_ref[...])).astype(_BF16)
    h = h + _dot(act, w_down_ref[...])
    y_ref[...] = _rms(h, g_final_ref[...])


def _output(x, an, cn, wts, name):
    n, d = x.shape
    tm = min(ROW_TILE, n)
    names = ("w_out_a", "w_out_c", "ln_ffn", "w_gate", "w_up", "w_down", "g_final")
    row_block = lambda w: pl.BlockSpec((tm, w), lambda i: (i, 0))
    resident = lambda shape: pl.BlockSpec(shape, lambda i: (0,) * len(shape), pipeline_mode=pl.Buffered(1))
    return pl.pallas_call(
        _output_kernel,
        grid=(n // tm,),
        in_specs=[row_block(d), row_block(an.shape[1]), row_block(cn.shape[1])]
                 + [resident(wts[k].shape) for k in names],
        out_specs=row_block(d),
        out_shape=jax.ShapeDtypeStruct((n, d), _F32),
        compiler_params=pltpu.CompilerParams(dimension_semantics=("arbitrary",),
                                             vmem_limit_bytes=VMEM_LIMIT_BYTES),
        name=name,
    )(x, an, cn, *[wts[k] for k in names])


def _pad_lanes(w, left):
    return jnp.pad(w, [(0, 0)] * (w.ndim - 1) + [(left, LANES - left - w.shape[-1])])


def _prepare_weights(ln_mix, w_in, g_q, w_uq, g_kv, w_uk, w_uv, w_dw, b_dw, g_cn, b_cn, g_om, g_oc, w_out,
                     ln_ffn, w_gate, w_up, w_down, g_final):
    q_lora, kv_lora = g_q.shape[0], g_kv.shape[0]
    d_model = w_in.shape[0]
    mla_width = g_om.shape[0]
    row = lambda v: v.reshape(1, -1)
    c0, c1, c2 = q_lora, q_lora + kv_lora, q_lora + kv_lora + ROPE_DIM
    w_in_p = jnp.concatenate([w_in[:, :c1], _pad_lanes(w_in[:, c1:c2], ROPE_LO), w_in[:, c2:]], axis=1)
    w_uq_p = _pad_lanes(w_uq.reshape(q_lora, MLA_HEADS, NOPE_DIM + ROPE_DIM), 0).reshape(q_lora, -1)
    w_uk_p = _pad_lanes(w_uk, 0).reshape(kv_lora, -1)
    w_uvt = jnp.pad(jnp.transpose(w_uv, (1, 2, 0)), ((0, 0), (0, VT_ROWS - V_DIM), (0, 0)))
    v_one = jnp.zeros((MLA_HEADS, VT_ROWS, 1), _F32).at[:, V_DIM].set(1.0)
    w_ukt = jnp.pad(jnp.transpose(w_uk, (1, 2, 0)), ((0, 0), (0, LANES - NOPE_DIM), (0, 0)))
    w_uvh = jnp.stack([jnp.pad(w_uv[:, h], ((0, 0), (h * V_DIM, (MLA_HEADS - 1 - h) * V_DIM)))
                       for h in range(MLA_HEADS)])
    return {
        "ln_mix": row(ln_mix), "w_in": w_in_p.astype(_BF16), "g_q": row(g_q), "w_uq": w_uq_p.astype(_BF16),
        "g_kv": row(g_kv), "w_uk": w_uk_p.astype(_BF16), "w_uvt": w_uvt.reshape(-1, kv_lora).astype(_BF16),
        "v_one": v_one.reshape(-1, 1), "w_dw": w_dw, "b_dw": row(b_dw), "g_cn": row(g_cn), "b_cn": row(b_cn),
        "g_om": row(g_om), "g_oc": row(g_oc),
        "w_ukt": w_ukt.astype(_BF16), "w_uvh": w_uvh.astype(_BF16),
        "w_out_a": w_out[:mla_width].astype(_BF16), "w_out_c": w_out[mla_width:].astype(_BF16),
        "ln_ffn": row(ln_ffn), "w_gate": w_gate.astype(_BF16), "w_up": w_up.astype(_BF16),
        "w_down": w_down.astype(_BF16), "g_final": row(g_final),
    }


def _layer(x_prompt, x_sample, ckv_past, kpe_past, conv_past, wts):
    batch, seq, d = x_prompt.shape
    dec_batch, dec_seq, _ = x_sample.shape
    past = ckv_past.shape[1]

    xp = x_prompt.reshape(batch * seq, d)
    q, k, v, kv_p, kr_p, cn, cv_p = _prompt_proj(xp, _rope_tables(jnp.arange(seq)), wts, seq)
    an = _prompt_attn(q, k, v, wts["g_om"], batch, seq)
    y_p = _output(xp, an, cn, wts, "prompt_output")

    xs = x_sample.reshape(dec_batch * dec_seq, d)
    tabs = _rope_tables(jnp.tile(past + jnp.arange(dec_seq), dec_batch))
    q, kv_s, kr_s, cn, cv_s = _sample_proj(xs, tabs, conv_past, wts, dec_seq)
    an = _sample_attn(q, kv_s, kr_s, ckv_past, kpe_past, wts["w_ukt"], wts["w_uvh"], wts["g_om"],
                      dec_batch, dec_seq)
    y_s = _output(xs, an, cn, wts, "sample_output")

    rope = lambda t, b, s: t[:, ROPE_LO:ROPE_HI].reshape(b, s, ROPE_DIM)
    return (y_p.reshape(batch, seq, d), y_s.reshape(dec_batch, dec_seq, d),
            kv_p.reshape(batch, seq, -1), rope(kr_p, batch, seq), cv_p,
            kv_s.reshape(dec_batch, dec_seq, -1), rope(kr_s, dec_batch, dec_seq), cv_s)


def kernel(x_prompt, x_sample, cache_kv_latent, cache_k_rope, state_conv, ln_mix, w_in, g_q, w_uq, g_kv, w_uk, w_uv, w_dw, b_dw, g_cn, b_cn, g_om, g_oc, w_out, ln_ffn, w_gate, w_up, w_down, g_final):
    depth = w_in.shape[0]
    assert depth == 1, "the kernel implements the single-layer model of the problem"
    wts = _prepare_weights(ln_mix[0], w_in[0], g_q[0], w_uq[0], g_kv[0], w_uk[0], w_uv[0], w_dw[0], b_dw[0],
                           g_cn[0], b_cn[0], g_om[0], g_oc[0], w_out[0], ln_ffn[0], w_gate[0], w_up[0],
                           w_down[0], g_final)
    outs = _layer(x_prompt, x_sample, cache_kv_latent[0], cache_k_rope[0], state_conv[0], wts)
    y_p, y_s = outs[0], outs[1]
    return (y_p, y_s) + tuple(o[None] for o in outs[2:])
```

```python
import functools
import math

import jax
import jax.numpy as jnp
from jax import lax
from jax.experimental import pallas as pl
from jax.experimental.pallas import tpu as pltpu

CHUNK = 64
MLA_HEADS = 8
NOPE_DIM = 64
ROPE_DIM = 32
V_DIM = 64
ROPE_THETA = 10000.0
EPS = 1e-6
CONV_W = 31
CONV_STATE = CONV_W - 1
ATTN_SCALE = 1.0 / math.sqrt(NOPE_DIM + ROPE_DIM)
MASK_VALUE = -1e30

LANES = 128
SUBLANES = 8
HALF_ROPE = ROPE_DIM // 2
ROPE_LO = NOPE_DIM
ROPE_HI = ROPE_LO + ROPE_DIM
assert ROPE_HI + ROPE_DIM == LANES
VT_ROWS = 80
HALO_ROWS = 32
HALO_PAD = HALO_ROWS - CONV_STATE
Q_SCALE = ATTN_SCALE * math.log2(math.e)

ROW_TILE = 512
ATTN_TILE = 512
KEY_PARTS = 2
SCORE_LOOKAHEAD = 4
ATTN_DIAG_TILE = 256
VMEM_LIMIT_BYTES = 56 * 1024 * 1024

_BF16 = jnp.bfloat16
_F32 = jnp.float32


def _rms(x, g):
    return x * lax.rsqrt(jnp.mean(x * x, axis=-1, keepdims=True) + EPS) * g


def _dot(a, b):
    return jnp.dot(a, b, preferred_element_type=_F32)


def _dot_nt(a, b):
    return lax.dot_general(a, b, (((1,), (1,)), ((), ())), preferred_element_type=_F32)


def _rope(x, table, rest):
    y = x * table
    lane = lax.broadcasted_iota(jnp.int32, x.shape, 1)
    in_rope = (lane >= ROPE_LO) & (lane < ROPE_HI)
    return jnp.where(in_rope, y + pltpu.roll(y, LANES - ROPE_DIM, 1), y if rest is None else rest)


def _rope_table_kernel(pos_ref, inv_ref, q_tab_ref, k_tab_ref):
    lane = lax.broadcasted_iota(jnp.int32, k_tab_ref.shape, 1)
    ang = pos_ref[...] * inv_ref[...]
    c = jnp.cos(ang)
    s = jnp.sin(ang)
    rot = jnp.where(lane < ROPE_HI, c, jnp.where(lane < ROPE_HI + HALF_ROPE, -s, s))
    k_tab_ref[...] = rot
    q_tab_ref[...] = jnp.where(lane < ROPE_LO, Q_SCALE, rot * Q_SCALE)


def _rope_tables(pos):
    inv = 1.0 / (ROPE_THETA ** (jnp.arange(0, ROPE_DIM, 2, dtype=_F32) / ROPE_DIM))
    inv_lanes = jnp.zeros((1, LANES), _F32).at[0, ROPE_LO:].set(jnp.tile(inv, 4))
    out = jax.ShapeDtypeStruct((pos.shape[0], LANES), _F32)
    return pl.pallas_call(
        _rope_table_kernel,
        out_shape=(out, out),
        name="rope_table",
    )(pos.astype(_F32).reshape(-1, 1), inv_lanes)


def _in_proj(x_ref, ln_mix_ref, w_in_ref, widths):
    hn = _rms(x_ref[...], ln_mix_ref[...]).astype(_BF16)
    outs, start = [], 0
    for w in widths:
        outs.append(_dot(hn, w_in_ref[:, start:start + w]))
        start += w
    return outs


def _query(cq, g_q_ref, w_uq_ref, q_tab, q_ref):
    q = _dot(_rms(cq, g_q_ref[...]).astype(_BF16), w_uq_ref[...])
    for h in range(MLA_HEADS):
        q_ref[h] = _rope(q[:, h * LANES:(h + 1) * LANES], q_tab, None).astype(q_ref.dtype)


def _depthwise_conv(ubuf, rows, w_dw_ref, b_dw_ref):
    out = None
    for phase in range(SUBLANES):
        ext = rows + (SUBLANES if phase else 0)
        partial = None
        for start in range(0, HALO_ROWS + 1, SUBLANES):
            k = start + phase - HALO_PAD
            if 0 <= k < CONV_W:
                term = ubuf[:, start:start + ext, :] * w_dw_ref[k:k + 1, :]
                partial = term if partial is None else partial + term
        shifted = partial[:, phase:phase + rows, :]
        out = shifted if out is None else out + shifted
    return out + b_dw_ref[...]


def _conv_branch(dw, g_cn_ref, b_cn_ref, g_oc_ref):
    mu = jnp.mean(dw, axis=-1, keepdims=True)
    xc = dw - mu
    y = xc * lax.rsqrt(jnp.mean(xc * xc, axis=-1, keepdims=True) + EPS) * g_cn_ref[...] + b_cn_ref[...]
    return _rms(y * jax.nn.sigmoid(y), g_oc_ref[...])


def _prompt_proj_kernel(x_ref, q_tab_ref, k_tab_ref, ln_mix_ref, w_in_ref, g_q_ref, w_uq_ref,
                        g_kv_ref, w_uk_ref, w_uvt_ref, v_one_ref, w_dw_ref, b_dw_ref, g_cn_ref, b_cn_ref,
                        g_oc_ref,
                        q_ref, k_ref, vt_ref, ckv_ref, kpe_ref, cn_ref, ncv_ref, ubuf, *, tiles_per_seq):
    rows = x_ref.shape[0]
    q_lora, kv_lora, conv_ch = g_q_ref.shape[1], g_kv_ref.shape[1], g_cn_ref.shape[1]
    cq, ckv, kpe, a, gate = _in_proj(x_ref, ln_mix_ref, w_in_ref, (q_lora, kv_lora, LANES, conv_ch, conv_ch))
    _query(cq, g_q_ref, w_uq_ref, q_tab_ref[...], q_ref)

    ckv = _rms(ckv, g_kv_ref[...])
    ckv_ref[...] = ckv
    ckv_b = ckv.astype(_BF16)
    kpe = _rope(kpe, k_tab_ref[...], 0.0)
    kpe_ref[...] = kpe
    k_nope = _dot(ckv_b, w_uk_ref[...])
    for h in range(MLA_HEADS):
        k_ref[h] = (k_nope[:, h * LANES:(h + 1) * LANES] + kpe).astype(k_ref.dtype)
    vt = _dot_nt(w_uvt_ref[...], ckv_b) + v_one_ref[...]
    key_tile = vt_ref.shape[3]
    for h in range(MLA_HEADS):
        for t in range(rows // key_tile):
            vt_ref[h, t] = vt[h * VT_ROWS:(h + 1) * VT_ROWS, t * key_tile:(t + 1) * key_tile].astype(vt_ref.dtype)

    @pl.when(lax.rem(pl.program_id(0), tiles_per_seq) == 0)
    def _():
        ubuf[:, 0:HALO_ROWS, :] = jnp.zeros((1, HALO_ROWS, conv_ch), _F32)

    ubuf[0, HALO_ROWS:HALO_ROWS + rows, :] = a * jax.nn.sigmoid(gate)
    dw = _depthwise_conv(ubuf, rows, w_dw_ref, b_dw_ref)[0]
    cn_ref[...] = _conv_branch(dw, g_cn_ref, b_cn_ref, g_oc_ref).astype(cn_ref.dtype)
    history = ubuf[0, HALO_PAD + rows:HALO_ROWS + rows, :]
    ncv_ref[0] = history
    ubuf[0, HALO_PAD:HALO_ROWS, :] = history


def _sample_proj_kernel(x_ref, q_tab_ref, k_tab_ref, state_ref, ln_mix_ref, w_in_ref, g_q_ref,
                        w_uq_ref, g_kv_ref, w_dw_ref, b_dw_ref, g_cn_ref, b_cn_ref, g_oc_ref,
                        q_ref, ckv_ref, kpe_ref, cn_ref, ncv_ref, ubuf):
    segs, seg_rows = ubuf.shape[0], ubuf.shape[1] - HALO_ROWS
    q_lora, kv_lora, conv_ch = g_q_ref.shape[1], g_kv_ref.shape[1], g_cn_ref.shape[1]
    cq, ckv, kpe, a, gate = _in_proj(x_ref, ln_mix_ref, w_in_ref, (q_lora, kv_lora, LANES, conv_ch, conv_ch))
    _query(cq, g_q_ref, w_uq_ref, q_tab_ref[...], q_ref)
    ckv_ref[...] = _rms(ckv, g_kv_ref[...])
    kpe_ref[...] = _rope(kpe, k_tab_ref[...], 0.0)

    ubuf[:, 0:HALO_PAD, :] = jnp.zeros((segs, HALO_PAD, conv_ch), _F32)
    ubuf[:, HALO_PAD:HALO_ROWS, :] = state_ref[...]
    ubuf[:, HALO_ROWS:, :] = (a * jax.nn.sigmoid(gate)).reshape(segs, seg_rows, conv_ch)
    dw = _depthwise_conv(ubuf, seg_rows, w_dw_ref, b_dw_ref).reshape(segs * seg_rows, conv_ch)
    cn_ref[...] = _conv_branch(dw, g_cn_ref, b_cn_ref, g_oc_ref).astype(cn_ref.dtype)
    ncv_ref[...] = ubuf[:, HALO_PAD + seg_rows:, :]


def _whole(shape):
    zeros = (0,) * len(shape)
    return pl.BlockSpec(shape, lambda *_: zeros)


def _prompt_proj(x, tabs, wts, seq):
    n, d = x.shape
    tm = min(ROW_TILE, seq)
    key_tile = min(ATTN_TILE, seq)
    assert tm % key_tile == 0 and seq % tm == 0
    tiles_per_seq = seq // tm
    conv_ch = wts["g_cn"].shape[1]
    kv_lora = wts["g_kv"].shape[1]
    row_block = lambda w: pl.BlockSpec((tm, w), lambda i: (i, 0))
    head_block = pl.BlockSpec((MLA_HEADS, tm, LANES), lambda i: (0, i, 0))
    vt_block = pl.BlockSpec((MLA_HEADS, tm // key_tile, VT_ROWS, key_tile), lambda i: (0, i, 0, 0))
    tab_block = pl.BlockSpec((tm, LANES), lambda i: (lax.rem(i, tiles_per_seq), 0))
    names = ("ln_mix", "w_in", "g_q", "w_uq", "g_kv", "w_uk", "w_uvt", "v_one", "w_dw", "b_dw", "g_cn",
             "b_cn", "g_oc")
    head_shape = jax.ShapeDtypeStruct((MLA_HEADS, n, LANES), _BF16)
    return pl.pallas_call(
        functools.partial(_prompt_proj_kernel, tiles_per_seq=tiles_per_seq),
        grid=(n // tm,),
        in_specs=[row_block(d), tab_block, tab_block] + [_whole(wts[k].shape) for k in names],
        out_specs=(head_block, head_block, vt_block, row_block(kv_lora), row_block(LANES),
                   row_block(conv_ch), pl.BlockSpec((1, CONV_STATE, conv_ch), lambda i: (i // tiles_per_seq, 0, 0))),
        out_shape=(head_shape, head_shape,
                   jax.ShapeDtypeStruct((MLA_HEADS, n // key_tile, VT_ROWS, key_tile), _BF16),
                   jax.ShapeDtypeStruct((n, kv_lora), _F32), jax.ShapeDtypeStruct((n, LANES), _F32),
                   jax.ShapeDtypeStruct((n, conv_ch), _BF16),
                   jax.ShapeDtypeStruct((n // seq, CONV_STATE, conv_ch), _F32)),
        scratch_shapes=[pltpu.VMEM((1, HALO_ROWS + tm, conv_ch), _F32)],
        compiler_params=pltpu.CompilerParams(dimension_semantics=("arbitrary",),
                                             vmem_limit_bytes=VMEM_LIMIT_BYTES),
        name="prompt_proj",
    )(x, *tabs, *[wts[k] for k in names])


def _sample_proj(x, tabs, state, wts, seq):
    n, d = x.shape
    conv_ch = wts["g_cn"].shape[1]
    kv_lora = wts["g_kv"].shape[1]
    names = ("ln_mix", "w_in", "g_q", "w_uq", "g_kv", "w_dw", "b_dw", "g_cn", "b_cn", "g_oc")
    return pl.pallas_call(
        _sample_proj_kernel,
        out_shape=(jax.ShapeDtypeStruct((MLA_HEADS, n, LANES), _BF16),
                   jax.ShapeDtypeStruct((n, kv_lora), _F32), jax.ShapeDtypeStruct((n, LANES), _F32),
                   jax.ShapeDtypeStruct((n, conv_ch), _BF16),
                   jax.ShapeDtypeStruct((n // seq, CONV_STATE, conv_ch), _F32)),
        scratch_shapes=[pltpu.VMEM((n // seq, HALO_ROWS + seq, conv_ch), _F32)],
        compiler_params=pltpu.CompilerParams(vmem_limit_bytes=VMEM_LIMIT_BYTES),
        name="sample_proj",
    )(x, *tabs, state, *[wts[k] for k in names])


def _software_pipeline(items, issue, finish):
    items = list(items)
    issued, done = [], []
    for t in range(len(items) + SCORE_LOOKAHEAD):
        if t < len(items):
            issued.append(issue(items[t]))
        if t >= SCORE_LOOKAHEAD:
            done.append(finish(items[t - SCORE_LOOKAHEAD], issued[t - SCORE_LOOKAHEAD]))
    return done


def _prompt_attn_kernel(q_ref, k_ref, vt_ref, g_om_ref, o_ref, *, sub):
    tile = q_ref.shape[1]
    i = pl.program_id(1)
    chunk_of = lambda t: lax.shift_right_logical(t, CHUNK.bit_length() - 1)

    def key_rows(j):
        return pl.ds(pl.multiple_of(j * tile, tile), tile)

    spans = []
    for r in range(tile // sub):
        n_keys = (r + 1) * sub
        key = lax.broadcasted_iota(jnp.int32, (n_keys, sub), 0)
        query = lax.broadcasted_iota(jnp.int32, (n_keys, sub), 1) + r * sub
        spans.append((slice(r * sub, (r + 1) * sub), n_keys, chunk_of(key) <= chunk_of(query)))

    def diag_scores(h, span):
        q_rows, n_keys, mask = span
        keys = pl.ds(pl.multiple_of(i * tile, tile), n_keys)
        return jnp.where(mask, _dot_nt(k_ref[h, keys, :], q_ref[h, q_rows, :]), MASK_VALUE)

    def diag_finish(h, span, s):
        m = jnp.max(s, axis=0, keepdims=True)
        return m, _dot(vt_ref[h, i, :, 0:span[1]], jnp.exp2(s - m).astype(_BF16))

    items = [(h, span) for h in range(MLA_HEADS) for span in spans]
    done = _software_pipeline(items, lambda it: diag_scores(*it), lambda it, s: diag_finish(*it, s))
    per_head = len(spans)
    maxes = [jnp.concatenate([m for m, _ in done[h * per_head:(h + 1) * per_head]], axis=1)
             for h in range(MLA_HEADS)]
    accs = [jnp.concatenate([a for _, a in done[h * per_head:(h + 1) * per_head]], axis=1)
            for h in range(MLA_HEADS)]

    def body(j, carry):
        state = [list(carry[0]), list(carry[1])]
        part = tile // KEY_PARTS

        def issue(item):
            h, t = item
            rows = pl.ds(pl.multiple_of(j * tile + t * part, part), part)
            return _dot_nt(k_ref[h, rows, :], q_ref[h])

        def finish(item, s):
            h, t = item
            m_old, acc = state[0][h], state[1][h]
            m_new = jnp.maximum(m_old, jnp.max(s, axis=0, keepdims=True))
            p = jnp.exp2(s - m_new).astype(_BF16)
            state[0][h] = m_new
            state[1][h] = acc * jnp.exp2(m_old - m_new) + _dot(vt_ref[h, j, :, t * part:(t + 1) * part], p)

        _software_pipeline([(h, t) for h in range(MLA_HEADS) for t in range(KEY_PARTS)], issue, finish)
        return tuple(state[0]), tuple(state[1])

    _, accs = lax.fori_loop(0, i, body, (tuple(maxes), tuple(accs)))

    o_t = jnp.concatenate([acc[0:V_DIM] / acc[V_DIM:V_DIM + 1] for acc in accs], axis=0)
    o_ref[...] = _rms(o_t.T, g_om_ref[...]).astype(o_ref.dtype)


def _prompt_attn(q, k, vt, g_om, batch, seq):
    tile = vt.shape[3]
    nq = seq // tile
    width = g_om.shape[1]
    return pl.pallas_call(
        functools.partial(_prompt_attn_kernel, sub=min(ATTN_DIAG_TILE, tile)),
        grid=(batch, nq),
        in_specs=[pl.BlockSpec((MLA_HEADS, tile, LANES), lambda b, i: (0, b * nq + i, 0)),
                  pl.BlockSpec((MLA_HEADS, seq, LANES), lambda b, i: (0, b, 0)),
                  pl.BlockSpec((MLA_HEADS, nq, VT_ROWS, tile), lambda b, i: (0, b, 0, 0)),
                  _whole(g_om.shape)],
        out_specs=pl.BlockSpec((tile, width), lambda b, i: (b * nq + i, 0)),
        out_shape=jax.ShapeDtypeStruct((batch * seq, width), _BF16),
        compiler_params=pltpu.CompilerParams(dimension_semantics=("arbitrary", "arbitrary"),
                                             vmem_limit_bytes=VMEM_LIMIT_BYTES),
        name="prompt_attn",
    )(q, k, vt, g_om)


def _sample_attn_kernel(q_ref, ckv_new_ref, kpe_new_ref, ckv_past_ref, kpe_past_ref, w_ukt_ref, w_uvh_ref,
                        g_om_ref, o_ref):
    seq = q_ref.shape[1]
    q_all = jnp.concatenate([q_ref[h] for h in range(MLA_HEADS)], axis=0)
    q_lat = jnp.concatenate([_dot(q_ref[h], w_ukt_ref[h]) for h in range(MLA_HEADS)], axis=0).astype(_BF16)
    q_pe = q_all[:, ROPE_LO:ROPE_HI]
    c_past = ckv_past_ref[0].astype(_BF16)
    c_new = ckv_new_ref[...].astype(_BF16)
    s_past = _dot_nt(q_lat, c_past) + _dot_nt(q_pe, kpe_past_ref[0].astype(_BF16))
    s_new = _dot_nt(q_lat, c_new) + _dot_nt(q_all, kpe_new_ref[...].astype(_BF16))
    m = jnp.maximum(jnp.max(s_past, axis=-1, keepdims=True), jnp.max(s_new, axis=-1, keepdims=True))
    p_past = jnp.exp2(s_past - m)
    p_new = jnp.exp2(s_new - m)
    denom = jnp.sum(p_past, axis=-1, keepdims=True) + jnp.sum(p_new, axis=-1, keepdims=True)
    o_lat = ((_dot(p_past.astype(_BF16), c_past) + _dot(p_new.astype(_BF16), c_new)) / denom).astype(_BF16)
    o = sum(_dot(o_lat[h * seq:(h + 1) * seq], w_uvh_ref[h]) for h in range(MLA_HEADS))
    o_ref[...] = _rms(o, g_om_ref[...]).astype(o_ref.dtype)


def _sample_attn(q, ckv_new, kpe_new, ckv_past, kpe_past, w_ukt, w_uvh, g_om, batch, seq):
    past, kv_lora = ckv_past.shape[1:]
    width = g_om.shape[1]
    return pl.pallas_call(
        _sample_attn_kernel,
        grid=(batch,),
        in_specs=[pl.BlockSpec((MLA_HEADS, seq, LANES), lambda b: (0, b, 0)),
                  pl.BlockSpec((seq, kv_lora), lambda b: (b, 0)),
                  pl.BlockSpec((seq, LANES), lambda b: (b, 0)),
                  pl.BlockSpec((1, past, kv_lora), lambda b: (b, 0, 0)),
                  pl.BlockSpec((1, past, ROPE_DIM), lambda b: (b, 0, 0)),
                  _whole(w_ukt.shape), _whole(w_uvh.shape), _whole(g_om.shape)],
        out_specs=pl.BlockSpec((seq, width), lambda b: (b, 0)),
        out_shape=jax.ShapeDtypeStruct((batch * seq, width), _BF16),
        compiler_params=pltpu.CompilerParams(dimension_semantics=("arbitrary",),
                                             vmem_limit_bytes=VMEM_LIMIT_BYTES),
        name="sample_attn",
    )(q, ckv_new, kpe_new, ckv_past, kpe_past, w_ukt, w_uvh, g_om)


def _output_kernel(x_ref, an_ref, cn_ref, w_out_a_ref, w_out_c_ref, ln_ffn_ref, w_gate_ref, w_up_ref,
                   w_down_ref, g_final_ref, y_ref):
    h = x_ref[...] + _dot(an_ref[...], w_out_a_ref[...]) + _dot(cn_ref[...], w_out_c_ref[...])
    f = _rms(h, ln_ffn_ref[...]).astype(_BF16)
    gate = _dot(f, w_gate_ref[...])
    act = (gate * jax.nn.sigmoid(gate) * _dot(f, w_up_ref[...])).astype(_BF16)
    h = h + _dot(act, w_down_ref[...])
    y_ref[...] = _rms(h, g_final_ref[...])


def _output(x, an, cn, wts, name):
    n, d = x.shape
    tm = min(ROW_TILE, n)
    names = ("w_out_a", "w_out_c", "ln_ffn", "w_gate", "w_up", "w_down", "g_final")
    row_block = lambda w: pl.BlockSpec((tm, w), lambda i: (i, 0))
    resident = lambda shape: pl.BlockSpec(shape, lambda i: (0,) * len(shape), pipeline_mode=pl.Buffered(1))
    return pl.pallas_call(
        _output_kernel,
        grid=(n // tm,),
        in_specs=[row_block(d), row_block(an.shape[1]), row_block(cn.shape[1])]
                 + [resident(wts[k].shape) for k in names],
        out_specs=row_block(d),
        out_shape=jax.ShapeDtypeStruct((n, d), _F32),
        compiler_params=pltpu.CompilerParams(dimension_semantics=("arbitrary",),
                                             vmem_limit_bytes=VMEM_LIMIT_BYTES),
        name=name,
    )(x, an, cn, *[wts[k] for k in names])


def _pad_lanes(w, left):
    return jnp.pad(w, [(0, 0)] * (w.ndim - 1) + [(left, LANES - left - w.shape[-1])])


def _prepare_weights(ln_mix, w_in, g_q, w_uq, g_kv, w_uk, w_uv, w_dw, b_dw, g_cn, b_cn, g_om, g_oc, w_out,
                     ln_ffn, w_gate, w_up, w_down, g_final):
    q_lora, kv_lora = g_q.shape[0], g_kv.shape[0]
    d_model = w_in.shape[0]
    mla_width = g_om.shape[0]
    row = lambda v: v.reshape(1, -1)
    c0, c1, c2 = q_lora, q_lora + kv_lora, q_lora + kv_lora + ROPE_DIM
    with_swap = lambda w: jnp.concatenate([w, w[..., HALF_ROPE:], w[..., :HALF_ROPE]], axis=-1)
    w_in_p = jnp.concatenate([w_in[:, :c1], _pad_lanes(with_swap(w_in[:, c1:c2]), ROPE_LO), w_in[:, c2:]], axis=1)
    w_uq_h = w_uq.reshape(q_lora, MLA_HEADS, NOPE_DIM + ROPE_DIM)
    w_uq_p = jnp.concatenate([w_uq_h[..., :NOPE_DIM], with_swap(w_uq_h[..., NOPE_DIM:])], axis=-1)
    w_uq_p = w_uq_p.reshape(q_lora, -1)
    w_uk_p = _pad_lanes(w_uk, 0).reshape(kv_lora, -1)
    w_uvt = jnp.pad(jnp.transpose(w_uv, (1, 2, 0)), ((0, 0), (0, VT_ROWS - V_DIM), (0, 0)))
    v_one = jnp.zeros((MLA_HEADS, VT_ROWS, 1), _F32).at[:, V_DIM].set(1.0)
    w_ukt = jnp.pad(jnp.transpose(w_uk, (1, 2, 0)), ((0, 0), (0, LANES - NOPE_DIM), (0, 0)))
    w_uvh = jnp.stack([jnp.pad(w_uv[:, h], ((0, 0), (h * V_DIM, (MLA_HEADS - 1 - h) * V_DIM)))
                       for h in range(MLA_HEADS)])
    return {
        "ln_mix": row(ln_mix), "w_in": w_in_p.astype(_BF16), "g_q": row(g_q), "w_uq": w_uq_p.astype(_BF16),
        "g_kv": row(g_kv), "w_uk": w_uk_p.astype(_BF16), "w_uvt": w_uvt.reshape(-1, kv_lora).astype(_BF16),
        "v_one": v_one.reshape(-1, 1), "w_dw": w_dw, "b_dw": row(b_dw), "g_cn": row(g_cn), "b_cn": row(b_cn),
        "g_om": row(g_om), "g_oc": row(g_oc),
        "w_ukt": w_ukt.astype(_BF16), "w_uvh": w_uvh.astype(_BF16),
        "w_out_a": w_out[:mla_width].astype(_BF16), "w_out_c": w_out[mla_width:].astype(_BF16),
        "ln_ffn": row(ln_ffn), "w_gate": w_gate.astype(_BF16), "w_up": w_up.astype(_BF16),
        "w_down": w_down.astype(_BF16), "g_final": row(g_final),
    }


def _layer(x_prompt, x_sample, ckv_past, kpe_past, conv_past, wts):
    batch, seq, d = x_prompt.shape
    dec_batch, dec_seq, _ = x_sample.shape
    past = ckv_past.shape[1]

    xp = x_prompt.reshape(batch * seq, d)
    q, k, v, kv_p, kr_p, cn, cv_p = _prompt_proj(xp, _rope_tables(jnp.arange(seq)), wts, seq)
    an = _prompt_attn(q, k, v, wts["g_om"], batch, seq)
    y_p = _output(xp, an, cn, wts, "prompt_output")

    xs = x_sample.reshape(dec_batch * dec_seq, d)
    tabs = _rope_tables(jnp.tile(past + jnp.arange(dec_seq), dec_batch))
    q, kv_s, kr_s, cn, cv_s = _sample_proj(xs, tabs, conv_past, wts, dec_seq)
    an = _sample_attn(q, kv_s, kr_s, ckv_past, kpe_past, wts["w_ukt"], wts["w_uvh"], wts["g_om"],
                      dec_batch, dec_seq)
    y_s = _output(xs, an, cn, wts, "sample_output")

    rope = lambda t, b, s: t[:, ROPE_LO:ROPE_HI].reshape(b, s, ROPE_DIM)
    return (y_p.reshape(batch, seq, d), y_s.reshape(dec_batch, dec_seq, d),
            kv_p.reshape(batch, seq, -1), rope(kr_p, batch, seq), cv_p,
            kv_s.reshape(dec_batch, dec_seq, -1), rope(kr_s, dec_batch, dec_seq), cv_s)


def kernel(x_prompt, x_sample, cache_kv_latent, cache_k_rope, state_conv, ln_mix, w_in, g_q, w_uq, g_kv, w_uk, w_uv, w_dw, b_dw, g_cn, b_cn, g_om, g_oc, w_out, ln_ffn, w_gate, w_up, w_down, g_final):
    depth = w_in.shape[0]
    assert depth == 1, "the kernel implements the single-layer model of the problem"
    wts = _prepare_weights(ln_mix[0], w_in[0], g_q[0], w_uq[0], g_kv[0], w_uk[0], w_uv[0], w_dw[0], b_dw[0],
                           g_cn[0], b_cn[0], g_om[0], g_oc[0], w_out[0], ln_ffn[0], w_gate[0], w_up[0],
                           w_down[0], g_final)
    outs = _layer(x_prompt, x_sample, cache_kv_latent[0], cache_k_rope[0], state_conv[0], wts)
    y_p, y_s = outs[0], outs[1]
    return (y_p, y_s) + tuple(o[None] for o in outs[2:])
```

```python
import functools
import math

import jax
import jax.numpy as jnp
from jax import lax
from jax.experimental import pallas as pl
from jax.experimental.pallas import tpu as pltpu

CHUNK = 64
MLA_HEADS = 8
NOPE_DIM = 64
ROPE_DIM = 32
V_DIM = 64
ROPE_THETA = 10000.0
EPS = 1e-6
CONV_W = 31
CONV_STATE = CONV_W - 1
ATTN_SCALE = 1.0 / math.sqrt(NOPE_DIM + ROPE_DIM)
MASK_VALUE = -1e30

LANES = 128
SUBLANES = 8
HALF_ROPE = ROPE_DIM // 2
ROPE_LO = NOPE_DIM
ROPE_HI = ROPE_LO + ROPE_DIM
assert ROPE_HI + ROPE_DIM == LANES
VT_ROWS = 80
HALO_ROWS = 32
HALO_PAD = HALO_ROWS - CONV_STATE
Q_SCALE = ATTN_SCALE * math.log2(math.e)

ROW_TILE = 512
ATTN_TILE = 512
KEY_PARTS = 2
SCORE_LOOKAHEAD = 4
ATTN_DIAG_TILE = 256
VMEM_LIMIT_BYTES = 56 * 1024 * 1024

_BF16 = jnp.bfloat16
_F32 = jnp.float32


def _rms(x, g):
    return x * lax.rsqrt(jnp.mean(x * x, axis=-1, keepdims=True) + EPS) * g


def _dot(a, b):
    return jnp.dot(a, b, preferred_element_type=_F32)


def _dot_nt(a, b):
    return lax.dot_general(a, b, (((1,), (1,)), ((), ())), preferred_element_type=_F32)


def _rope(x, table, rest):
    y = x * table
    lane = lax.broadcasted_iota(jnp.int32, x.shape, 1)
    in_rope = (lane >= ROPE_LO) & (lane < ROPE_HI)
    return jnp.where(in_rope, y + pltpu.roll(y, LANES - ROPE_DIM, 1), y if rest is None else rest)


def _rope_table_kernel(pos_ref, inv_ref, q_tab_ref, k_tab_ref):
    lane = lax.broadcasted_iota(jnp.int32, k_tab_ref.shape, 1)
    ang = pos_ref[...] * inv_ref[...]
    c = jnp.cos(ang)
    s = jnp.sin(ang)
    rot = jnp.where(lane < ROPE_HI, c, jnp.where(lane < ROPE_HI + HALF_ROPE, -s, s))
    k_tab_ref[...] = rot
    q_tab_ref[...] = jnp.where(lane < ROPE_LO, Q_SCALE, rot * Q_SCALE)


def _rope_tables(pos):
    inv = 1.0 / (ROPE_THETA ** (jnp.arange(0, ROPE_DIM, 2, dtype=_F32) / ROPE_DIM))
    inv_lanes = jnp.zeros((1, LANES), _F32).at[0, ROPE_LO:].set(jnp.tile(inv, 4))
    out = jax.ShapeDtypeStruct((pos.shape[0], LANES), _F32)
    return pl.pallas_call(
        _rope_table_kernel,
        out_shape=(out, out),
        name="rope_table",
    )(pos.astype(_F32).reshape(-1, 1), inv_lanes)


def _in_proj(x_ref, ln_mix_ref, w_in_ref, widths):
    hn = _rms(x_ref[...], ln_mix_ref[...]).astype(_BF16)
    outs, start = [], 0
    for w in widths:
        outs.append(_dot(hn, w_in_ref[:, start:start + w]))
        start += w
    return outs


def _query(cq, g_q_ref, w_uq_ref, q_tab, q_ref):
    q = _dot(_rms(cq, g_q_ref[...]).astype(_BF16), w_uq_ref[...])
    for h in range(MLA_HEADS):
        q_ref[h] = _rope(q[:, h * LANES:(h + 1) * LANES], q_tab, None).astype(q_ref.dtype)


def _depthwise_conv(ubuf, rows, w_dw_ref, b_dw_ref):
    out = None
    for phase in range(SUBLANES):
        ext = rows + (SUBLANES if phase else 0)
        partial = None
        for start in range(0, HALO_ROWS + 1, SUBLANES):
            k = start + phase - HALO_PAD
            if 0 <= k < CONV_W:
                term = ubuf[:, start:start + ext, :] * w_dw_ref[k:k + 1, :]
                partial = term if partial is None else partial + term
        shifted = partial[:, phase:phase + rows, :]
        out = shifted if out is None else out + shifted
    return out + b_dw_ref[...]


def _conv_branch(dw, g_cn_ref, b_cn_ref, g_oc_ref):
    mu = jnp.mean(dw, axis=-1, keepdims=True)
    xc = dw - mu
    y = xc * lax.rsqrt(jnp.mean(xc * xc, axis=-1, keepdims=True) + EPS) * g_cn_ref[...] + b_cn_ref[...]
    return _rms(y * jax.nn.sigmoid(y), g_oc_ref[...])


def _prompt_proj_kernel(x_ref, q_tab_ref, k_tab_ref, ln_mix_ref, w_in_ref, g_q_ref, w_uq_ref,
                        g_kv_ref, w_uk_ref, w_uvt_ref, v_one_ref, w_dw_ref, b_dw_ref, g_cn_ref, b_cn_ref,
                        g_oc_ref,
                        q_ref, k_ref, vt_ref, ckv_ref, kpe_t_ref, cn_ref, ncv_ref, ubuf, *, tiles_per_seq):
    rows = x_ref.shape[0]
    q_lora, kv_lora, conv_ch = g_q_ref.shape[1], g_kv_ref.shape[1], g_cn_ref.shape[1]
    cq, ckv, kpe, a, gate = _in_proj(x_ref, ln_mix_ref, w_in_ref, (q_lora, kv_lora, LANES, conv_ch, conv_ch))
    _query(cq, g_q_ref, w_uq_ref, q_tab_ref[...], q_ref)

    ckv = _rms(ckv, g_kv_ref[...])
    ckv_ref[...] = ckv
    ckv_b = ckv.astype(_BF16)
    kpe = _rope(kpe, k_tab_ref[...], 0.0)
    kpe_t_ref[0] = kpe.T[ROPE_LO:ROPE_HI, :]
    k_nope = _dot(ckv_b, w_uk_ref[...])
    for h in range(MLA_HEADS):
        k_ref[h] = (k_nope[:, h * LANES:(h + 1) * LANES] + kpe).astype(k_ref.dtype)
    vt = _dot_nt(w_uvt_ref[...], ckv_b) + v_one_ref[...]
    key_tile = vt_ref.shape[3]
    for h in range(MLA_HEADS):
        for t in range(rows // key_tile):
            vt_ref[h, t] = vt[h * VT_ROWS:(h + 1) * VT_ROWS, t * key_tile:(t + 1) * key_tile].astype(vt_ref.dtype)

    @pl.when(lax.rem(pl.program_id(0), tiles_per_seq) == 0)
    def _():
        ubuf[:, 0:HALO_ROWS, :] = jnp.zeros((1, HALO_ROWS, conv_ch), _F32)

    ubuf[0, HALO_ROWS:HALO_ROWS + rows, :] = a * jax.nn.sigmoid(gate)
    dw = _depthwise_conv(ubuf, rows, w_dw_ref, b_dw_ref)[0]
    cn_ref[...] = _conv_branch(dw, g_cn_ref, b_cn_ref, g_oc_ref).astype(cn_ref.dtype)
    history = ubuf[0, HALO_PAD + rows:HALO_ROWS + rows, :]
    ncv_ref[0] = history
    ubuf[0, HALO_PAD:HALO_ROWS, :] = history


def _sample_proj_kernel(x_ref, q_tab_ref, k_tab_ref, state_ref, ln_mix_ref, w_in_ref, g_q_ref,
                        w_uq_ref, g_kv_ref, w_dw_ref, b_dw_ref, g_cn_ref, b_cn_ref, g_oc_ref,
                        q_ref, ckv_ref, kpe_ref, cn_ref, ncv_ref, ubuf):
    segs, seg_rows = ubuf.shape[0], ubuf.shape[1] - HALO_ROWS
    q_lora, kv_lora, conv_ch = g_q_ref.shape[1], g_kv_ref.shape[1], g_cn_ref.shape[1]
    cq, ckv, kpe, a, gate = _in_proj(x_ref, ln_mix_ref, w_in_ref, (q_lora, kv_lora, LANES, conv_ch, conv_ch))
    _query(cq, g_q_ref, w_uq_ref, q_tab_ref[...], q_ref)
    ckv_ref[...] = _rms(ckv, g_kv_ref[...])
    kpe_ref[...] = _rope(kpe, k_tab_ref[...], 0.0)

    ubuf[:, 0:HALO_PAD, :] = jnp.zeros((segs, HALO_PAD, conv_ch), _F32)
    ubuf[:, HALO_PAD:HALO_ROWS, :] = state_ref[...]
    ubuf[:, HALO_ROWS:, :] = (a * jax.nn.sigmoid(gate)).reshape(segs, seg_rows, conv_ch)
    dw = _depthwise_conv(ubuf, seg_rows, w_dw_ref, b_dw_ref).reshape(segs * seg_rows, conv_ch)
    cn_ref[...] = _conv_branch(dw, g_cn_ref, b_cn_ref, g_oc_ref).astype(cn_ref.dtype)
    ncv_ref[...] = ubuf[:, HALO_PAD + seg_rows:, :]


def _whole(shape):
    zeros = (0,) * len(shape)
    return pl.BlockSpec(shape, lambda *_: zeros)


def _prompt_proj(x, tabs, wts, seq):
    n, d = x.shape
    tm = min(ROW_TILE, seq)
    key_tile = min(ATTN_TILE, seq)
    assert tm % key_tile == 0 and seq % tm == 0
    tiles_per_seq = seq // tm
    conv_ch = wts["g_cn"].shape[1]
    kv_lora = wts["g_kv"].shape[1]
    row_block = lambda w: pl.BlockSpec((tm, w), lambda i: (i, 0))
    head_block = pl.BlockSpec((MLA_HEADS, tm, LANES), lambda i: (0, i, 0))
    vt_block = pl.BlockSpec((MLA_HEADS, tm // key_tile, VT_ROWS, key_tile), lambda i: (0, i, 0, 0))
    tab_block = pl.BlockSpec((tm, LANES), lambda i: (lax.rem(i, tiles_per_seq), 0))
    names = ("ln_mix", "w_in", "g_q", "w_uq", "g_kv", "w_uk", "w_uvt", "v_one", "w_dw", "b_dw", "g_cn",
             "b_cn", "g_oc")
    head_shape = jax.ShapeDtypeStruct((MLA_HEADS, n, LANES), _BF16)
    return pl.pallas_call(
        functools.partial(_prompt_proj_kernel, tiles_per_seq=tiles_per_seq),
        grid=(n // tm,),
        in_specs=[row_block(d), tab_block, tab_block] + [_whole(wts[k].shape) for k in names],
        out_specs=(head_block, head_block, vt_block, row_block(kv_lora),
                   pl.BlockSpec((1, ROPE_DIM, tm), lambda i: (i // tiles_per_seq, 0, lax.rem(i, tiles_per_seq))),
                   row_block(conv_ch), pl.BlockSpec((1, CONV_STATE, conv_ch), lambda i: (i // tiles_per_seq, 0, 0))),
        out_shape=(head_shape, head_shape,
                   jax.ShapeDtypeStruct((MLA_HEADS, n // key_tile, VT_ROWS, key_tile), _BF16),
                   jax.ShapeDtypeStruct((n, kv_lora), _F32), jax.ShapeDtypeStruct((n // seq, ROPE_DIM, seq), _F32),
                   jax.ShapeDtypeStruct((n, conv_ch), _BF16),
                   jax.ShapeDtypeStruct((n // seq, CONV_STATE, conv_ch), _F32)),
        scratch_shapes=[pltpu.VMEM((1, HALO_ROWS + tm, conv_ch), _F32)],
        compiler_params=pltpu.CompilerParams(dimension_semantics=("arbitrary",),
                                             vmem_limit_bytes=VMEM_LIMIT_BYTES),
        name="prompt_proj",
    )(x, *tabs, *[wts[k] for k in names])


def _sample_proj(x, tabs, state, wts, seq):
    n, d = x.shape
    conv_ch = wts["g_cn"].shape[1]
    kv_lora = wts["g_kv"].shape[1]
    names = ("ln_mix", "w_in", "g_q", "w_uq", "g_kv", "w_dw", "b_dw", "g_cn", "b_cn", "g_oc")
    return pl.pallas_call(
        _sample_proj_kernel,
        out_shape=(jax.ShapeDtypeStruct((MLA_HEADS, n, LANES), _BF16),
                   jax.ShapeDtypeStruct((n, kv_lora), _F32), jax.ShapeDtypeStruct((n, LANES), _F32),
                   jax.ShapeDtypeStruct((n, conv_ch), _BF16),
                   jax.ShapeDtypeStruct((n // seq, CONV_STATE, conv_ch), _F32)),
        scratch_shapes=[pltpu.VMEM((n // seq, HALO_ROWS + seq, conv_ch), _F32)],
        compiler_params=pltpu.CompilerParams(vmem_limit_bytes=VMEM_LIMIT_BYTES),
        name="sample_proj",
    )(x, *tabs, state, *[wts[k] for k in names])


def _software_pipeline(items, issue, finish):
    items = list(items)
    issued, done = [], []
    for t in range(len(items) + SCORE_LOOKAHEAD):
        if t < len(items):
            issued.append(issue(items[t]))
        if t >= SCORE_LOOKAHEAD:
            done.append(finish(items[t - SCORE_LOOKAHEAD], issued[t - SCORE_LOOKAHEAD]))
    return done


def _prompt_attn_kernel(q_ref, k_ref, vt_ref, g_om_ref, o_ref, *, sub):
    tile = q_ref.shape[1]
    i = pl.program_id(1)
    chunk_of = lambda t: lax.shift_right_logical(t, CHUNK.bit_length() - 1)

    def key_rows(j):
        return pl.ds(pl.multiple_of(j * tile, tile), tile)

    spans = []
    for r in range(tile // sub):
        n_keys = (r + 1) * sub
        key = lax.broadcasted_iota(jnp.int32, (n_keys, sub), 0)
        query = lax.broadcasted_iota(jnp.int32, (n_keys, sub), 1) + r * sub
        spans.append((slice(r * sub, (r + 1) * sub), n_keys, chunk_of(key) <= chunk_of(query)))

    def diag_scores(h, span):
        q_rows, n_keys, mask = span
        keys = pl.ds(pl.multiple_of(i * tile, tile), n_keys)
        return jnp.where(mask, _dot_nt(k_ref[h, keys, :], q_ref[h, q_rows, :]), MASK_VALUE)

    def diag_finish(h, span, s):
        m = jnp.max(s, axis=0, keepdims=True)
        return m, _dot(vt_ref[h, i, :, 0:span[1]], jnp.exp2(s - m).astype(_BF16))

    items = [(h, span) for h in range(MLA_HEADS) for span in spans]
    done = _software_pipeline(items, lambda it: diag_scores(*it), lambda it, s: diag_finish(*it, s))
    per_head = len(spans)
    maxes = [jnp.concatenate([m for m, _ in done[h * per_head:(h + 1) * per_head]], axis=1)
             for h in range(MLA_HEADS)]
    accs = [jnp.concatenate([a for _, a in done[h * per_head:(h + 1) * per_head]], axis=1)
            for h in range(MLA_HEADS)]

    def body(j, carry):
        state = [list(carry[0]), list(carry[1])]
        part = tile // KEY_PARTS

        def issue(item):
            h, t = item
            rows = pl.ds(pl.multiple_of(j * tile + t * part, part), part)
            return _dot_nt(k_ref[h, rows, :], q_ref[h])

        def finish(item, s):
            h, t = item
            m_old, acc = state[0][h], state[1][h]
            m_new = jnp.maximum(m_old, jnp.max(s, axis=0, keepdims=True))
            p = jnp.exp2(s - m_new).astype(_BF16)
            state[0][h] = m_new
            state[1][h] = acc * jnp.exp2(m_old - m_new) + _dot(vt_ref[h, j, :, t * part:(t + 1) * part], p)

        _software_pipeline([(h, t) for h in range(MLA_HEADS) for t in range(KEY_PARTS)], issue, finish)
        return tuple(state[0]), tuple(state[1])

    _, accs = lax.fori_loop(0, i, body, (tuple(maxes), tuple(accs)))

    o_t = jnp.concatenate([acc[0:V_DIM] / acc[V_DIM:V_DIM + 1] for acc in accs], axis=0)
    o_ref[...] = _rms(o_t.T, g_om_ref[...]).astype(o_ref.dtype)


def _prompt_attn(q, k, vt, g_om, batch, seq):
    tile = vt.shape[3]
    nq = seq // tile
    width = g_om.shape[1]
    return pl.pallas_call(
        functools.partial(_prompt_attn_kernel, sub=min(ATTN_DIAG_TILE, tile)),
        grid=(batch, nq),
        in_specs=[pl.BlockSpec((MLA_HEADS, tile, LANES), lambda b, i: (0, b * nq + i, 0)),
                  pl.BlockSpec((MLA_HEADS, seq, LANES), lambda b, i: (0, b, 0)),
                  pl.BlockSpec((MLA_HEADS, nq, VT_ROWS, tile), lambda b, i: (0, b, 0, 0)),
                  _whole(g_om.shape)],
        out_specs=pl.BlockSpec((tile, width), lambda b, i: (b * nq + i, 0)),
        out_shape=jax.ShapeDtypeStruct((batch * seq, width), _BF16),
        compiler_params=pltpu.CompilerParams(dimension_semantics=("arbitrary", "arbitrary"),
                                             vmem_limit_bytes=VMEM_LIMIT_BYTES),
        name="prompt_attn",
    )(q, k, vt, g_om)


def _sample_attn_kernel(q_ref, ckv_new_ref, kpe_new_ref, ckv_past_ref, kpe_past_t_ref, w_ukt_ref, w_uvh_ref,
                        g_om_ref, o_ref):
    seq = q_ref.shape[1]
    q_all = jnp.concatenate([q_ref[h] for h in range(MLA_HEADS)], axis=0)
    q_lat = jnp.concatenate([_dot(q_ref[h], w_ukt_ref[h]) for h in range(MLA_HEADS)], axis=0).astype(_BF16)
    q_pe = q_all[:, ROPE_LO:ROPE_HI]
    c_past = ckv_past_ref[0].astype(_BF16)
    c_new = ckv_new_ref[...].astype(_BF16)
    s_past = _dot_nt(q_lat, c_past) + _dot(q_pe, kpe_past_t_ref[0].astype(_BF16))
    s_new = _dot_nt(q_lat, c_new) + _dot_nt(q_all, kpe_new_ref[...].astype(_BF16))
    m = jnp.maximum(jnp.max(s_past, axis=-1, keepdims=True), jnp.max(s_new, axis=-1, keepdims=True))
    p_past = jnp.exp2(s_past - m)
    p_new = jnp.exp2(s_new - m)
    denom = jnp.sum(p_past, axis=-1, keepdims=True) + jnp.sum(p_new, axis=-1, keepdims=True)
    o_lat = ((_dot(p_past.astype(_BF16), c_past) + _dot(p_new.astype(_BF16), c_new)) / denom).astype(_BF16)
    o = sum(_dot(o_lat[h * seq:(h + 1) * seq], w_uvh_ref[h]) for h in range(MLA_HEADS))
    o_ref[...] = _rms(o, g_om_ref[...]).astype(o_ref.dtype)


def _sample_attn(q, ckv_new, kpe_new, ckv_past, kpe_past_t, w_ukt, w_uvh, g_om, batch, seq):
    past, kv_lora = ckv_past.shape[1:]
    width = g_om.shape[1]
    return pl.pallas_call(
        _sample_attn_kernel,
        grid=(batch,),
        in_specs=[pl.BlockSpec((MLA_HEADS, seq, LANES), lambda b: (0, b, 0)),
                  pl.BlockSpec((seq, kv_lora), lambda b: (b, 0)),
                  pl.BlockSpec((seq, LANES), lambda b: (b, 0)),
                  pl.BlockSpec((1, past, kv_lora), lambda b: (b, 0, 0)),
                  pl.BlockSpec((1, ROPE_DIM, past), lambda b: (b, 0, 0)),
                  _whole(w_ukt.shape), _whole(w_uvh.shape), _whole(g_om.shape)],
        out_specs=pl.BlockSpec((seq, width), lambda b: (b, 0)),
        out_shape=jax.ShapeDtypeStruct((batch * seq, width), _BF16),
        compiler_params=pltpu.CompilerParams(dimension_semantics=("arbitrary",),
                                             vmem_limit_bytes=VMEM_LIMIT_BYTES),
        name="sample_attn",
    )(q, ckv_new, kpe_new, ckv_past, kpe_past_t, w_ukt, w_uvh, g_om)


def _output_kernel(x_ref, an_ref, cn_ref, w_out_a_ref, w_out_c_ref, ln_ffn_ref, w_gate_ref, w_up_ref,
                   w_down_ref, g_final_ref, y_ref):
    h = x_ref[...] + _dot(an_ref[...], w_out_a_ref[...]) + _dot(cn_ref[...], w_out_c_ref[...])
    f = _rms(h, ln_ffn_ref[...]).astype(_BF16)
    gate = _dot(f, w_gate_ref[...])
    act = (gate * jax.nn.sigmoid(gate) * _dot(f, w_up_ref[...])).astype(_BF16)
    h = h + _dot(act, w_down_ref[...])
    y_ref[...] = _rms(h, g_final_ref[...])


def _output(x, an, cn, wts, name):
    n, d = x.shape
    tm = min(ROW_TILE, n)
    names = ("w_out_a", "w_out_c", "ln_ffn", "w_gate", "w_up", "w_down", "g_final")
    row_block = lambda w: pl.BlockSpec((tm, w), lambda i: (i, 0))
    resident = lambda shape: pl.BlockSpec(shape, lambda i: (0,) * len(shape), pipeline_mode=pl.Buffered(1))
    return pl.pallas_call(
        _output_kernel,
        grid=(n // tm,),
        in_specs=[row_block(d), row_block(an.shape[1]), row_block(cn.shape[1])]
                 + [resident(wts[k].shape) for k in names],
        out_specs=row_block(d),
        out_shape=jax.ShapeDtypeStruct((n, d), _F32),
        compiler_params=pltpu.CompilerParams(dimension_semantics=("arbitrary",),
                                             vmem_limit_bytes=VMEM_LIMIT_BYTES),
        name=name,
    )(x, an, cn, *[wts[k] for k in names])


def _pad_lanes(w, left):
    return jnp.pad(w, [(0, 0)] * (w.ndim - 1) + [(left, LANES - left - w.shape[-1])])


def _prepare_weights(ln_mix, w_in, g_q, w_uq, g_kv, w_uk, w_uv, w_dw, b_dw, g_cn, b_cn, g_om, g_oc, w_out,
                     ln_ffn, w_gate, w_up, w_down, g_final):
    q_lora, kv_lora = g_q.shape[0], g_kv.shape[0]
    d_model = w_in.shape[0]
    mla_width = g_om.shape[0]
    row = lambda v: v.reshape(1, -1)
    c0, c1, c2 = q_lora, q_lora + kv_lora, q_lora + kv_lora + ROPE_DIM
    with_swap = lambda w: jnp.concatenate([w, w[..., HALF_ROPE:], w[..., :HALF_ROPE]], axis=-1)
    w_in_p = jnp.concatenate([w_in[:, :c1], _pad_lanes(with_swap(w_in[:, c1:c2]), ROPE_LO), w_in[:, c2:]], axis=1)
    w_uq_h = w_uq.reshape(q_lora, MLA_HEADS, NOPE_DIM + ROPE_DIM)
    w_uq_p = jnp.concatenate([w_uq_h[..., :NOPE_DIM], with_swap(w_uq_h[..., NOPE_DIM:])], axis=-1)
    w_uq_p = w_uq_p.reshape(q_lora, -1)
    w_uk_p = _pad_lanes(w_uk, 0).reshape(kv_lora, -1)
    w_uvt = jnp.pad(jnp.transpose(w_uv, (1, 2, 0)), ((0, 0), (0, VT_ROWS - V_DIM), (0, 0)))
    v_one = jnp.zeros((MLA_HEADS, VT_ROWS, 1), _F32).at[:, V_DIM].set(1.0)
    w_ukt = jnp.pad(jnp.transpose(w_uk, (1, 2, 0)), ((0, 0), (0, LANES - NOPE_DIM), (0, 0)))
    w_uvh = jnp.stack([jnp.pad(w_uv[:, h], ((0, 0), (h * V_DIM, (MLA_HEADS - 1 - h) * V_DIM)))
                       for h in range(MLA_HEADS)])
    return {
        "ln_mix": row(ln_mix), "w_in": w_in_p.astype(_BF16), "g_q": row(g_q), "w_uq": w_uq_p.astype(_BF16),
        "g_kv": row(g_kv), "w_uk": w_uk_p.astype(_BF16), "w_uvt": w_uvt.reshape(-1, kv_lora).astype(_BF16),
        "v_one": v_one.reshape(-1, 1), "w_dw": w_dw, "b_dw": row(b_dw), "g_cn": row(g_cn), "b_cn": row(b_cn),
        "g_om": row(g_om), "g_oc": row(g_oc),
        "w_ukt": w_ukt.astype(_BF16), "w_uvh": w_uvh.astype(_BF16),
        "w_out_a": w_out[:mla_width].astype(_BF16), "w_out_c": w_out[mla_width:].astype(_BF16),
        "ln_ffn": row(ln_ffn), "w_gate": w_gate.astype(_BF16), "w_up": w_up.astype(_BF16),
        "w_down": w_down.astype(_BF16), "g_final": row(g_final),
    }


def _layer(x_prompt, x_sample, ckv_past, kpe_past, conv_past, wts):
    batch, seq, d = x_prompt.shape
    dec_batch, dec_seq, _ = x_sample.shape
    past = ckv_past.shape[1]

    xp = x_prompt.reshape(batch * seq, d)
    q, k, vt, kv_p, kr_p_t, cn, cv_p = _prompt_proj(xp, _rope_tables(jnp.arange(seq)), wts, seq)
    an = _prompt_attn(q, k, vt, wts["g_om"], batch, seq)
    y_p = _output(xp, an, cn, wts, "prompt_output")

    xs = x_sample.reshape(dec_batch * dec_seq, d)
    tabs = _rope_tables(jnp.tile(past + jnp.arange(dec_seq), dec_batch))
    q, kv_s, kr_s, cn, cv_s = _sample_proj(xs, tabs, conv_past, wts, dec_seq)
    an = _sample_attn(q, kv_s, kr_s, ckv_past, jnp.swapaxes(kpe_past, 1, 2), wts["w_ukt"], wts["w_uvh"],
                      wts["g_om"], dec_batch, dec_seq)
    y_s = _output(xs, an, cn, wts, "sample_output")

    return (y_p.reshape(batch, seq, d), y_s.reshape(dec_batch, dec_seq, d),
            kv_p.reshape(batch, seq, -1), jnp.swapaxes(kr_p_t, 1, 2), cv_p,
            kv_s.reshape(dec_batch, dec_seq, -1),
            kr_s[:, ROPE_LO:ROPE_HI].reshape(dec_batch, dec_seq, ROPE_DIM), cv_s)


def kernel(x_prompt, x_sample, cache_kv_latent, cache_k_rope, state_conv, ln_mix, w_in, g_q, w_uq, g_kv, w_uk, w_uv, w_dw, b_dw, g_cn, b_cn, g_om, g_oc, w_out, ln_ffn, w_gate, w_up, w_down, g_final):
    depth = w_in.shape[0]
    assert depth == 1, "the kernel implements the single-layer model of the problem"
    wts = _prepare_weights(ln_mix[0], w_in[0], g_q[0], w_uq[0], g_kv[0], w_uk[0], w_uv[0], w_dw[0], b_dw[0],
                           g_cn[0], b_cn[0], g_om[0], g_oc[0], w_out[0], ln_ffn[0], w_gate[0], w_up[0],
                           w_down[0], g_final)
    outs = _layer(x_prompt, x_sample, cache_kv_latent[0], cache_k_rope[0], state_conv[0], wts)
    y_p, y_s = outs[0], outs[1]
    return (y_p, y_s) + tuple(o[None] for o in outs[2:])
```

```python
import functools
import math

import jax
import jax.numpy as jnp
from jax import lax
from jax.experimental import pallas as pl
from jax.experimental.pallas import tpu as pltpu

CHUNK = 64
MLA_HEADS = 8
NOPE_DIM = 64
ROPE_DIM = 32
V_DIM = 64
ROPE_THETA = 10000.0
EPS = 1e-6
CONV_W = 31
CONV_STATE = CONV_W - 1
ATTN_SCALE = 1.0 / math.sqrt(NOPE_DIM + ROPE_DIM)
MASK_VALUE = -1e30

LANES = 128
SUBLANES = 8
HALF_ROPE = ROPE_DIM // 2
ROPE_LO = NOPE_DIM
ROPE_HI = ROPE_LO + ROPE_DIM
assert ROPE_HI + ROPE_DIM == LANES
VT_ROWS = 80
HALO_ROWS = 32
HALO_PAD = HALO_ROWS - CONV_STATE
Q_SCALE = ATTN_SCALE * math.log2(math.e)

ROW_TILE = 512
ATTN_TILE = 512
KEY_PARTS = 2
SCORE_LOOKAHEAD = 4
ATTN_DIAG_TILE = 256
VMEM_LIMIT_BYTES = 56 * 1024 * 1024

_BF16 = jnp.bfloat16
_F32 = jnp.float32


def _rms(x, g):
    return x * lax.rsqrt(jnp.mean(x * x, axis=-1, keepdims=True) + EPS) * g


def _dot(a, b):
    return jnp.dot(a, b, preferred_element_type=_F32)


def _dot_nt(a, b):
    return lax.dot_general(a, b, (((1,), (1,)), ((), ())), preferred_element_type=_F32)


def _rope(x, table, rest):
    y = x * table
    lane = lax.broadcasted_iota(jnp.int32, x.shape, 1)
    in_rope = (lane >= ROPE_LO) & (lane < ROPE_HI)
    return jnp.where(in_rope, y + pltpu.roll(y, LANES - ROPE_DIM, 1), y if rest is None else rest)


def _rope_table_kernel(pos_ref, inv_ref, q_tab_ref, k_tab_ref):
    lane = lax.broadcasted_iota(jnp.int32, k_tab_ref.shape, 1)
    ang = pos_ref[...] * inv_ref[...]
    c = jnp.cos(ang)
    s = jnp.sin(ang)
    rot = jnp.where(lane < ROPE_HI, c, jnp.where(lane < ROPE_HI + HALF_ROPE, -s, s))
    k_tab_ref[...] = rot
    q_tab_ref[...] = jnp.where(lane < ROPE_LO, Q_SCALE, rot * Q_SCALE)


def _rope_tables(pos):
    inv = 1.0 / (ROPE_THETA ** (jnp.arange(0, ROPE_DIM, 2, dtype=_F32) / ROPE_DIM))
    inv_lanes = jnp.zeros((1, LANES), _F32).at[0, ROPE_LO:].set(jnp.tile(inv, 4))
    out = jax.ShapeDtypeStruct((pos.shape[0], LANES), _F32)
    return pl.pallas_call(
        _rope_table_kernel,
        out_shape=(out, out),
        name="rope_table",
    )(pos.astype(_F32).reshape(-1, 1), inv_lanes)


def _in_proj(x_ref, ln_mix_ref, w_in_ref, widths):
    hn = _rms(x_ref[...], ln_mix_ref[...]).astype(_BF16)
    outs, start = [], 0
    for w in widths:
        outs.append(_dot(hn, w_in_ref[:, start:start + w]))
        start += w
    return outs


def _query(cq, g_q_ref, w_uq_ref, q_tab, q_ref):
    q = _dot(_rms(cq, g_q_ref[...]).astype(_BF16), w_uq_ref[...])
    for h in range(MLA_HEADS):
        q_ref[h] = _rope(q[:, h * LANES:(h + 1) * LANES], q_tab, None).astype(q_ref.dtype)


def _depthwise_conv(ubuf, first, rows, w_dw_ref, b_dw_ref):
    out = None
    for phase in range(SUBLANES):
        ext = rows + (SUBLANES if phase else 0)
        partial = None
        for start in range(0, HALO_ROWS + 1, SUBLANES):
            k = start + phase - HALO_PAD
            if 0 <= k < CONV_W:
                term = ubuf[:, first + start:first + start + ext, :] * w_dw_ref[k:k + 1, :]
                partial = term if partial is None else partial + term
        shifted = partial[:, phase:phase + rows, :]
        out = shifted if out is None else out + shifted
    return out + b_dw_ref[...]


def _conv_branch(dw, g_cn_ref, b_cn_ref, g_oc_ref):
    mu = jnp.mean(dw, axis=-1, keepdims=True)
    xc = dw - mu
    y = xc * lax.rsqrt(jnp.mean(xc * xc, axis=-1, keepdims=True) + EPS) * g_cn_ref[...] + b_cn_ref[...]
    return _rms(y * jax.nn.sigmoid(y), g_oc_ref[...])


def _prompt_proj_kernel(x_ref, q_tab_ref, k_tab_ref, ln_mix_ref, w_in_ref, g_q_ref, w_uq_ref,
                        g_kv_ref, w_uk_ref, w_uvt_ref, v_one_ref,
                        q_ref, k_ref, vt_ref, ckv_ref, kpe_t_ref, u_ref, ncv_ref):
    rows = x_ref.shape[0]
    q_lora, kv_lora, conv_ch = g_q_ref.shape[1], g_kv_ref.shape[1], u_ref.shape[1]
    cq, ckv, kpe, a, gate = _in_proj(x_ref, ln_mix_ref, w_in_ref, (q_lora, kv_lora, LANES, conv_ch, conv_ch))
    _query(cq, g_q_ref, w_uq_ref, q_tab_ref[...], q_ref)

    ckv = _rms(ckv, g_kv_ref[...])
    ckv_ref[...] = ckv
    ckv_b = ckv.astype(_BF16)
    kpe = _rope(kpe, k_tab_ref[...], 0.0)
    kpe_t_ref[0] = kpe.T[ROPE_LO:ROPE_HI, :]
    k_nope = _dot(ckv_b, w_uk_ref[...])
    for h in range(MLA_HEADS):
        k_ref[h] = (k_nope[:, h * LANES:(h + 1) * LANES] + kpe).astype(k_ref.dtype)
    vt = _dot_nt(w_uvt_ref[...], ckv_b) + v_one_ref[...]
    key_tile = vt_ref.shape[3]
    for h in range(MLA_HEADS):
        for t in range(rows // key_tile):
            vt_ref[h, t] = vt[h * VT_ROWS:(h + 1) * VT_ROWS, t * key_tile:(t + 1) * key_tile].astype(vt_ref.dtype)

    u = a * jax.nn.sigmoid(gate)
    u_ref[...] = u
    ncv_ref[0] = u[rows - CONV_STATE:, :]


def _sample_proj_kernel(x_ref, q_tab_ref, k_tab_ref, ln_mix_ref, w_in_ref, g_q_ref, w_uq_ref, g_kv_ref,
                        q_ref, ckv_ref, kpe_ref, u_ref):
    q_lora, kv_lora, conv_ch = g_q_ref.shape[1], g_kv_ref.shape[1], u_ref.shape[1]
    cq, ckv, kpe, a, gate = _in_proj(x_ref, ln_mix_ref, w_in_ref, (q_lora, kv_lora, LANES, conv_ch, conv_ch))
    _query(cq, g_q_ref, w_uq_ref, q_tab_ref[...], q_ref)
    ckv_ref[...] = _rms(ckv, g_kv_ref[...])
    kpe_ref[...] = _rope(kpe, k_tab_ref[...], 0.0)
    u_ref[...] = a * jax.nn.sigmoid(gate)


def _whole(shape):
    zeros = (0,) * len(shape)
    return pl.BlockSpec(shape, lambda *_: zeros)


def _prompt_proj(x, tabs, wts, seq):
    n, d = x.shape
    tm = min(ROW_TILE, seq)
    key_tile = min(ATTN_TILE, seq)
    assert tm % key_tile == 0 and seq % tm == 0
    tiles_per_seq = seq // tm
    conv_ch = wts["g_cn"].shape[1]
    kv_lora = wts["g_kv"].shape[1]
    row_block = lambda w: pl.BlockSpec((tm, w), lambda i: (i, 0))
    head_block = pl.BlockSpec((MLA_HEADS, tm, LANES), lambda i: (0, i, 0))
    vt_block = pl.BlockSpec((MLA_HEADS, tm // key_tile, VT_ROWS, key_tile), lambda i: (0, i, 0, 0))
    tab_block = pl.BlockSpec((tm, LANES), lambda i: (lax.rem(i, tiles_per_seq), 0))
    names = ("ln_mix", "w_in", "g_q", "w_uq", "g_kv", "w_uk", "w_uvt", "v_one")
    head_shape = jax.ShapeDtypeStruct((MLA_HEADS, n, LANES), _BF16)
    return pl.pallas_call(
        _prompt_proj_kernel,
        grid=(n // tm,),
        in_specs=[row_block(d), tab_block, tab_block] + [_whole(wts[k].shape) for k in names],
        out_specs=(head_block, head_block, vt_block, row_block(kv_lora),
                   pl.BlockSpec((1, ROPE_DIM, tm), lambda i: (i // tiles_per_seq, 0, lax.rem(i, tiles_per_seq))),
                   row_block(conv_ch), pl.BlockSpec((1, CONV_STATE, conv_ch), lambda i: (i // tiles_per_seq, 0, 0))),
        out_shape=(head_shape, head_shape,
                   jax.ShapeDtypeStruct((MLA_HEADS, n // key_tile, VT_ROWS, key_tile), _BF16),
                   jax.ShapeDtypeStruct((n, kv_lora), _F32), jax.ShapeDtypeStruct((n // seq, ROPE_DIM, seq), _F32),
                   jax.ShapeDtypeStruct((n, conv_ch), _F32),
                   jax.ShapeDtypeStruct((n // seq, CONV_STATE, conv_ch), _F32)),
        compiler_params=pltpu.CompilerParams(dimension_semantics=("arbitrary",),
                                             vmem_limit_bytes=VMEM_LIMIT_BYTES),
        name="prompt_proj",
    )(x, *tabs, *[wts[k] for k in names])


def _sample_proj(x, tabs, wts):
    n, d = x.shape
    conv_ch = wts["g_cn"].shape[1]
    kv_lora = wts["g_kv"].shape[1]
    names = ("ln_mix", "w_in", "g_q", "w_uq", "g_kv")
    return pl.pallas_call(
        _sample_proj_kernel,
        out_shape=(jax.ShapeDtypeStruct((MLA_HEADS, n, LANES), _BF16),
                   jax.ShapeDtypeStruct((n, kv_lora), _F32), jax.ShapeDtypeStruct((n, LANES), _F32),
                   jax.ShapeDtypeStruct((n, conv_ch), _F32)),
        compiler_params=pltpu.CompilerParams(vmem_limit_bytes=VMEM_LIMIT_BYTES),
        name="sample_proj",
    )(x, *tabs, *[wts[k] for k in names])


def _software_pipeline(items, issue, finish):
    items = list(items)
    issued, done = [], []
    for t in range(len(items) + SCORE_LOOKAHEAD):
        if t < len(items):
            issued.append(issue(items[t]))
        if t >= SCORE_LOOKAHEAD:
            done.append(finish(items[t - SCORE_LOOKAHEAD], issued[t - SCORE_LOOKAHEAD]))
    return done


def _prompt_attn_kernel(q_ref, k_ref, vt_ref, g_om_ref, o_ref, *, sub):
    tile = q_ref.shape[1]
    i = pl.program_id(1)
    chunk_of = lambda t: lax.shift_right_logical(t, CHUNK.bit_length() - 1)

    def key_rows(j):
        return pl.ds(pl.multiple_of(j * tile, tile), tile)

    spans = []
    for r in range(tile // sub):
        n_keys = (r + 1) * sub
        key = lax.broadcasted_iota(jnp.int32, (n_keys, sub), 0)
        query = lax.broadcasted_iota(jnp.int32, (n_keys, sub), 1) + r * sub
        spans.append((slice(r * sub, (r + 1) * sub), n_keys, chunk_of(key) <= chunk_of(query)))

    def diag_scores(h, span):
        q_rows, n_keys, mask = span
        keys = pl.ds(pl.multiple_of(i * tile, tile), n_keys)
        return jnp.where(mask, _dot_nt(k_ref[h, keys, :], q_ref[h, q_rows, :]), MASK_VALUE)

    def diag_finish(h, span, s):
        m = jnp.max(s, axis=0, keepdims=True)
        return m, _dot(vt_ref[h, i, :, 0:span[1]], jnp.exp2(s - m).astype(_BF16))

    items = [(h, span) for h in range(MLA_HEADS) for span in spans]
    done = _software_pipeline(items, lambda it: diag_scores(*it), lambda it, s: diag_finish(*it, s))
    per_head = len(spans)
    maxes = [jnp.concatenate([m for m, _ in done[h * per_head:(h + 1) * per_head]], axis=1)
             for h in range(MLA_HEADS)]
    accs = [jnp.concatenate([a for _, a in done[h * per_head:(h + 1) * per_head]], axis=1)
            for h in range(MLA_HEADS)]

    def body(j, carry):
        state = [list(carry[0]), list(carry[1])]
        part = tile // KEY_PARTS

        def issue(item):
            h, t = item
            rows = pl.ds(pl.multiple_of(j * tile + t * part, part), part)
            return _dot_nt(k_ref[h, rows, :], q_ref[h])

        def finish(item, s):
            h, t = item
            m_old, acc = state[0][h], state[1][h]
            m_new = jnp.maximum(m_old, jnp.max(s, axis=0, keepdims=True))
            p = jnp.exp2(s - m_new).astype(_BF16)
            state[0][h] = m_new
            state[1][h] = acc * jnp.exp2(m_old - m_new) + _dot(vt_ref[h, j, :, t * part:(t + 1) * part], p)

        _software_pipeline([(h, t) for h in range(MLA_HEADS) for t in range(KEY_PARTS)], issue, finish)
        return tuple(state[0]), tuple(state[1])

    _, accs = lax.fori_loop(0, i, body, (tuple(maxes), tuple(accs)))

    o_t = jnp.concatenate([acc[0:V_DIM] / acc[V_DIM:V_DIM + 1] for acc in accs], axis=0)
    o_ref[...] = _rms(o_t.T, g_om_ref[...]).astype(o_ref.dtype)


def _prompt_attn(q, k, vt, g_om, batch, seq):
    tile = vt.shape[3]
    nq = seq // tile
    width = g_om.shape[1]
    return pl.pallas_call(
        functools.partial(_prompt_attn_kernel, sub=min(ATTN_DIAG_TILE, tile)),
        grid=(batch, nq),
        in_specs=[pl.BlockSpec((MLA_HEADS, tile, LANES), lambda b, i: (0, b * nq + i, 0)),
                  pl.BlockSpec((MLA_HEADS, seq, LANES), lambda b, i: (0, b, 0)),
                  pl.BlockSpec((MLA_HEADS, nq, VT_ROWS, tile), lambda b, i: (0, b, 0, 0)),
                  _whole(g_om.shape)],
        out_specs=pl.BlockSpec((tile, width), lambda b, i: (b * nq + i, 0)),
        out_shape=jax.ShapeDtypeStruct((batch * seq, width), _BF16),
        compiler_params=pltpu.CompilerParams(dimension_semantics=("arbitrary", "arbitrary"),
                                             vmem_limit_bytes=VMEM_LIMIT_BYTES),
        name="prompt_attn",
    )(q, k, vt, g_om)


def _sample_attn_kernel(q_ref, ckv_new_ref, kpe_new_ref, ckv_past_ref, kpe_past_t_ref, w_ukt_ref, w_uvh_ref,
                        g_om_ref, o_ref):
    seq = q_ref.shape[1]
    q_all = jnp.concatenate([q_ref[h] for h in range(MLA_HEADS)], axis=0)
    q_lat = jnp.concatenate([_dot(q_ref[h], w_ukt_ref[h]) for h in range(MLA_HEADS)], axis=0).astype(_BF16)
    q_pe = q_all[:, ROPE_LO:ROPE_HI]
    c_past = ckv_past_ref[0].astype(_BF16)
    c_new = ckv_new_ref[...].astype(_BF16)
    s_past = _dot_nt(q_lat, c_past) + _dot(q_pe, kpe_past_t_ref[0].astype(_BF16))
    s_new = _dot_nt(q_lat, c_new) + _dot_nt(q_all, kpe_new_ref[...].astype(_BF16))
    m = jnp.maximum(jnp.max(s_past, axis=-1, keepdims=True), jnp.max(s_new, axis=-1, keepdims=True))
    p_past = jnp.exp2(s_past - m)
    p_new = jnp.exp2(s_new - m)
    denom = jnp.sum(p_past, axis=-1, keepdims=True) + jnp.sum(p_new, axis=-1, keepdims=True)
    o_lat = ((_dot(p_past.astype(_BF16), c_past) + _dot(p_new.astype(_BF16), c_new)) / denom).astype(_BF16)
    o = sum(_dot(o_lat[h * seq:(h + 1) * seq], w_uvh_ref[h]) for h in range(MLA_HEADS))
    o_ref[...] = _rms(o, g_om_ref[...]).astype(o_ref.dtype)


def _sample_attn(q, ckv_new, kpe_new, ckv_past, kpe_past_t, w_ukt, w_uvh, g_om, batch, seq):
    past, kv_lora = ckv_past.shape[1:]
    width = g_om.shape[1]
    return pl.pallas_call(
        _sample_attn_kernel,
        grid=(batch,),
        in_specs=[pl.BlockSpec((MLA_HEADS, seq, LANES), lambda b: (0, b, 0)),
                  pl.BlockSpec((seq, kv_lora), lambda b: (b, 0)),
                  pl.BlockSpec((seq, LANES), lambda b: (b, 0)),
                  pl.BlockSpec((1, past, kv_lora), lambda b: (b, 0, 0)),
                  pl.BlockSpec((1, ROPE_DIM, past), lambda b: (b, 0, 0)),
                  _whole(w_ukt.shape), _whole(w_uvh.shape), _whole(g_om.shape)],
        out_specs=pl.BlockSpec((seq, width), lambda b: (b, 0)),
        out_shape=jax.ShapeDtypeStruct((batch * seq, width), _BF16),
        compiler_params=pltpu.CompilerParams(dimension_semantics=("arbitrary",),
                                             vmem_limit_bytes=VMEM_LIMIT_BYTES),
        name="sample_attn",
    )(q, ckv_new, kpe_new, ckv_past, kpe_past_t, w_ukt, w_uvh, g_om)


FFN_CHUNK = 256
ANCHOR_DISTANCE = 1
CONV_CHUNK_ROWS = 64
_CONV_NAMES = ("w_dw", "b_dw", "g_cn", "b_cn", "g_oc")
_FFN_NAMES = ("w_out_a", "w_out_c", "ln_ffn", "w_gate", "w_up", "w_down", "g_final")


def _conv_module(ubuf, first, rows, conv_refs):
    w_dw_ref, b_dw_ref, g_cn_ref, b_cn_ref, g_oc_ref = conv_refs
    dw = _depthwise_conv(ubuf, first, rows, w_dw_ref, b_dw_ref)
    dw = dw.reshape(dw.shape[0] * rows, dw.shape[2])
    return _conv_branch(dw, g_cn_ref, b_cn_ref, g_oc_ref)


def _mix_ffn(x, an, cn, ffn_refs, side_work=()):
    w_out_a_ref, w_out_c_ref, ln_ffn_ref, w_gate_ref, w_up_ref, w_down_ref, g_final_ref = ffn_refs
    h = x + _dot(an, w_out_a_ref[...]) + _dot(cn, w_out_c_ref[...])
    f = _rms(h, ln_ffn_ref[...]).astype(_BF16)
    side_work = list(side_work)
    down, anchors = None, []
    for start in range(0, w_gate_ref.shape[1], FFN_CHUNK):
        cols = slice(start, start + FFN_CHUNK)
        gate = _dot(f, w_gate_ref[:, cols])
        act = gate * jax.nn.sigmoid(gate) * _dot(f, w_up_ref[:, cols])
        if len(anchors) >= ANCHOR_DISTANCE:
            act = act + anchors.pop(0)
        part = _dot(act.astype(_BF16), w_down_ref[cols, :])
        down = part if down is None else down + part
        if side_work:
            produced = side_work.pop(0)()
            anchors.append(jnp.minimum(jnp.abs(jnp.sum(produced, keepdims=True)), 0.0))
    assert not side_work, "more side work than FFN chunks"
    h = h + down
    for anchor in anchors:
        h = h + anchor
    return _rms(h, g_final_ref[...])


def _prompt_output_kernel(x_ref, an_ref, u_next_ref, u_halo_ref, u_first_ref, *refs, tiles_per_seq):
    conv_refs, ffn_refs = refs[:len(_CONV_NAMES)], refs[len(_CONV_NAMES):len(_CONV_NAMES) + len(_FFN_NAMES)]
    y_ref, ubuf, cn_buf = refs[len(_CONV_NAMES) + len(_FFN_NAMES):]
    rows = x_ref.shape[0]
    i = pl.program_id(0)

    def fill(u_ref, history):
        ubuf[0, 0:HALO_ROWS, :] = history
        ubuf[0, HALO_ROWS:, :] = u_ref[...]

    def conv_piece(first):
        def work():
            piece = _conv_module(ubuf, first, CONV_CHUNK_ROWS, conv_refs)
            cn_buf[first:first + CONV_CHUNK_ROWS, :] = piece.astype(cn_buf.dtype)
            return piece
        return work

    pieces = [conv_piece(first) for first in range(0, rows, CONV_CHUNK_ROWS)]
    no_history = jnp.zeros(u_halo_ref.shape, _F32)

    @pl.when(i == 0)
    def _():
        fill(u_first_ref, no_history)
        for work in pieces:
            work()

    cn = cn_buf[...]
    starts_sequence = lax.rem(i + 1, tiles_per_seq) == 0
    fill(u_next_ref, jnp.where(starts_sequence, no_history, u_halo_ref[...]))
    y_ref[...] = _mix_ffn(x_ref[...], an_ref[...], cn, ffn_refs, side_work=pieces)


def _prompt_output(x, an, u, wts, seq):
    n, d = x.shape
    tm = min(ROW_TILE, seq)
    n_tiles = n // tm
    conv_ch = u.shape[1]
    halo_blocks = tm // HALO_ROWS
    row_block = lambda w: pl.BlockSpec((tm, w), lambda i: (i, 0))
    resident = lambda shape: pl.BlockSpec(shape, lambda i: (0,) * len(shape), pipeline_mode=pl.Buffered(1))
    names = _CONV_NAMES + _FFN_NAMES
    return pl.pallas_call(
        functools.partial(_prompt_output_kernel, tiles_per_seq=seq // tm),
        grid=(n_tiles,),
        in_specs=[row_block(d), row_block(an.shape[1]),
                  pl.BlockSpec((tm, conv_ch), lambda i: (jnp.minimum(i + 1, n_tiles - 1), 0)),
                  pl.BlockSpec((HALO_ROWS, conv_ch), lambda i: ((i + 1) * halo_blocks - 1, 0)),
                  pl.BlockSpec((tm, conv_ch), lambda i: (0, 0), pipeline_mode=pl.Buffered(1))]
                 + [resident(wts[k].shape) for k in names],
        out_specs=row_block(d),
        out_shape=jax.ShapeDtypeStruct((n, d), _F32),
        scratch_shapes=[pltpu.VMEM((1, HALO_ROWS + tm, conv_ch), _F32), pltpu.VMEM((tm, conv_ch), _BF16)],
        compiler_params=pltpu.CompilerParams(dimension_semantics=("arbitrary",),
                                             vmem_limit_bytes=VMEM_LIMIT_BYTES),
        name="prompt_output",
    )(x, an, u, u, u, *[wts[k] for k in names])


def _sample_output_kernel(x_ref, an_ref, u_ref, state_ref, *refs):
    conv_refs, ffn_refs = refs[:len(_CONV_NAMES)], refs[len(_CONV_NAMES):len(_CONV_NAMES) + len(_FFN_NAMES)]
    y_ref, ncv_ref, ubuf = refs[len(_CONV_NAMES) + len(_FFN_NAMES):]
    segs, seg_rows, conv_ch = ubuf.shape[0], ubuf.shape[1] - HALO_ROWS, ubuf.shape[2]
    ubuf[:, 0:HALO_PAD, :] = jnp.zeros((segs, HALO_PAD, conv_ch), _F32)
    ubuf[:, HALO_PAD:HALO_ROWS, :] = state_ref[...]
    ubuf[:, HALO_ROWS:, :] = u_ref[...].reshape(segs, seg_rows, conv_ch)
    ncv_ref[...] = ubuf[:, HALO_PAD + seg_rows:, :]
    cn = _conv_module(ubuf, 0, seg_rows, conv_refs).astype(_BF16)
    y_ref[...] = _mix_ffn(x_ref[...], an_ref[...], cn, ffn_refs)


def _sample_output(x, an, u, state, wts, seq):
    n, d = x.shape
    conv_ch = u.shape[1]
    names = _CONV_NAMES + _FFN_NAMES
    return pl.pallas_call(
        _sample_output_kernel,
        out_shape=(jax.ShapeDtypeStruct((n, d), _F32), jax.ShapeDtypeStruct(state.shape, _F32)),
        scratch_shapes=[pltpu.VMEM((n // seq, HALO_ROWS + seq, conv_ch), _F32)],
        compiler_params=pltpu.CompilerParams(vmem_limit_bytes=VMEM_LIMIT_BYTES),
        name="sample_output",
    )(x, an, u, state, *[wts[k] for k in names])


def _pad_lanes(w, left):
    return jnp.pad(w, [(0, 0)] * (w.ndim - 1) + [(left, LANES - left - w.shape[-1])])


def _prepare_weights(ln_mix, w_in, g_q, w_uq, g_kv, w_uk, w_uv, w_dw, b_dw, g_cn, b_cn, g_om, g_oc, w_out,
                     ln_ffn, w_gate, w_up, w_down, g_final):
    q_lora, kv_lora = g_q.shape[0], g_kv.shape[0]
    d_model = w_in.shape[0]
    mla_width = g_om.shape[0]
    row = lambda v: v.reshape(1, -1)
    c0, c1, c2 = q_lora, q_lora + kv_lora, q_lora + kv_lora + ROPE_DIM
    with_swap = lambda w: jnp.concatenate([w, w[..., HALF_ROPE:], w[..., :HALF_ROPE]], axis=-1)
    w_in_p = jnp.concatenate([w_in[:, :c1], _pad_lanes(with_swap(w_in[:, c1:c2]), ROPE_LO), w_in[:, c2:]], axis=1)
    w_uq_h = w_uq.reshape(q_lora, MLA_HEADS, NOPE_DIM + ROPE_DIM)
    w_uq_p = jnp.concatenate([w_uq_h[..., :NOPE_DIM], with_swap(w_uq_h[..., NOPE_DIM:])], axis=-1)
    w_uq_p = w_uq_p.reshape(q_lora, -1)
    w_uk_p = _pad_lanes(w_uk, 0).reshape(kv_lora, -1)
    w_uvt = jnp.pad(jnp.transpose(w_uv, (1, 2, 0)), ((0, 0), (0, VT_ROWS - V_DIM), (0, 0)))
    v_one = jnp.zeros((MLA_HEADS, VT_ROWS, 1), _F32).at[:, V_DIM].set(1.0)
    w_ukt = jnp.pad(jnp.transpose(w_uk, (1, 2, 0)), ((0, 0), (0, LANES - NOPE_DIM), (0, 0)))
    w_uvh = jnp.stack([jnp.pad(w_uv[:, h], ((0, 0), (h * V_DIM, (MLA_HEADS - 1 - h) * V_DIM)))
                       for h in range(MLA_HEADS)])
    return {
        "ln_mix": row(ln_mix), "w_in": w_in_p.astype(_BF16), "g_q": row(g_q), "w_uq": w_uq_p.astype(_BF16),
        "g_kv": row(g_kv), "w_uk": w_uk_p.astype(_BF16), "w_uvt": w_uvt.reshape(-1, kv_lora).astype(_BF16),
        "v_one": v_one.reshape(-1, 1), "w_dw": w_dw, "b_dw": row(b_dw), "g_cn": row(g_cn), "b_cn": row(b_cn),
        "g_om": row(g_om), "g_oc": row(g_oc),
        "w_ukt": w_ukt.astype(_BF16), "w_uvh": w_uvh.astype(_BF16),
        "w_out_a": w_out[:mla_width].astype(_BF16), "w_out_c": w_out[mla_width:].astype(_BF16),
        "ln_ffn": row(ln_ffn), "w_gate": w_gate.astype(_BF16), "w_up": w_up.astype(_BF16),
        "w_down": w_down.astype(_BF16), "g_final": row(g_final),
    }


def _layer(x_prompt, x_sample, ckv_past, kpe_past, conv_past, wts):
    batch, seq, d = x_prompt.shape
    dec_batch, dec_seq, _ = x_sample.shape
    past = ckv_past.shape[1]

    xp = x_prompt.reshape(batch * seq, d)
    q, k, vt, kv_p, kr_p_t, u, cv_p = _prompt_proj(xp, _rope_tables(jnp.arange(seq)), wts, seq)
    an = _prompt_attn(q, k, vt, wts["g_om"], batch, seq)
    y_p = _prompt_output(xp, an, u, wts, seq)

    xs = x_sample.reshape(dec_batch * dec_seq, d)
    tabs = _rope_tables(jnp.tile(past + jnp.arange(dec_seq), dec_batch))
    q, kv_s, kr_s, u = _sample_proj(xs, tabs, wts)
    an = _sample_attn(q, kv_s, kr_s, ckv_past, jnp.swapaxes(kpe_past, 1, 2), wts["w_ukt"], wts["w_uvh"],
                      wts["g_om"], dec_batch, dec_seq)
    y_s, cv_s = _sample_output(xs, an, u, conv_past, wts, dec_seq)

    return (y_p.reshape(batch, seq, d), y_s.reshape(dec_batch, dec_seq, d),
            kv_p.reshape(batch, seq, -1), jnp.swapaxes(kr_p_t, 1, 2), cv_p,
            kv_s.reshape(dec_batch, dec_seq, -1),
            kr_s[:, ROPE_LO:ROPE_HI].reshape(dec_batch, dec_seq, ROPE_DIM), cv_s)


def kernel(x_prompt, x_sample, cache_kv_latent, cache_k_rope, state_conv, ln_mix, w_in, g_q, w_uq, g_kv, w_uk, w_uv, w_dw, b_dw, g_cn, b_cn, g_om, g_oc, w_out, ln_ffn, w_gate, w_up, w_down, g_final):
    depth = w_in.shape[0]
    assert depth == 1, "the kernel implements the single-layer model of the problem"
    wts = _prepare_weights(ln_mix[0], w_in[0], g_q[0], w_uq[0], g_kv[0], w_uk[0], w_uv[0], w_dw[0], b_dw[0],
                           g_cn[0], b_cn[0], g_om[0], g_oc[0], w_out[0], ln_ffn[0], w_gate[0], w_up[0],
                           w_down[0], g_final)
    outs = _layer(x_prompt, x_sample, cache_kv_latent[0], cache_k_rope[0], state_conv[0], wts)
    y_p, y_s = outs[0], outs[1]
    return (y_p, y_s) + tuple(o[None] for o in outs[2:])
```

```python
import functools
import math

import jax
import jax.numpy as jnp
from jax import lax
from jax.experimental import pallas as pl
from jax.experimental.pallas import tpu as pltpu

CHUNK = 64
MLA_HEADS = 8
NOPE_DIM = 64
ROPE_DIM = 32
V_DIM = 64
ROPE_THETA = 10000.0
EPS = 1e-6
CONV_W = 31
CONV_STATE = CONV_W - 1
ATTN_SCALE = 1.0 / math.sqrt(NOPE_DIM + ROPE_DIM)
MASK_VALUE = -1e30

LANES = 128
SUBLANES = 8
HALF_ROPE = ROPE_DIM // 2
ROPE_LO = NOPE_DIM
ROPE_HI = ROPE_LO + ROPE_DIM
assert ROPE_HI + ROPE_DIM == LANES
VT_ROWS = 80
HALO_ROWS = 32
HALO_PAD = HALO_ROWS - CONV_STATE
CONV_STRIDE = 4
CONV_BLOCK = SUBLANES * CONV_STRIDE
Q_SCALE = ATTN_SCALE * math.log2(math.e)

ROW_TILE = 512
ATTN_TILE = 512
KEY_PARTS = 2
SCORE_LOOKAHEAD = 4
ATTN_DIAG_TILE = 256
CAST_ROW_ALIGN = 32
VMEM_LIMIT_BYTES = 56 * 1024 * 1024

_BF16 = jnp.bfloat16
_F32 = jnp.float32


def _rms(x, g):
    return x * lax.rsqrt(jnp.mean(x * x, axis=-1, keepdims=True) + EPS) * g


def _dot(a, b):
    return jnp.dot(a, b, preferred_element_type=_F32)


def _dot_nt(a, b):
    return lax.dot_general(a, b, (((1,), (1,)), ((), ())), preferred_element_type=_F32)


def _rope(x, table, rest):
    y = x * table
    lane = lax.broadcasted_iota(jnp.int32, x.shape, 1)
    in_rope = (lane >= ROPE_LO) & (lane < ROPE_HI)
    return jnp.where(in_rope, y + pltpu.roll(y, LANES - ROPE_DIM, 1), y if rest is None else rest)


def _rope_table_kernel(pos_ref, inv_ref, q_tab_ref, k_tab_ref):
    lane = lax.broadcasted_iota(jnp.int32, k_tab_ref.shape, 1)
    ang = pos_ref[...] * inv_ref[...]
    c = jnp.cos(ang)
    s = jnp.sin(ang)
    rot = jnp.where(lane < ROPE_HI, c, jnp.where(lane < ROPE_HI + HALF_ROPE, -s, s))
    k_tab_ref[...] = rot
    q_tab_ref[...] = jnp.where(lane < ROPE_LO, Q_SCALE, rot * Q_SCALE)


def _rope_tables(pos):
    inv = 1.0 / (ROPE_THETA ** (jnp.arange(0, ROPE_DIM, 2, dtype=_F32) / ROPE_DIM))
    inv_lanes = jnp.zeros((1, LANES), _F32).at[0, ROPE_LO:].set(jnp.tile(inv, 4))
    out = jax.ShapeDtypeStruct((pos.shape[0], LANES), _F32)
    return pl.pallas_call(
        _rope_table_kernel,
        out_shape=(out, out),
        name="rope_table",
    )(pos.astype(_F32).reshape(-1, 1), inv_lanes)


def _in_proj(x_ref, ln_mix_ref, w_in_ref, widths):
    hn = _rms(x_ref[...], ln_mix_ref[...]).astype(_BF16)
    outs, start = [], 0
    for w in widths:
        outs.append(_dot(hn, w_in_ref[:, start:start + w]))
        start += w
    return outs


def _query(cq, g_q_ref, w_uq_ref, q_tab, q_ref):
    q = _dot(_rms(cq, g_q_ref[...]).astype(_BF16), w_uq_ref[...])
    for h in range(MLA_HEADS):
        q_ref[h] = _rope(q[:, h * LANES:(h + 1) * LANES], q_tab, None).astype(q_ref.dtype)


def _depthwise_conv(ubuf, obuf, w_dw_ref, b_dw_ref):
    groups, n, _ = obuf.shape

    def group(g):
        lanes = slice(g * LANES, (g + 1) * LANES)
        weights = [jnp.broadcast_to(w_dw_ref[k:k + 1, lanes], (SUBLANES, LANES)) for k in range(CONV_W)]
        bias = jnp.broadcast_to(b_dw_ref[:, lanes], (SUBLANES, LANES))
        for base in range(0, n, CONV_BLOCK):
            taps = [ubuf[g, pl.ds(base + HALO_PAD + s, SUBLANES, stride=CONV_STRIDE), :]
                    for s in range(CONV_STRIDE + CONV_W - 1)]
            for c in range(CONV_STRIDE):
                acc = bias
                for k in range(CONV_W):
                    acc = acc + taps[c + k] * weights[k]
                obuf[g, pl.ds(base + c, SUBLANES, stride=CONV_STRIDE), :] = acc

    return [functools.partial(group, g) for g in range(groups)]


def _split_lanes(x):
    return jnp.stack([x[:, c:c + LANES] for c in range(0, x.shape[1], LANES)])


def _merge_lanes(x):
    return jnp.concatenate(list(x), axis=1)


def _conv_branch(dw, g_cn_ref, b_cn_ref, g_oc_ref):
    mu = jnp.mean(dw, axis=-1, keepdims=True)
    xc = dw - mu
    y = xc * lax.rsqrt(jnp.mean(xc * xc, axis=-1, keepdims=True) + EPS) * g_cn_ref[...] + b_cn_ref[...]
    return _rms(y * jax.nn.sigmoid(y), g_oc_ref[...])


def _prompt_proj_kernel(x_ref, q_tab_ref, k_tab_ref, ln_mix_ref, w_in_ref, g_q_ref, w_uq_ref,
                        g_kv_ref, w_uk_ref, w_uvt_ref, v_one_ref, w_dw_ref, b_dw_ref, g_cn_ref, b_cn_ref,
                        g_oc_ref,
                        q_ref, k_ref, vt_ref, ckv_ref, kpe_t_ref, cn_ref, ncv_ref, ubuf, obuf, *, tiles_per_seq):
    rows = x_ref.shape[0]
    q_lora, kv_lora, conv_ch = g_q_ref.shape[1], g_kv_ref.shape[1], g_cn_ref.shape[1]
    cq, ckv, kpe, a, gate = _in_proj(x_ref, ln_mix_ref, w_in_ref, (q_lora, kv_lora, LANES, conv_ch, conv_ch))
    _query(cq, g_q_ref, w_uq_ref, q_tab_ref[...], q_ref)

    ckv = _rms(ckv, g_kv_ref[...])
    ckv_ref[...] = ckv
    ckv_b = ckv.astype(_BF16)
    kpe = _rope(kpe, k_tab_ref[...], 0.0)
    kpe_t_ref[0] = kpe.T[ROPE_LO:ROPE_HI, :]
    k_nope = _dot(ckv_b, w_uk_ref[...])
    for h in range(MLA_HEADS):
        k_ref[h] = (k_nope[:, h * LANES:(h + 1) * LANES] + kpe).astype(k_ref.dtype)

    vt = _dot_nt(w_uvt_ref[...], ckv_b) + v_one_ref[...]
    key_tile = vt_ref.shape[3]
    for h in range(MLA_HEADS):
        for t in range(rows // key_tile):
            vt_ref[h, t] = vt[h * VT_ROWS:(h + 1) * VT_ROWS, t * key_tile:(t + 1) * key_tile].astype(vt_ref.dtype)

    @pl.when(lax.rem(pl.program_id(0), tiles_per_seq) == 0)
    def _():
        ubuf[:, 0:HALO_ROWS, :] = jnp.zeros((ubuf.shape[0], HALO_ROWS, LANES), _F32)

    u = a * jax.nn.sigmoid(gate)
    ubuf[:, HALO_ROWS:HALO_ROWS + rows, :] = _split_lanes(u)
    ncv_ref[0] = u[rows - CONV_STATE:, :]
    for group in _depthwise_conv(ubuf, obuf, w_dw_ref, b_dw_ref):
        group()
    cn_ref[...] = _conv_branch(_merge_lanes(obuf[...]), g_cn_ref, b_cn_ref, g_oc_ref).astype(cn_ref.dtype)
    ubuf[:, 0:HALO_ROWS, :] = ubuf[:, rows:HALO_ROWS + rows, :]


def _sample_proj_kernel(x_ref, q_tab_ref, k_tab_ref, state_ref, ln_mix_ref, w_in_ref, g_q_ref,
                        w_uq_ref, g_kv_ref, w_dw_ref, b_dw_ref, g_cn_ref, b_cn_ref, g_oc_ref,
                        q_ref, ckv_ref, kpe_ref, cn_ref, ncv_ref, ubuf, obuf):
    segs = state_ref.shape[0]
    seq = x_ref.shape[0] // segs
    stride = HALO_ROWS + seq
    q_lora, kv_lora, conv_ch = g_q_ref.shape[1], g_kv_ref.shape[1], g_cn_ref.shape[1]
    cq, ckv, kpe, a, gate = _in_proj(x_ref, ln_mix_ref, w_in_ref, (q_lora, kv_lora, LANES, conv_ch, conv_ch))
    _query(cq, g_q_ref, w_uq_ref, q_tab_ref[...], q_ref)
    ckv_ref[...] = _rms(ckv, g_kv_ref[...])
    kpe_ref[...] = _rope(kpe, k_tab_ref[...], 0.0)

    u = a * jax.nn.sigmoid(gate)
    ubuf[...] = jnp.zeros(ubuf.shape, _F32)
    for s in range(segs):
        ubuf[:, s * stride + HALO_PAD:s * stride + HALO_ROWS, :] = _split_lanes(state_ref[s])
        ubuf[:, s * stride + HALO_ROWS:(s + 1) * stride, :] = _split_lanes(u[s * seq:(s + 1) * seq, :])
    for group in _depthwise_conv(ubuf, obuf, w_dw_ref, b_dw_ref):
        group()
    dw = jnp.concatenate([_merge_lanes(obuf[:, s * stride:s * stride + seq, :]) for s in range(segs)], axis=0)
    cn_ref[...] = _conv_branch(dw, g_cn_ref, b_cn_ref, g_oc_ref).astype(cn_ref.dtype)
    for s in range(segs):
        ncv_ref[s] = jnp.concatenate([state_ref[s][seq:, :], u[s * seq:(s + 1) * seq, :]], axis=0)


def _whole(shape):
    zeros = (0,) * len(shape)
    return pl.BlockSpec(shape, lambda *_: zeros)


def _prompt_proj(x, tabs, wts, seq):
    n, d = x.shape
    tm = min(ROW_TILE, seq)
    key_tile = min(ATTN_TILE, seq)
    assert tm % key_tile == 0 and seq % tm == 0 and tm % CONV_BLOCK == 0
    tiles_per_seq = seq // tm
    conv_ch = wts["g_cn"].shape[1]
    kv_lora = wts["g_kv"].shape[1]
    row_block = lambda w: pl.BlockSpec((tm, w), lambda i: (i, 0))
    head_block = pl.BlockSpec((MLA_HEADS, tm, LANES), lambda i: (0, i, 0))
    vt_block = pl.BlockSpec((MLA_HEADS, tm // key_tile, VT_ROWS, key_tile), lambda i: (0, i, 0, 0))
    tab_block = pl.BlockSpec((tm, LANES), lambda i: (lax.rem(i, tiles_per_seq), 0))
    names = ("ln_mix", "w_in", "g_q", "w_uq", "g_kv", "w_uk", "w_uvt", "v_one", "w_dw", "b_dw", "g_cn",
             "b_cn", "g_oc")
    head_shape = jax.ShapeDtypeStruct((MLA_HEADS, n, LANES), _BF16)
    return pl.pallas_call(
        functools.partial(_prompt_proj_kernel, tiles_per_seq=tiles_per_seq),
        grid=(n // tm,),
        in_specs=[row_block(d), tab_block, tab_block] + [_whole(wts[k].shape) for k in names],
        out_specs=(head_block, head_block, vt_block, row_block(kv_lora),
                   pl.BlockSpec((1, ROPE_DIM, tm), lambda i: (i // tiles_per_seq, 0, lax.rem(i, tiles_per_seq))),
                   row_block(conv_ch), pl.BlockSpec((1, CONV_STATE, conv_ch), lambda i: (i // tiles_per_seq, 0, 0))),
        out_shape=(head_shape, head_shape,
                   jax.ShapeDtypeStruct((MLA_HEADS, n // key_tile, VT_ROWS, key_tile), _BF16),
                   jax.ShapeDtypeStruct((n, kv_lora), _F32), jax.ShapeDtypeStruct((n // seq, ROPE_DIM, seq), _F32),
                   jax.ShapeDtypeStruct((n, conv_ch), _BF16),
                   jax.ShapeDtypeStruct((n // seq, CONV_STATE, conv_ch), _F32)),
        scratch_shapes=[pltpu.VMEM((conv_ch // LANES, HALO_ROWS + tm, LANES), _F32),
                        pltpu.VMEM((conv_ch // LANES, tm, LANES), _F32)],
        compiler_params=pltpu.CompilerParams(dimension_semantics=("arbitrary",),
                                             vmem_limit_bytes=VMEM_LIMIT_BYTES),
        name="prompt_proj",
    )(x, *tabs, *[wts[k] for k in names])


def _sample_proj(x, tabs, state, wts, seq):
    n, d = x.shape
    conv_ch = wts["g_cn"].shape[1]
    kv_lora = wts["g_kv"].shape[1]
    names = ("ln_mix", "w_in", "g_q", "w_uq", "g_kv", "w_dw", "b_dw", "g_cn", "b_cn", "g_oc")
    conv_rows = (n // seq) * (HALO_ROWS + seq)
    assert conv_rows % CONV_BLOCK == 0
    return pl.pallas_call(
        _sample_proj_kernel,
        out_shape=(jax.ShapeDtypeStruct((MLA_HEADS, n, LANES), _BF16),
                   jax.ShapeDtypeStruct((n, kv_lora), _F32), jax.ShapeDtypeStruct((n, LANES), _F32),
                   jax.ShapeDtypeStruct((n, conv_ch), _BF16),
                   jax.ShapeDtypeStruct((n // seq, CONV_STATE, conv_ch), _F32)),
        scratch_shapes=[pltpu.VMEM((conv_ch // LANES, conv_rows + HALO_ROWS, LANES), _F32),
                        pltpu.VMEM((conv_ch // LANES, conv_rows, LANES), _F32)],
        compiler_params=pltpu.CompilerParams(vmem_limit_bytes=VMEM_LIMIT_BYTES),
        name="sample_proj",
    )(x, *tabs, state, *[wts[k] for k in names])


def _software_pipeline(items, issue, finish):
    items = list(items)
    issued, done = [], []
    for t in range(len(items) + SCORE_LOOKAHEAD):
        if t < len(items):
            issued.append(issue(items[t]))
        if t >= SCORE_LOOKAHEAD:
            done.append(finish(items[t - SCORE_LOOKAHEAD], issued[t - SCORE_LOOKAHEAD]))
    return done


def _prompt_attn_kernel(q_ref, k_ref, vt_ref, g_om_ref, *refs, sub):
    n_cast = (len(refs) - 1) // 2
    o_ref = refs[n_cast]
    for src_ref, dst_ref in zip(refs[:n_cast], refs[n_cast + 1:]):
        dst_ref[...] = src_ref[...].astype(dst_ref.dtype)
    tile = q_ref.shape[1]
    i = pl.program_id(1)
    chunk_of = lambda t: lax.shift_right_logical(t, CHUNK.bit_length() - 1)

    def key_rows(j):
        return pl.ds(pl.multiple_of(j * tile, tile), tile)

    spans = []
    for r in range(tile // sub):
        n_keys = (r + 1) * sub
        key = lax.broadcasted_iota(jnp.int32, (n_keys, sub), 0)
        query = lax.broadcasted_iota(jnp.int32, (n_keys, sub), 1) + r * sub
        spans.append((slice(r * sub, (r + 1) * sub), n_keys, chunk_of(key) <= chunk_of(query)))

    def diag_scores(h, span):
        q_rows, n_keys, mask = span
        keys = pl.ds(pl.multiple_of(i * tile, tile), n_keys)
        return jnp.where(mask, _dot_nt(k_ref[h, keys, :], q_ref[h, q_rows, :]), MASK_VALUE)

    def diag_finish(h, span, s):
        m = jnp.max(s, axis=0, keepdims=True)
        return m, _dot(vt_ref[h, i, :, 0:span[1]], jnp.exp2(s - m).astype(_BF16))

    items = [(h, span) for h in range(MLA_HEADS) for span in spans]
    done = _software_pipeline(items, lambda it: diag_scores(*it), lambda it, s: diag_finish(*it, s))
    per_head = len(spans)
    maxes = [jnp.concatenate([m for m, _ in done[h * per_head:(h + 1) * per_head]], axis=1)
             for h in range(MLA_HEADS)]
    accs = [jnp.concatenate([a for _, a in done[h * per_head:(h + 1) * per_head]], axis=1)
            for h in range(MLA_HEADS)]

    def body(j, carry):
        state = [list(carry[0]), list(carry[1])]
        part = tile // KEY_PARTS

        def issue(item):
            h, t = item
            rows = pl.ds(pl.multiple_of(j * tile + t * part, part), part)
            return _dot_nt(k_ref[h, rows, :], q_ref[h])

        def finish(item, s):
            h, t = item
            m_old, acc = state[0][h], state[1][h]
            m_new = jnp.maximum(m_old, jnp.max(s, axis=0, keepdims=True))
            p = jnp.exp2(s - m_new).astype(_BF16)
            state[0][h] = m_new
            state[1][h] = acc * jnp.exp2(m_old - m_new) + _dot(vt_ref[h, j, :, t * part:(t + 1) * part], p)

        _software_pipeline([(h, t) for h in range(MLA_HEADS) for t in range(KEY_PARTS)], issue, finish)
        return tuple(state[0]), tuple(state[1])

    _, accs = lax.fori_loop(0, i, body, (tuple(maxes), tuple(accs)))

    o_t = jnp.concatenate([acc[0:V_DIM] / acc[V_DIM:V_DIM + 1] for acc in accs], axis=0)
    o_ref[...] = _rms(o_t.T, g_om_ref[...]).astype(o_ref.dtype)


def _cast_block_rows(rows, steps):
    block = -(-rows // (steps * CAST_ROW_ALIGN)) * CAST_ROW_ALIGN
    while rows % block:
        block += CAST_ROW_ALIGN
    return block


def _prompt_attn(q, k, vt, g_om, to_cast, batch, seq):
    tile = vt.shape[3]
    nq = seq // tile
    width = g_om.shape[1]
    cast_specs = []
    for w in to_cast:
        block = _cast_block_rows(w.shape[0], batch * nq)
        cast_specs.append(pl.BlockSpec(
            (block, w.shape[1]),
            lambda b, i, last=w.shape[0] // block - 1: (jnp.minimum(b * nq + i, last), 0)))
    outs = pl.pallas_call(
        functools.partial(_prompt_attn_kernel, sub=min(ATTN_DIAG_TILE, tile)),
        grid=(batch, nq),
        in_specs=[pl.BlockSpec((MLA_HEADS, tile, LANES), lambda b, i: (0, b * nq + i, 0)),
                  pl.BlockSpec((MLA_HEADS, seq, LANES), lambda b, i: (0, b, 0)),
                  pl.BlockSpec((MLA_HEADS, nq, VT_ROWS, tile), lambda b, i: (0, b, 0, 0)),
                  _whole(g_om.shape)] + cast_specs,
        out_specs=[pl.BlockSpec((tile, width), lambda b, i: (b * nq + i, 0))] + cast_specs,
        out_shape=[jax.ShapeDtypeStruct((batch * seq, width), _BF16)]
                  + [jax.ShapeDtypeStruct(w.shape, _BF16) for w in to_cast],
        compiler_params=pltpu.CompilerParams(dimension_semantics=("arbitrary", "arbitrary"),
                                             vmem_limit_bytes=VMEM_LIMIT_BYTES),
        name="prompt_attn",
    )(q, k, vt, g_om, *to_cast)
    return outs[0], outs[1:]


def _sample_attn_kernel(q_ref, ckv_new_ref, kpe_new_ref, ckv_past_ref, kpe_past_t_ref, w_ukt_ref, w_uvh_ref,
                        g_om_ref, o_ref):
    seq = q_ref.shape[1]
    q_all = jnp.concatenate([q_ref[h] for h in range(MLA_HEADS)], axis=0)
    q_lat = jnp.concatenate([_dot(q_ref[h], w_ukt_ref[h]) for h in range(MLA_HEADS)], axis=0).astype(_BF16)
    q_pe = q_all[:, ROPE_LO:ROPE_HI]
    c_past = ckv_past_ref[0].astype(_BF16)
    c_new = ckv_new_ref[...].astype(_BF16)
    s_past = _dot_nt(q_lat, c_past) + _dot(q_pe, kpe_past_t_ref[0].astype(_BF16))
    s_new = _dot_nt(q_lat, c_new) + _dot_nt(q_all, kpe_new_ref[...].astype(_BF16))
    m = jnp.maximum(jnp.max(s_past, axis=-1, keepdims=True), jnp.max(s_new, axis=-1, keepdims=True))
    p_past = jnp.exp2(s_past - m)
    p_new = jnp.exp2(s_new - m)
    denom = jnp.sum(p_past, axis=-1, keepdims=True) + jnp.sum(p_new, axis=-1, keepdims=True)
    o_lat = ((_dot(p_past.astype(_BF16), c_past) + _dot(p_new.astype(_BF16), c_new)) / denom).astype(_BF16)
    o = sum(_dot(o_lat[h * seq:(h + 1) * seq], w_uvh_ref[h]) for h in range(MLA_HEADS))
    o_ref[...] = _rms(o, g_om_ref[...]).astype(o_ref.dtype)


def _sample_attn(q, ckv_new, kpe_new, ckv_past, kpe_past_t, w_ukt, w_uvh, g_om, batch, seq):
    past, kv_lora = ckv_past.shape[1:]
    width = g_om.shape[1]
    return pl.pallas_call(
        _sample_attn_kernel,
        grid=(batch,),
        in_specs=[pl.BlockSpec((MLA_HEADS, seq, LANES), lambda b: (0, b, 0)),
                  pl.BlockSpec((seq, kv_lora), lambda b: (b, 0)),
                  pl.BlockSpec((seq, LANES), lambda b: (b, 0)),
                  pl.BlockSpec((1, past, kv_lora), lambda b: (b, 0, 0)),
                  pl.BlockSpec((1, ROPE_DIM, past), lambda b: (b, 0, 0)),
                  _whole(w_ukt.shape), _whole(w_uvh.shape), _whole(g_om.shape)],
        out_specs=pl.BlockSpec((seq, width), lambda b: (b, 0)),
        out_shape=jax.ShapeDtypeStruct((batch * seq, width), _BF16),
        compiler_params=pltpu.CompilerParams(dimension_semantics=("arbitrary",),
                                             vmem_limit_bytes=VMEM_LIMIT_BYTES),
        name="sample_attn",
    )(q, ckv_new, kpe_new, ckv_past, kpe_past_t, w_ukt, w_uvh, g_om)


def _output_kernel(x_ref, an_ref, cn_ref, w_out_a_ref, w_out_c_ref, ln_ffn_ref, w_gate_ref, w_up_ref,
                   w_down_ref, g_final_ref, y_ref):
    h = x_ref[...] + _dot(an_ref[...], w_out_a_ref[...]) + _dot(cn_ref[...], w_out_c_ref[...])
    f = _rms(h, ln_ffn_ref[...]).astype(_BF16)
    gate = _dot(f, w_gate_ref[...])
    act = (gate * jax.nn.sigmoid(gate) * _dot(f, w_up_ref[...])).astype(_BF16)
    h = h + _dot(act, w_down_ref[...])
    y_ref[...] = _rms(h, g_final_ref[...])


def _output(x, an, cn, wts, name):
    n, d = x.shape
    tm = min(ROW_TILE, n)
    names = ("w_out_a", "w_out_c", "ln_ffn", "w_gate", "w_up", "w_down", "g_final")
    row_block = lambda w: pl.BlockSpec((tm, w), lambda i: (i, 0))
    resident = lambda shape: pl.BlockSpec(shape, lambda i: (0,) * len(shape), pipeline_mode=pl.Buffered(1))
    return pl.pallas_call(
        _output_kernel,
        grid=(n // tm,),
        in_specs=[row_block(d), row_block(an.shape[1]), row_block(cn.shape[1])]
                 + [resident(wts[k].shape) for k in names],
        out_specs=row_block(d),
        out_shape=jax.ShapeDtypeStruct((n, d), _F32),
        compiler_params=pltpu.CompilerParams(dimension_semantics=("arbitrary",),
                                             vmem_limit_bytes=VMEM_LIMIT_BYTES),
        name=name,
    )(x, an, cn, *[wts[k] for k in names])


def _pad_lanes(w, left):
    return jnp.pad(w, [(0, 0)] * (w.ndim - 1) + [(left, LANES - left - w.shape[-1])])


def _prepare_weights(ln_mix, w_in, g_q, w_uq, g_kv, w_uk, w_uv, w_dw, b_dw, g_cn, b_cn, g_om, g_oc, w_out,
                     ln_ffn, g_final):
    q_lora, kv_lora = g_q.shape[0], g_kv.shape[0]
    mla_width = g_om.shape[0]
    row = lambda v: v.reshape(1, -1)
    c1, c2 = q_lora + kv_lora, q_lora + kv_lora + ROPE_DIM
    with_swap = lambda w: jnp.concatenate([w, w[..., HALF_ROPE:], w[..., :HALF_ROPE]], axis=-1)
    w_in_p = jnp.concatenate([w_in[:, :c1], _pad_lanes(with_swap(w_in[:, c1:c2]), ROPE_LO), w_in[:, c2:]], axis=1)
    w_uq_h = w_uq.reshape(q_lora, MLA_HEADS, NOPE_DIM + ROPE_DIM)
    w_uq_p = jnp.concatenate([w_uq_h[..., :NOPE_DIM], with_swap(w_uq_h[..., NOPE_DIM:])], axis=-1)
    w_uq_p = w_uq_p.reshape(q_lora, -1)
    w_uk_p = _pad_lanes(w_uk, 0).reshape(kv_lora, -1)
    w_uvt = jnp.pad(jnp.transpose(w_uv, (1, 2, 0)), ((0, 0), (0, VT_ROWS - V_DIM), (0, 0)))
    v_one = jnp.zeros((MLA_HEADS, VT_ROWS, 1), _F32).at[:, V_DIM].set(1.0)
    w_ukt = jnp.pad(jnp.transpose(w_uk, (1, 2, 0)), ((0, 0), (0, LANES - NOPE_DIM), (0, 0)))
    w_uvh = jnp.stack([jnp.pad(w_uv[:, h], ((0, 0), (h * V_DIM, (MLA_HEADS - 1 - h) * V_DIM)))
                       for h in range(MLA_HEADS)])
    return {
        "ln_mix": row(ln_mix), "w_in": w_in_p.astype(_BF16), "g_q": row(g_q), "w_uq": w_uq_p.astype(_BF16),
        "g_kv": row(g_kv), "w_uk": w_uk_p.astype(_BF16), "w_uvt": w_uvt.reshape(-1, kv_lora).astype(_BF16),
        "v_one": v_one.reshape(-1, 1), "w_dw": w_dw, "b_dw": row(b_dw), "g_cn": row(g_cn), "b_cn": row(b_cn),
        "g_om": row(g_om), "g_oc": row(g_oc),
        "w_ukt": w_ukt.astype(_BF16), "w_uvh": w_uvh.astype(_BF16),
        "w_out_a": w_out[:mla_width].astype(_BF16), "w_out_c": w_out[mla_width:].astype(_BF16),
        "ln_ffn": row(ln_ffn), "g_final": row(g_final),
    }


def _layer(x_prompt, x_sample, ckv_past, kpe_past, conv_past, wts, ffn_f32):
    batch, seq, d = x_prompt.shape
    dec_batch, dec_seq, _ = x_sample.shape
    past = ckv_past.shape[1]

    xp = x_prompt.reshape(batch * seq, d)
    q, k, vt, kv_p, kr_p_t, cn, cv_p = _prompt_proj(xp, _rope_tables(jnp.arange(seq)), wts, seq)
    an, ffn_bf16 = _prompt_attn(q, k, vt, wts["g_om"], list(ffn_f32.values()), batch, seq)
    wts = dict(wts, **dict(zip(ffn_f32, ffn_bf16)))
    y_p = _output(xp, an, cn, wts, "prompt_output")

    xs = x_sample.reshape(dec_batch * dec_seq, d)
    tabs = _rope_tables(jnp.tile(past + jnp.arange(dec_seq), dec_batch))
    q, kv_s, kr_s, cn, cv_s = _sample_proj(xs, tabs, conv_past, wts, dec_seq)
    an = _sample_attn(q, kv_s, kr_s, ckv_past, jnp.swapaxes(kpe_past, 1, 2), wts["w_ukt"], wts["w_uvh"],
                      wts["g_om"], dec_batch, dec_seq)
    y_s = _output(xs, an, cn, wts, "sample_output")

    return (y_p.reshape(batch, seq, d), y_s.reshape(dec_batch, dec_seq, d),
            kv_p.reshape(batch, seq, -1), jnp.swapaxes(kr_p_t, 1, 2), cv_p,
            kv_s.reshape(dec_batch, dec_seq, -1),
            kr_s[:, ROPE_LO:ROPE_HI].reshape(dec_batch, dec_seq, ROPE_DIM), cv_s)


def kernel(x_prompt, x_sample, cache_kv_latent, cache_k_rope, state_conv, ln_mix, w_in, g_q, w_uq, g_kv, w_uk, w_uv, w_dw, b_dw, g_cn, b_cn, g_om, g_oc, w_out, ln_ffn, w_gate, w_up, w_down, g_final):
    depth = w_in.shape[0]
    assert depth == 1, "the kernel implements the single-layer model of the problem"
    wts = _prepare_weights(ln_mix[0], w_in[0], g_q[0], w_uq[0], g_kv[0], w_uk[0], w_uv[0], w_dw[0], b_dw[0],
                           g_cn[0], b_cn[0], g_om[0], g_oc[0], w_out[0], ln_ffn[0], g_final)
    ffn_f32 = {"w_gate": w_gate[0], "w_up": w_up[0], "w_down": w_down[0]}
    outs = _layer(x_prompt, x_sample, cache_kv_latent[0], cache_k_rope[0], state_conv[0], wts, ffn_f32)
    y_p, y_s = outs[0], outs[1]
    return (y_p, y_s) + tuple(o[None] for o in outs[2:])
```

```python
import functools
import math

import jax
import jax.numpy as jnp
from jax import lax
from jax.experimental import pallas as pl
from jax.experimental.pallas import tpu as pltpu

CHUNK = 64
MLA_HEADS = 8
NOPE_DIM = 64
ROPE_DIM = 32
V_DIM = 64
ROPE_THETA = 10000.0
EPS = 1e-6
CONV_W = 31
CONV_STATE = CONV_W - 1
ATTN_SCALE = 1.0 / math.sqrt(NOPE_DIM + ROPE_DIM)
MASK_VALUE = -1e30

LANES = 128
SUBLANES = 8
HALF_ROPE = ROPE_DIM // 2
ROPE_LO = NOPE_DIM
ROPE_HI = ROPE_LO + ROPE_DIM
assert ROPE_HI + ROPE_DIM == LANES
VT_ROWS = 80
HALO_ROWS = 32
HALO_PAD = HALO_ROWS - CONV_STATE
CONV_STRIDE = 4
CONV_BLOCK = SUBLANES * CONV_STRIDE
Q_SCALE = ATTN_SCALE * math.log2(math.e)

HEADS_PER_PIECE = 2
ROW_TILE = 512
ATTN_TILE = 512
KEY_PARTS = 2
SCORE_LOOKAHEAD = 6
ATTN_DIAG_TILE = 256
CAST_ROW_ALIGN = 32
VMEM_LIMIT_BYTES = 56 * 1024 * 1024

_BF16 = jnp.bfloat16
_F32 = jnp.float32


def _rms(x, g):
    return x * lax.rsqrt(jnp.mean(x * x, axis=-1, keepdims=True) + EPS) * g


def _dot(a, b):
    return jnp.dot(a, b, preferred_element_type=_F32)


def _dot_nt(a, b):
    return lax.dot_general(a, b, (((1,), (1,)), ((), ())), preferred_element_type=_F32)


def _rope(x, table, rest):
    y = x * table
    lane = lax.broadcasted_iota(jnp.int32, x.shape, 1)
    in_rope = (lane >= ROPE_LO) & (lane < ROPE_HI)
    return jnp.where(in_rope, y + pltpu.roll(y, LANES - ROPE_DIM, 1), y if rest is None else rest)


def _rope_table_kernel(pos_ref, inv_ref, q_tab_ref, k_tab_ref):
    lane = lax.broadcasted_iota(jnp.int32, k_tab_ref.shape, 1)
    ang = pos_ref[...] * inv_ref[...]
    c = jnp.cos(ang)
    s = jnp.sin(ang)
    rot = jnp.where(lane < ROPE_HI, c, jnp.where(lane < ROPE_HI + HALF_ROPE, -s, s))
    k_tab_ref[...] = rot
    q_tab_ref[...] = jnp.where(lane < ROPE_LO, Q_SCALE, rot * Q_SCALE)


def _rope_tables(pos):
    inv = 1.0 / (ROPE_THETA ** (jnp.arange(0, ROPE_DIM, 2, dtype=_F32) / ROPE_DIM))
    inv_lanes = jnp.zeros((1, LANES), _F32).at[0, ROPE_LO:].set(jnp.tile(inv, 4))
    out = jax.ShapeDtypeStruct((pos.shape[0], LANES), _F32)
    return pl.pallas_call(
        _rope_table_kernel,
        out_shape=(out, out),
        name="rope_table",
    )(pos.astype(_F32).reshape(-1, 1), inv_lanes)


def _in_proj(x_ref, ln_mix_ref, w_in_ref, widths):
    hn = _rms(x_ref[...], ln_mix_ref[...]).astype(_BF16)
    outs, start = [], 0
    for w in widths:
        outs.append(_dot(hn, w_in_ref[:, start:start + w]))
        start += w
    return outs


def _query(cq, g_q_ref, w_uq_ref, q_tab, q_ref):
    q = _dot(_rms(cq, g_q_ref[...]).astype(_BF16), w_uq_ref[...])
    for h in range(MLA_HEADS):
        q_ref[h] = _rope(q[:, h * LANES:(h + 1) * LANES], q_tab, None).astype(q_ref.dtype)


def _depthwise_conv(ubuf, obuf, w_dw_ref, b_dw_ref):
    groups, n, _ = obuf.shape

    def block(g, base):
        lanes = slice(g * LANES, (g + 1) * LANES)
        taps = [ubuf[g, pl.ds(base + HALO_PAD + s, SUBLANES, stride=CONV_STRIDE), :]
                for s in range(CONV_STRIDE + CONV_W - 1)]
        for c in range(CONV_STRIDE):
            acc = jnp.broadcast_to(b_dw_ref[:, lanes], (SUBLANES, LANES))
            for k in range(CONV_W):
                acc = acc + taps[c + k] * w_dw_ref[k:k + 1, lanes]
            obuf[g, pl.ds(base + c, SUBLANES, stride=CONV_STRIDE), :] = acc

    return [functools.partial(block, g, base) for g in range(groups) for base in range(0, n, CONV_BLOCK)]


def _interleave(stages, fillers):
    fillers = list(fillers)
    share = -(-len(fillers) // len(stages))
    for stage in stages:
        stage()
        for filler in fillers[:share]:
            filler()
        fillers = fillers[share:]


def _split_lanes(x):
    return jnp.stack([x[:, c:c + LANES] for c in range(0, x.shape[1], LANES)])


def _merge_lanes(x):
    return jnp.concatenate(list(x), axis=1)


def _conv_branch(dw, g_cn_ref, b_cn_ref, g_oc_ref):
    mu = jnp.mean(dw, axis=-1, keepdims=True)
    xc = dw - mu
    y = xc * lax.rsqrt(jnp.mean(xc * xc, axis=-1, keepdims=True) + EPS) * g_cn_ref[...] + b_cn_ref[...]
    return _rms(y * jax.nn.sigmoid(y), g_oc_ref[...])


def _prompt_proj_kernel(x_ref, q_tab_ref, k_tab_ref, ln_mix_ref, w_in_ref, g_q_ref, w_uq_ref,
                        g_kv_ref, w_uk_ref, w_uvt_ref, v_one_ref, w_dw_ref, b_dw_ref, g_cn_ref, b_cn_ref,
                        g_oc_ref,
                        q_ref, k_ref, vt_ref, ckv_ref, kpe_t_ref, cn_ref, ncv_ref, ubuf, obuf, *, tiles_per_seq):
    rows = x_ref.shape[0]
    q_lora, kv_lora, conv_ch = g_q_ref.shape[1], g_kv_ref.shape[1], g_cn_ref.shape[1]
    hn = _rms(x_ref[...], ln_mix_ref[...]).astype(_BF16)
    w_cq, w_ckv, w_kpe = 0, q_lora, q_lora + kv_lora
    w_a, w_gate = w_kpe + LANES, w_kpe + LANES + conv_ch

    @pl.when(lax.rem(pl.program_id(0), tiles_per_seq) == 0)
    def _():
        ubuf[:, 0:HALO_ROWS, :] = jnp.zeros((ubuf.shape[0], HALO_ROWS, LANES), _F32)

    u = _dot(hn, w_in_ref[:, w_a:w_a + conv_ch]) * jax.nn.sigmoid(_dot(hn, w_in_ref[:, w_gate:w_gate + conv_ch]))
    ubuf[:, HALO_ROWS:HALO_ROWS + rows, :] = _split_lanes(u)
    ncv_ref[0] = u[rows - CONV_STATE:, :]

    state = {}

    def query_latent():
        cq = _dot(hn, w_in_ref[:, w_cq:w_cq + q_lora])
        state["cq"] = _rms(cq, g_q_ref[...]).astype(_BF16)

    def key_latent():
        ckv = _rms(_dot(hn, w_in_ref[:, w_ckv:w_ckv + kv_lora]), g_kv_ref[...])
        ckv_ref[...] = ckv
        state["ckv"] = ckv.astype(_BF16)
        kpe = _rope(_dot(hn, w_in_ref[:, w_kpe:w_kpe + LANES]), k_tab_ref[...], 0.0)
        kpe_t_ref[0] = kpe.T[ROPE_LO:ROPE_HI, :]
        state["kpe"] = kpe

    def query_heads(h0):
        q = _dot(state["cq"], w_uq_ref[:, h0 * LANES:(h0 + HEADS_PER_PIECE) * LANES])
        for j in range(HEADS_PER_PIECE):
            q_ref[h0 + j] = _rope(q[:, j * LANES:(j + 1) * LANES], q_tab_ref[...], None).astype(q_ref.dtype)

    def key_heads(h0):
        k_nope = _dot(state["ckv"], w_uk_ref[:, h0 * LANES:(h0 + HEADS_PER_PIECE) * LANES])
        for j in range(HEADS_PER_PIECE):
            k_ref[h0 + j] = (k_nope[:, j * LANES:(j + 1) * LANES] + state["kpe"]).astype(k_ref.dtype)

    def values():
        vt = _dot_nt(w_uvt_ref[...], state["ckv"]) + v_one_ref[...]
        key_tile = vt_ref.shape[3]
        for h in range(MLA_HEADS):
            for t in range(rows // key_tile):
                vt_ref[h, t] = vt[h * VT_ROWS:(h + 1) * VT_ROWS,
                                  t * key_tile:(t + 1) * key_tile].astype(vt_ref.dtype)

    stages = [query_latent, key_latent]
    for h0 in range(0, MLA_HEADS, HEADS_PER_PIECE):
        stages += [functools.partial(query_heads, h0), functools.partial(key_heads, h0)]
    stages.append(values)
    _interleave(stages, _depthwise_conv(ubuf, obuf, w_dw_ref, b_dw_ref))

    cn_ref[...] = _conv_branch(_merge_lanes(obuf[...]), g_cn_ref, b_cn_ref, g_oc_ref).astype(cn_ref.dtype)
    ubuf[:, 0:HALO_ROWS, :] = ubuf[:, rows:HALO_ROWS + rows, :]


def _sample_proj_kernel(x_ref, q_tab_ref, k_tab_ref, state_ref, ln_mix_ref, w_in_ref, g_q_ref,
                        w_uq_ref, g_kv_ref, w_dw_ref, b_dw_ref, g_cn_ref, b_cn_ref, g_oc_ref,
                        q_ref, ckv_ref, kpe_ref, cn_ref, ncv_ref, ubuf, obuf):
    segs = state_ref.shape[0]
    seq = x_ref.shape[0] // segs
    stride = HALO_ROWS + seq
    q_lora, kv_lora, conv_ch = g_q_ref.shape[1], g_kv_ref.shape[1], g_cn_ref.shape[1]
    cq, ckv, kpe, a, gate = _in_proj(x_ref, ln_mix_ref, w_in_ref, (q_lora, kv_lora, LANES, conv_ch, conv_ch))
    _query(cq, g_q_ref, w_uq_ref, q_tab_ref[...], q_ref)
    ckv_ref[...] = _rms(ckv, g_kv_ref[...])
    kpe_ref[...] = _rope(kpe, k_tab_ref[...], 0.0)

    u = a * jax.nn.sigmoid(gate)
    ubuf[...] = jnp.zeros(ubuf.shape, _F32)
    for s in range(segs):
        ubuf[:, s * stride + HALO_PAD:s * stride + HALO_ROWS, :] = _split_lanes(state_ref[s])
        ubuf[:, s * stride + HALO_ROWS:(s + 1) * stride, :] = _split_lanes(u[s * seq:(s + 1) * seq, :])
    for group in _depthwise_conv(ubuf, obuf, w_dw_ref, b_dw_ref):
        group()
    dw = jnp.concatenate([_merge_lanes(obuf[:, s * stride:s * stride + seq, :]) for s in range(segs)], axis=0)
    cn_ref[...] = _conv_branch(dw, g_cn_ref, b_cn_ref, g_oc_ref).astype(cn_ref.dtype)
    for s in range(segs):
        ncv_ref[s] = jnp.concatenate([state_ref[s][seq:, :], u[s * seq:(s + 1) * seq, :]], axis=0)


def _whole(shape):
    zeros = (0,) * len(shape)
    return pl.BlockSpec(shape, lambda *_: zeros)


def _prompt_proj(x, tabs, wts, seq):
    n, d = x.shape
    tm = min(ROW_TILE, seq)
    key_tile = min(ATTN_TILE, seq)
    assert tm % key_tile == 0 and seq % tm == 0 and tm % CONV_BLOCK == 0
    tiles_per_seq = seq // tm
    conv_ch = wts["g_cn"].shape[1]
    kv_lora = wts["g_kv"].shape[1]
    row_block = lambda w: pl.BlockSpec((tm, w), lambda i: (i, 0))
    head_block = pl.BlockSpec((MLA_HEADS, tm, LANES), lambda i: (0, i, 0))
    vt_block = pl.BlockSpec((MLA_HEADS, tm // key_tile, VT_ROWS, key_tile), lambda i: (0, i, 0, 0))
    tab_block = pl.BlockSpec((tm, LANES), lambda i: (lax.rem(i, tiles_per_seq), 0))
    names = ("ln_mix", "w_in", "g_q", "w_uq", "g_kv", "w_uk", "w_uvt", "v_one", "w_dw", "b_dw", "g_cn",
             "b_cn", "g_oc")
    head_shape = jax.ShapeDtypeStruct((MLA_HEADS, n, LANES), _BF16)
    return pl.pallas_call(
        functools.partial(_prompt_proj_kernel, tiles_per_seq=tiles_per_seq),
        grid=(n // tm,),
        in_specs=[row_block(d), tab_block, tab_block] + [_whole(wts[k].shape) for k in names],
        out_specs=(head_block, head_block, vt_block, row_block(kv_lora),
                   pl.BlockSpec((1, ROPE_DIM, tm), lambda i: (i // tiles_per_seq, 0, lax.rem(i, tiles_per_seq))),
                   row_block(conv_ch), pl.BlockSpec((1, CONV_STATE, conv_ch), lambda i: (i // tiles_per_seq, 0, 0))),
        out_shape=(head_shape, head_shape,
                   jax.ShapeDtypeStruct((MLA_HEADS, n // key_tile, VT_ROWS, key_tile), _BF16),
                   jax.ShapeDtypeStruct((n, kv_lora), _F32), jax.ShapeDtypeStruct((n // seq, ROPE_DIM, seq), _F32),
                   jax.ShapeDtypeStruct((n, conv_ch), _BF16),
                   jax.ShapeDtypeStruct((n // seq, CONV_STATE, conv_ch), _F32)),
        scratch_shapes=[pltpu.VMEM((conv_ch // LANES, HALO_ROWS + tm, LANES), _F32),
                        pltpu.VMEM((conv_ch // LANES, tm, LANES), _F32)],
        compiler_params=pltpu.CompilerParams(dimension_semantics=("arbitrary",),
                                             vmem_limit_bytes=VMEM_LIMIT_BYTES),
        name="prompt_proj",
    )(x, *tabs, *[wts[k] for k in names])


def _sample_proj(x, tabs, state, wts, seq):
    n, d = x.shape
    conv_ch = wts["g_cn"].shape[1]
    kv_lora = wts["g_kv"].shape[1]
    names = ("ln_mix", "w_in", "g_q", "w_uq", "g_kv", "w_dw", "b_dw", "g_cn", "b_cn", "g_oc")
    conv_rows = (n // seq) * (HALO_ROWS + seq)
    assert conv_rows % CONV_BLOCK == 0
    return pl.pallas_call(
        _sample_proj_kernel,
        out_shape=(jax.ShapeDtypeStruct((MLA_HEADS, n, LANES), _BF16),
                   jax.ShapeDtypeStruct((n, kv_lora), _F32), jax.ShapeDtypeStruct((n, LANES), _F32),
                   jax.ShapeDtypeStruct((n, conv_ch), _BF16),
                   jax.ShapeDtypeStruct((n // seq, CONV_STATE, conv_ch), _F32)),
        scratch_shapes=[pltpu.VMEM((conv_ch // LANES, conv_rows + HALO_ROWS, LANES), _F32),
                        pltpu.VMEM((conv_ch // LANES, conv_rows, LANES), _F32)],
        compiler_params=pltpu.CompilerParams(vmem_limit_bytes=VMEM_LIMIT_BYTES),
        name="sample_proj",
    )(x, *tabs, state, *[wts[k] for k in names])


def _software_pipeline(items, issue, finish):
    items = list(items)
    issued, done = [], []
    for t in range(len(items) + SCORE_LOOKAHEAD):
        if t < len(items):
            issued.append(issue(items[t]))
        if t >= SCORE_LOOKAHEAD:
            done.append(finish(items[t - SCORE_LOOKAHEAD], issued[t - SCORE_LOOKAHEAD]))
    return done


def _prompt_attn_kernel(q_ref, k_ref, vt_ref, g_om_ref, *refs, sub):
    n_cast = (len(refs) - 1) // 2
    o_ref = refs[n_cast]
    for src_ref, dst_ref in zip(refs[:n_cast], refs[n_cast + 1:]):
        dst_ref[...] = src_ref[...].astype(dst_ref.dtype)
    tile = q_ref.shape[1]
    i = pl.program_id(1)
    chunk_of = lambda t: lax.shift_right_logical(t, CHUNK.bit_length() - 1)

    def key_rows(j):
        return pl.ds(pl.multiple_of(j * tile, tile), tile)

    spans = []
    for r in range(tile // sub):
        n_keys = (r + 1) * sub
        key = lax.broadcasted_iota(jnp.int32, (n_keys, sub), 0)
        query = lax.broadcasted_iota(jnp.int32, (n_keys, sub), 1) + r * sub
        spans.append((slice(r * sub, (r + 1) * sub), n_keys, chunk_of(key) <= chunk_of(query)))

    def diag_scores(h, span):
        q_rows, n_keys, mask = span
        keys = pl.ds(pl.multiple_of(i * tile, tile), n_keys)
        return jnp.where(mask, _dot_nt(k_ref[h, keys, :], q_ref[h, q_rows, :]), MASK_VALUE)

    def diag_finish(h, span, s):
        m = jnp.max(s, axis=0, keepdims=True)
        return m, _dot(vt_ref[h, i, :, 0:span[1]], jnp.exp2(s - m).astype(_BF16))

    items = [(h, span) for h in range(MLA_HEADS) for span in spans]
    done = _software_pipeline(items, lambda it: diag_scores(*it), lambda it, s: diag_finish(*it, s))
    per_head = len(spans)
    maxes = [jnp.concatenate([m for m, _ in done[h * per_head:(h + 1) * per_head]], axis=1)
             for h in range(MLA_HEADS)]
    accs = [jnp.concatenate([a for _, a in done[h * per_head:(h + 1) * per_head]], axis=1)
            for h in range(MLA_HEADS)]

    def body(j, carry):
        state = [list(carry[0]), list(carry[1])]
        part = tile // KEY_PARTS

        def issue(item):
            h, t = item
            rows = pl.ds(pl.multiple_of(j * tile + t * part, part), part)
            return _dot_nt(k_ref[h, rows, :], q_ref[h])

        def finish(item, s):
            h, t = item
            m_old, acc = state[0][h], state[1][h]
            m_new = jnp.maximum(m_old, jnp.max(s, axis=0, keepdims=True))
            p = jnp.exp2(s - m_new).astype(_BF16)
            state[0][h] = m_new
            state[1][h] = acc * jnp.exp2(m_old - m_new) + _dot(vt_ref[h, j, :, t * part:(t + 1) * part], p)

        _software_pipeline([(h, t) for h in range(MLA_HEADS) for t in range(KEY_PARTS)], issue, finish)
        return tuple(state[0]), tuple(state[1])

    _, accs = lax.fori_loop(0, i, body, (tuple(maxes), tuple(accs)))

    o_t = jnp.concatenate([acc[0:V_DIM] / acc[V_DIM:V_DIM + 1] for acc in accs], axis=0)
    o_ref[...] = _rms(o_t.T, g_om_ref[...]).astype(o_ref.dtype)


def _cast_block_rows(rows, steps):
    block = -(-rows // (steps * CAST_ROW_ALIGN)) * CAST_ROW_ALIGN
    while rows % block:
        block += CAST_ROW_ALIGN
    return block


def _prompt_attn(q, k, vt, g_om, to_cast, batch, seq):
    tile = vt.shape[3]
    nq = seq // tile
    width = g_om.shape[1]
    cast_specs = []
    for w in to_cast:
        block = _cast_block_rows(w.shape[0], batch * nq)
        cast_specs.append(pl.BlockSpec(
            (block, w.shape[1]),
            lambda b, i, last=w.shape[0] // block - 1: (jnp.minimum(b * nq + i, last), 0)))
    outs = pl.pallas_call(
        functools.partial(_prompt_attn_kernel, sub=min(ATTN_DIAG_TILE, tile)),
        grid=(batch, nq),
        in_specs=[pl.BlockSpec((MLA_HEADS, tile, LANES), lambda b, i: (0, b * nq + i, 0)),
                  pl.BlockSpec((MLA_HEADS, seq, LANES), lambda b, i: (0, b, 0)),
                  pl.BlockSpec((MLA_HEADS, nq, VT_ROWS, tile), lambda b, i: (0, b, 0, 0)),
                  _whole(g_om.shape)] + cast_specs,
        out_specs=[pl.BlockSpec((tile, width), lambda b, i: (b * nq + i, 0))] + cast_specs,
        out_shape=[jax.ShapeDtypeStruct((batch * seq, width), _BF16)]
                  + [jax.ShapeDtypeStruct(w.shape, _BF16) for w in to_cast],
        compiler_params=pltpu.CompilerParams(dimension_semantics=("arbitrary", "arbitrary"),
                                             vmem_limit_bytes=VMEM_LIMIT_BYTES),
        name="prompt_attn",
    )(q, k, vt, g_om, *to_cast)
    return outs[0], outs[1:]


def _sample_attn_kernel(q_ref, ckv_new_ref, kpe_new_ref, ckv_past_ref, kpe_past_t_ref, w_ukt_ref, w_uvh_ref,
                        g_om_ref, o_ref):
    seq = q_ref.shape[1]
    q_all = jnp.concatenate([q_ref[h] for h in range(MLA_HEADS)], axis=0)
    q_lat = jnp.concatenate([_dot(q_ref[h], w_ukt_ref[h]) for h in range(MLA_HEADS)], axis=0).astype(_BF16)
    q_pe = q_all[:, ROPE_LO:ROPE_HI]
    c_past = ckv_past_ref[0].astype(_BF16)
    c_new = ckv_new_ref[...].astype(_BF16)
    s_past = _dot_nt(q_lat, c_past) + _dot(q_pe, kpe_past_t_ref[0].astype(_BF16))
    s_new = _dot_nt(q_lat, c_new) + _dot_nt(q_all, kpe_new_ref[...].astype(_BF16))
    m = jnp.maximum(jnp.max(s_past, axis=-1, keepdims=True), jnp.max(s_new, axis=-1, keepdims=True))
    p_past = jnp.exp2(s_past - m)
    p_new = jnp.exp2(s_new - m)
    denom = jnp.sum(p_past, axis=-1, keepdims=True) + jnp.sum(p_new, axis=-1, keepdims=True)
    o_lat = ((_dot(p_past.astype(_BF16), c_past) + _dot(p_new.astype(_BF16), c_new)) / denom).astype(_BF16)
    o = sum(_dot(o_lat[h * seq:(h + 1) * seq], w_uvh_ref[h]) for h in range(MLA_HEADS))
    o_ref[...] = _rms(o, g_om_ref[...]).astype(o_ref.dtype)


def _sample_attn(q, ckv_new, kpe_new, ckv_past, kpe_past_t, w_ukt, w_uvh, g_om, batch, seq):
    past, kv_lora = ckv_past.shape[1:]
    width = g_om.shape[1]
    return pl.pallas_call(
        _sample_attn_kernel,
        grid=(batch,),
        in_specs=[pl.BlockSpec((MLA_HEADS, seq, LANES), lambda b: (0, b, 0)),
                  pl.BlockSpec((seq, kv_lora), lambda b: (b, 0)),
                  pl.BlockSpec((seq, LANES), lambda b: (b, 0)),
                  pl.BlockSpec((1, past, kv_lora), lambda b: (b, 0, 0)),
                  pl.BlockSpec((1, ROPE_DIM, past), lambda b: (b, 0, 0)),
                  _whole(w_ukt.shape), _whole(w_uvh.shape), _whole(g_om.shape)],
        out_specs=pl.BlockSpec((seq, width), lambda b: (b, 0)),
        out_shape=jax.ShapeDtypeStruct((batch * seq, width), _BF16),
        compiler_params=pltpu.CompilerParams(dimension_semantics=("arbitrary",),
                                             vmem_limit_bytes=VMEM_LIMIT_BYTES),
        name="sample_attn",
    )(q, ckv_new, kpe_new, ckv_past, kpe_past_t, w_ukt, w_uvh, g_om)


def _output_kernel(x_ref, an_ref, cn_ref, w_out_a_ref, w_out_c_ref, ln_ffn_ref, w_gate_ref, w_up_ref,
                   w_down_ref, g_final_ref, y_ref):
    h = x_ref[...] + _dot(an_ref[...], w_out_a_ref[...]) + _dot(cn_ref[...], w_out_c_ref[...])
    f = _rms(h, ln_ffn_ref[...]).astype(_BF16)
    gate = _dot(f, w_gate_ref[...])
    act = (gate * jax.nn.sigmoid(gate) * _dot(f, w_up_ref[...])).astype(_BF16)
    h = h + _dot(act, w_down_ref[...])
    y_ref[...] = _rms(h, g_final_ref[...])


def _output(x, an, cn, wts, name):
    n, d = x.shape
    tm = min(ROW_TILE, n)
    names = ("w_out_a", "w_out_c", "ln_ffn", "w_gate", "w_up", "w_down", "g_final")
    row_block = lambda w: pl.BlockSpec((tm, w), lambda i: (i, 0))
    resident = lambda shape: pl.BlockSpec(shape, lambda i: (0,) * len(shape), pipeline_mode=pl.Buffered(1))
    return pl.pallas_call(
        _output_kernel,
        grid=(n // tm,),
        in_specs=[row_block(d), row_block(an.shape[1]), row_block(cn.shape[1])]
                 + [resident(wts[k].shape) for k in names],
        out_specs=row_block(d),
        out_shape=jax.ShapeDtypeStruct((n, d), _F32),
        compiler_params=pltpu.CompilerParams(dimension_semantics=("arbitrary",),
                                             vmem_limit_bytes=VMEM_LIMIT_BYTES),
        name=name,
    )(x, an, cn, *[wts[k] for k in names])


def _pad_lanes(w, left):
    return jnp.pad(w, [(0, 0)] * (w.ndim - 1) + [(left, LANES - left - w.shape[-1])])


def _prepare_weights(ln_mix, w_in, g_q, w_uq, g_kv, w_uk, w_uv, w_dw, b_dw, g_cn, b_cn, g_om, g_oc, w_out,
                     ln_ffn, g_final):
    q_lora, kv_lora = g_q.shape[0], g_kv.shape[0]
    mla_width = g_om.shape[0]
    row = lambda v: v.reshape(1, -1)
    c1, c2 = q_lora + kv_lora, q_lora + kv_lora + ROPE_DIM
    with_swap = lambda w: jnp.concatenate([w, w[..., HALF_ROPE:], w[..., :HALF_ROPE]], axis=-1)
    w_in_p = jnp.concatenate([w_in[:, :c1], _pad_lanes(with_swap(w_in[:, c1:c2]), ROPE_LO), w_in[:, c2:]], axis=1)
    w_uq_h = w_uq.reshape(q_lora, MLA_HEADS, NOPE_DIM + ROPE_DIM)
    w_uq_p = jnp.concatenate([w_uq_h[..., :NOPE_DIM], with_swap(w_uq_h[..., NOPE_DIM:])], axis=-1)
    w_uq_p = w_uq_p.reshape(q_lora, -1)
    w_uk_p = _pad_lanes(w_uk, 0).reshape(kv_lora, -1)
    w_uvt = jnp.pad(jnp.transpose(w_uv, (1, 2, 0)), ((0, 0), (0, VT_ROWS - V_DIM), (0, 0)))
    v_one = jnp.zeros((MLA_HEADS, VT_ROWS, 1), _F32).at[:, V_DIM].set(1.0)
    w_ukt = jnp.pad(jnp.transpose(w_uk, (1, 2, 0)), ((0, 0), (0, LANES - NOPE_DIM), (0, 0)))
    w_uvh = jnp.stack([jnp.pad(w_uv[:, h], ((0, 0), (h * V_DIM, (MLA_HEADS - 1 - h) * V_DIM)))
                       for h in range(MLA_HEADS)])
    return {
        "ln_mix": row(ln_mix), "w_in": w_in_p.astype(_BF16), "g_q": row(g_q), "w_uq": w_uq_p.astype(_BF16),
        "g_kv": row(g_kv), "w_uk": w_uk_p.astype(_BF16), "w_uvt": w_uvt.reshape(-1, kv_lora).astype(_BF16),
        "v_one": v_one.reshape(-1, 1), "w_dw": w_dw, "b_dw": row(b_dw), "g_cn": row(g_cn), "b_cn": row(b_cn),
        "g_om": row(g_om), "g_oc": row(g_oc),
        "w_ukt": w_ukt.astype(_BF16), "w_uvh": w_uvh.astype(_BF16),
        "w_out_a": w_out[:mla_width].astype(_BF16), "w_out_c": w_out[mla_width:].astype(_BF16),
        "ln_ffn": row(ln_ffn), "g_final": row(g_final),
    }


def _layer(x_prompt, x_sample, ckv_past, kpe_past, conv_past, wts, ffn_f32):
    batch, seq, d = x_prompt.shape
    dec_batch, dec_seq, _ = x_sample.shape
    past = ckv_past.shape[1]

    xp = x_prompt.reshape(batch * seq, d)
    q, k, vt, kv_p, kr_p_t, cn, cv_p = _prompt_proj(xp, _rope_tables(jnp.arange(seq)), wts, seq)
    an, ffn_bf16 = _prompt_attn(q, k, vt, wts["g_om"], list(ffn_f32.values()), batch, seq)
    wts = dict(wts, **dict(zip(ffn_f32, ffn_bf16)))
    y_p = _output(xp, an, cn, wts, "prompt_output")

    xs = x_sample.reshape(dec_batch * dec_seq, d)
    tabs = _rope_tables(jnp.tile(past + jnp.arange(dec_seq), dec_batch))
    q, kv_s, kr_s, cn, cv_s = _sample_proj(xs, tabs, conv_past, wts, dec_seq)
    an = _sample_attn(q, kv_s, kr_s, ckv_past, jnp.swapaxes(kpe_past, 1, 2), wts["w_ukt"], wts["w_uvh"],
                      wts["g_om"], dec_batch, dec_seq)
    y_s = _output(xs, an, cn, wts, "sample_output")

    return (y_p.reshape(batch, seq, d), y_s.reshape(dec_batch, dec_seq, d),
            kv_p.reshape(batch, seq, -1), jnp.swapaxes(kr_p_t, 1, 2), cv_p,
            kv_s.reshape(dec_batch, dec_seq, -1),
            kr_s[:, ROPE_LO:ROPE_HI].reshape(dec_batch, dec_seq, ROPE_DIM), cv_s)


def kernel(x_prompt, x_sample, cache_kv_latent, cache_k_rope, state_conv, ln_mix, w_in, g_q, w_uq, g_kv, w_uk, w_uv, w_dw, b_dw, g_cn, b_cn, g_om, g_oc, w_out, ln_ffn, w_gate, w_up, w_down, g_final):
    depth = w_in.shape[0]
    assert depth == 1, "the kernel implements the single-layer model of the problem"
    wts = _prepare_weights(ln_mix[0], w_in[0], g_q[0], w_uq[0], g_kv[0], w_uk[0], w_uv[0], w_dw[0], b_dw[0],
                           g_cn[0], b_cn[0], g_om[0], g_oc[0], w_out[0], ln_ffn[0], g_final)
    ffn_f32 = {"w_gate": w_gate[0], "w_up": w_up[0], "w_down": w_down[0]}
    outs = _layer(x_prompt, x_sample, cache_kv_latent[0], cache_k_rope[0], state_conv[0], wts, ffn_f32)
    y_p, y_s = outs[0], outs[1]
    return (y_p, y_s) + tuple(o[None] for o in outs[2:])
```

```python
import functools
import math

import jax
import jax.numpy as jnp
from jax import lax
from jax.experimental import pallas as pl
from jax.experimental.pallas import tpu as pltpu

CHUNK = 64
MLA_HEADS = 8
NOPE_DIM = 64
ROPE_DIM = 32
V_DIM = 64
ROPE_THETA = 10000.0
EPS = 1e-6
CONV_W = 31
CONV_STATE = CONV_W - 1
ATTN_SCALE = 1.0 / math.sqrt(NOPE_DIM + ROPE_DIM)
MASK_VALUE = -1e30

LANES = 128
SUBLANES = 8
HALF_ROPE = ROPE_DIM // 2
ROPE_LO = NOPE_DIM
ROPE_HI = ROPE_LO + ROPE_DIM
assert ROPE_HI + ROPE_DIM == LANES
VT_ROWS = 80
HALO_ROWS = 32
HALO_PAD = HALO_ROWS - CONV_STATE
CONV_STRIDE = 4
CONV_BLOCK = SUBLANES * CONV_STRIDE
Q_SCALE = ATTN_SCALE * math.log2(math.e)

ROPE_FINE = 32
HEADS_PER_PIECE = 2
ROW_TILE = 512
ATTN_TILE = 512
KEY_PARTS = 2
SCORE_LOOKAHEAD = 6
ATTN_DIAG_TILE = 256
CAST_ROW_ALIGN = 32
VMEM_LIMIT_BYTES = 56 * 1024 * 1024

_BF16 = jnp.bfloat16
_F32 = jnp.float32


def _rms(x, g):
    return x * lax.rsqrt(jnp.mean(x * x, axis=-1, keepdims=True) + EPS) * g


def _dot(a, b):
    return jnp.dot(a, b, preferred_element_type=_F32)


def _dot_nt(a, b):
    return lax.dot_general(a, b, (((1,), (1,)), ((), ())), preferred_element_type=_F32)


def _rope(x, table, rest):
    y = x * table
    lane = lax.broadcasted_iota(jnp.int32, x.shape, 1)
    in_rope = (lane >= ROPE_LO) & (lane < ROPE_HI)
    return jnp.where(in_rope, y + pltpu.roll(y, LANES - ROPE_DIM, 1), y if rest is None else rest)


def _rope_table_kernel(inv_ref, q_tab_ref, k_tab_ref, *, base, count, fine):
    inv = inv_ref[...].reshape(1, 1, LANES)
    coarse = base + fine * lax.broadcasted_iota(jnp.int32, (count // fine, 1, LANES), 0)
    offset = lax.broadcasted_iota(jnp.int32, (1, fine, LANES), 1)
    ang_a, ang_b = coarse.astype(_F32) * inv, offset.astype(_F32) * inv
    ca, sa, cb, sb = jnp.cos(ang_a), jnp.sin(ang_a), jnp.cos(ang_b), jnp.sin(ang_b)
    c = (ca * cb - sa * sb).reshape(count, LANES)
    s = (sa * cb + ca * sb).reshape(count, LANES)
    lane = lax.broadcasted_iota(jnp.int32, (count, LANES), 1)
    rot = jnp.where(lane < ROPE_HI, c, jnp.where(lane < ROPE_HI + HALF_ROPE, -s, s))
    q_tab = jnp.where(lane < ROPE_LO, Q_SCALE, rot * Q_SCALE)
    for r in range(k_tab_ref.shape[0] // count):
        k_tab_ref[r * count:(r + 1) * count, :] = rot
        q_tab_ref[r * count:(r + 1) * count, :] = q_tab


def _rope_tables(base, count, repeat):
    inv = 1.0 / (ROPE_THETA ** (jnp.arange(0, ROPE_DIM, 2, dtype=_F32) / ROPE_DIM))
    inv_lanes = jnp.zeros((1, LANES), _F32).at[0, ROPE_LO:].set(jnp.tile(inv, 4))
    fine = math.gcd(count, ROPE_FINE)
    assert fine % SUBLANES == 0
    out = jax.ShapeDtypeStruct((count * repeat, LANES), _F32)
    return pl.pallas_call(
        functools.partial(_rope_table_kernel, base=base, count=count, fine=fine),
        out_shape=(out, out),
        name="rope_table",
    )(inv_lanes)


def _in_proj(x_ref, ln_mix_ref, w_in_ref, widths):
    hn = _rms(x_ref[...], ln_mix_ref[...]).astype(_BF16)
    outs, start = [], 0
    for w in widths:
        outs.append(_dot(hn, w_in_ref[:, start:start + w]))
        start += w
    return outs


def _query(cq, g_q_ref, w_uq_ref, q_tab, q_ref):
    q = _dot(_rms(cq, g_q_ref[...]).astype(_BF16), w_uq_ref[...])
    for h in range(MLA_HEADS):
        q_ref[h] = _rope(q[:, h * LANES:(h + 1) * LANES], q_tab, None).astype(q_ref.dtype)


def _depthwise_conv(ubuf, obuf, w_dw_ref, b_dw_ref):
    groups, n, _ = obuf.shape

    def block(g, base):
        lanes = slice(g * LANES, (g + 1) * LANES)
        taps = [ubuf[g, pl.ds(base + HALO_PAD + s, SUBLANES, stride=CONV_STRIDE), :]
                for s in range(CONV_STRIDE + CONV_W - 1)]
        for c in range(CONV_STRIDE):
            acc = jnp.broadcast_to(b_dw_ref[:, lanes], (SUBLANES, LANES))
            for k in range(CONV_W):
                acc = acc + taps[c + k] * w_dw_ref[k:k + 1, lanes]
            obuf[g, pl.ds(base + c, SUBLANES, stride=CONV_STRIDE), :] = acc

    return [functools.partial(block, g, base) for g in range(groups) for base in range(0, n, CONV_BLOCK)]


def _interleave(stages, fillers):
    fillers = list(fillers)
    share = -(-len(fillers) // len(stages))
    for stage in stages:
        stage()
        for filler in fillers[:share]:
            filler()
        fillers = fillers[share:]


def _split_lanes(x):
    return jnp.stack([x[:, c:c + LANES] for c in range(0, x.shape[1], LANES)])


def _merge_lanes(x):
    return jnp.concatenate(list(x), axis=1)


def _conv_branch(dw, g_cn_ref, b_cn_ref, g_oc_ref):
    mu = jnp.mean(dw, axis=-1, keepdims=True)
    xc = dw - mu
    y = xc * lax.rsqrt(jnp.mean(xc * xc, axis=-1, keepdims=True) + EPS) * g_cn_ref[...] + b_cn_ref[...]
    return _rms(y * jax.nn.sigmoid(y), g_oc_ref[...])


def _prompt_proj_kernel(x_ref, q_tab_ref, k_tab_ref, ln_mix_ref, w_in_ref, g_q_ref, w_uq_ref,
                        g_kv_ref, w_uk_ref, w_uvt_ref, v_one_ref, w_dw_ref, b_dw_ref, g_cn_ref, b_cn_ref,
                        g_oc_ref,
                        q_ref, k_ref, vt_ref, ckv_ref, kpe_t_ref, cn_ref, ncv_ref, ubuf, obuf, *, tiles_per_seq):
    rows = x_ref.shape[0]
    q_lora, kv_lora, conv_ch = g_q_ref.shape[1], g_kv_ref.shape[1], g_cn_ref.shape[1]
    hn = _rms(x_ref[...], ln_mix_ref[...]).astype(_BF16)
    w_cq, w_ckv, w_kpe = 0, q_lora, q_lora + kv_lora
    w_a, w_gate = w_kpe + LANES, w_kpe + LANES + conv_ch

    @pl.when(lax.rem(pl.program_id(0), tiles_per_seq) == 0)
    def _():
        ubuf[:, 0:HALO_ROWS, :] = jnp.zeros((ubuf.shape[0], HALO_ROWS, LANES), _F32)

    u = _dot(hn, w_in_ref[:, w_a:w_a + conv_ch]) * jax.nn.sigmoid(_dot(hn, w_in_ref[:, w_gate:w_gate + conv_ch]))
    ubuf[:, HALO_ROWS:HALO_ROWS + rows, :] = _split_lanes(u)
    ncv_ref[0] = u[rows - CONV_STATE:, :]

    state = {}

    def query_latent():
        cq = _dot(hn, w_in_ref[:, w_cq:w_cq + q_lora])
        state["cq"] = _rms(cq, g_q_ref[...]).astype(_BF16)

    def key_latent():
        ckv = _rms(_dot(hn, w_in_ref[:, w_ckv:w_ckv + kv_lora]), g_kv_ref[...])
        ckv_ref[...] = ckv
        state["ckv"] = ckv.astype(_BF16)
        kpe = _rope(_dot(hn, w_in_ref[:, w_kpe:w_kpe + LANES]), k_tab_ref[...], 0.0)
        kpe_t_ref[0] = kpe.T[ROPE_LO:ROPE_HI, :]
        state["kpe"] = kpe

    def query_heads(h0):
        q = _dot(state["cq"], w_uq_ref[:, h0 * LANES:(h0 + HEADS_PER_PIECE) * LANES])
        for j in range(HEADS_PER_PIECE):
            q_ref[h0 + j] = _rope(q[:, j * LANES:(j + 1) * LANES], q_tab_ref[...], None).astype(q_ref.dtype)

    def key_heads(h0):
        k_nope = _dot(state["ckv"], w_uk_ref[:, h0 * LANES:(h0 + HEADS_PER_PIECE) * LANES])
        for j in range(HEADS_PER_PIECE):
            k_ref[h0 + j] = (k_nope[:, j * LANES:(j + 1) * LANES] + state["kpe"]).astype(k_ref.dtype)

    def values():
        vt = _dot_nt(w_uvt_ref[...], state["ckv"]) + v_one_ref[...]
        key_tile = vt_ref.shape[3]
        for h in range(MLA_HEADS):
            for t in range(rows // key_tile):
                vt_ref[h, t] = vt[h * VT_ROWS:(h + 1) * VT_ROWS,
                                  t * key_tile:(t + 1) * key_tile].astype(vt_ref.dtype)

    stages = [query_latent, key_latent]
    for h0 in range(0, MLA_HEADS, HEADS_PER_PIECE):
        stages += [functools.partial(query_heads, h0), functools.partial(key_heads, h0)]
    stages.append(values)
    _interleave(stages, _depthwise_conv(ubuf, obuf, w_dw_ref, b_dw_ref))

    cn_ref[...] = _conv_branch(_merge_lanes(obuf[...]), g_cn_ref, b_cn_ref, g_oc_ref).astype(cn_ref.dtype)
    ubuf[:, 0:HALO_ROWS, :] = ubuf[:, rows:HALO_ROWS + rows, :]


def _sample_proj_kernel(x_ref, q_tab_ref, k_tab_ref, state_ref, ln_mix_ref, w_in_ref, g_q_ref,
                        w_uq_ref, g_kv_ref, w_dw_ref, b_dw_ref, g_cn_ref, b_cn_ref, g_oc_ref,
                        q_ref, ckv_ref, kpe_ref, cn_ref, ncv_ref, ubuf, obuf):
    segs = state_ref.shape[0]
    seq = x_ref.shape[0] // segs
    stride = HALO_ROWS + seq
    q_lora, kv_lora, conv_ch = g_q_ref.shape[1], g_kv_ref.shape[1], g_cn_ref.shape[1]
    cq, ckv, kpe, a, gate = _in_proj(x_ref, ln_mix_ref, w_in_ref, (q_lora, kv_lora, LANES, conv_ch, conv_ch))
    _query(cq, g_q_ref, w_uq_ref, q_tab_ref[...], q_ref)
    ckv_ref[...] = _rms(ckv, g_kv_ref[...])
    kpe_ref[...] = _rope(kpe, k_tab_ref[...], 0.0)

    u = a * jax.nn.sigmoid(gate)
    ubuf[...] = jnp.zeros(ubuf.shape, _F32)
    for s in range(segs):
        ubuf[:, s * stride + HALO_PAD:s * stride + HALO_ROWS, :] = _split_lanes(state_ref[s])
        ubuf[:, s * stride + HALO_ROWS:(s + 1) * stride, :] = _split_lanes(u[s * seq:(s + 1) * seq, :])
    for group in _depthwise_conv(ubuf, obuf, w_dw_ref, b_dw_ref):
        group()
    dw = jnp.concatenate([_merge_lanes(obuf[:, s * stride:s * stride + seq, :]) for s in range(segs)], axis=0)
    cn_ref[...] = _conv_branch(dw, g_cn_ref, b_cn_ref, g_oc_ref).astype(cn_ref.dtype)
    for s in range(segs):
        ncv_ref[s] = jnp.concatenate([state_ref[s][seq:, :], u[s * seq:(s + 1) * seq, :]], axis=0)


def _whole(shape):
    zeros = (0,) * len(shape)
    return pl.BlockSpec(shape, lambda *_: zeros)


def _prompt_proj(x, tabs, wts, seq):
    n, d = x.shape
    tm = min(2 * ROW_TILE, seq)
    key_tile = min(ATTN_TILE, seq)
    assert tm % key_tile == 0 and seq % tm == 0 and tm % CONV_BLOCK == 0
    tiles_per_seq = seq // tm
    conv_ch = wts["g_cn"].shape[1]
    kv_lora = wts["g_kv"].shape[1]
    row_block = lambda w: pl.BlockSpec((tm, w), lambda i: (i, 0))
    head_block = pl.BlockSpec((MLA_HEADS, tm, LANES), lambda i: (0, i, 0))
    vt_block = pl.BlockSpec((MLA_HEADS, tm // key_tile, VT_ROWS, key_tile), lambda i: (0, i, 0, 0))
    tab_block = pl.BlockSpec((tm, LANES), lambda i: (lax.rem(i, tiles_per_seq), 0))
    names = ("ln_mix", "w_in", "g_q", "w_uq", "g_kv", "w_uk", "w_uvt", "v_one", "w_dw", "b_dw", "g_cn",
             "b_cn", "g_oc")
    head_shape = jax.ShapeDtypeStruct((MLA_HEADS, n, LANES), _BF16)
    return pl.pallas_call(
        functools.partial(_prompt_proj_kernel, tiles_per_seq=tiles_per_seq),
        grid=(n // tm,),
        in_specs=[row_block(d), tab_block, tab_block] + [_whole(wts[k].shape) for k in names],
        out_specs=(head_block, head_block, vt_block, row_block(kv_lora),
                   pl.BlockSpec((1, ROPE_DIM, tm), lambda i: (i // tiles_per_seq, 0, lax.rem(i, tiles_per_seq))),
                   row_block(conv_ch), pl.BlockSpec((1, CONV_STATE, conv_ch), lambda i: (i // tiles_per_seq, 0, 0))),
        out_shape=(head_shape, head_shape,
                   jax.ShapeDtypeStruct((MLA_HEADS, n // key_tile, VT_ROWS, key_tile), _BF16),
                   jax.ShapeDtypeStruct((n, kv_lora), _F32), jax.ShapeDtypeStruct((n // seq, ROPE_DIM, seq), _F32),
                   jax.ShapeDtypeStruct((n, conv_ch), _BF16),
                   jax.ShapeDtypeStruct((n // seq, CONV_STATE, conv_ch), _F32)),
        scratch_shapes=[pltpu.VMEM((conv_ch // LANES, HALO_ROWS + tm, LANES), _F32),
                        pltpu.VMEM((conv_ch // LANES, tm, LANES), _F32)],
        compiler_params=pltpu.CompilerParams(dimension_semantics=("arbitrary",),
                                             vmem_limit_bytes=VMEM_LIMIT_BYTES),
        name="prompt_proj",
    )(x, *tabs, *[wts[k] for k in names])


def _sample_proj(x, tabs, state, wts, seq):
    n, d = x.shape
    conv_ch = wts["g_cn"].shape[1]
    kv_lora = wts["g_kv"].shape[1]
    names = ("ln_mix", "w_in", "g_q", "w_uq", "g_kv", "w_dw", "b_dw", "g_cn", "b_cn", "g_oc")
    conv_rows = (n // seq) * (HALO_ROWS + seq)
    assert conv_rows % CONV_BLOCK == 0
    return pl.pallas_call(
        _sample_proj_kernel,
        out_shape=(jax.ShapeDtypeStruct((MLA_HEADS, n, LANES), _BF16),
                   jax.ShapeDtypeStruct((n, kv_lora), _F32), jax.ShapeDtypeStruct((n, LANES), _F32),
                   jax.ShapeDtypeStruct((n, conv_ch), _BF16),
                   jax.ShapeDtypeStruct((n // seq, CONV_STATE, conv_ch), _F32)),
        scratch_shapes=[pltpu.VMEM((conv_ch // LANES, conv_rows + HALO_ROWS, LANES), _F32),
                        pltpu.VMEM((conv_ch // LANES, conv_rows, LANES), _F32)],
        compiler_params=pltpu.CompilerParams(vmem_limit_bytes=VMEM_LIMIT_BYTES),
        name="sample_proj",
    )(x, *tabs, state, *[wts[k] for k in names])


def _software_pipeline(items, issue, finish):
    items = list(items)
    issued, done = [], []
    for t in range(len(items) + SCORE_LOOKAHEAD):
        if t < len(items):
            issued.append(issue(items[t]))
        if t >= SCORE_LOOKAHEAD:
            done.append(finish(items[t - SCORE_LOOKAHEAD], issued[t - SCORE_LOOKAHEAD]))
    return done


def _prompt_attn_kernel(q_ref, k_ref, vt_ref, g_om_ref, *refs, sub):
    n_cast = (len(refs) - 1) // 2
    o_ref = refs[n_cast]
    for src_ref, dst_ref in zip(refs[:n_cast], refs[n_cast + 1:]):
        dst_ref[...] = src_ref[...].astype(dst_ref.dtype)
    tile = q_ref.shape[1]
    i = pl.program_id(1)
    chunk_of = lambda t: lax.shift_right_logical(t, CHUNK.bit_length() - 1)

    def key_rows(j):
        return pl.ds(pl.multiple_of(j * tile, tile), tile)

    spans = []
    for r in range(tile // sub):
        n_keys = (r + 1) * sub
        key = lax.broadcasted_iota(jnp.int32, (n_keys, sub), 0)
        query = lax.broadcasted_iota(jnp.int32, (n_keys, sub), 1) + r * sub
        spans.append((slice(r * sub, (r + 1) * sub), n_keys, chunk_of(key) <= chunk_of(query)))

    def diag_scores(h, span):
        q_rows, n_keys, mask = span
        keys = pl.ds(pl.multiple_of(i * tile, tile), n_keys)
        return jnp.where(mask, _dot_nt(k_ref[h, keys, :], q_ref[h, q_rows, :]), MASK_VALUE)

    def diag_finish(h, span, s):
        m = jnp.max(s, axis=0, keepdims=True)
        return m, _dot(vt_ref[h, i, :, 0:span[1]], jnp.exp2(s - m).astype(_BF16))

    items = [(h, span) for h in range(MLA_HEADS) for span in spans]
    done = _software_pipeline(items, lambda it: diag_scores(*it), lambda it, s: diag_finish(*it, s))
    per_head = len(spans)
    maxes = [jnp.concatenate([m for m, _ in done[h * per_head:(h + 1) * per_head]], axis=1)
             for h in range(MLA_HEADS)]
    accs = [jnp.concatenate([a for _, a in done[h * per_head:(h + 1) * per_head]], axis=1)
            for h in range(MLA_HEADS)]

    def body(j, carry):
        state = [list(carry[0]), list(carry[1])]
        part = tile // KEY_PARTS

        def issue(item):
            h, t = item
            rows = pl.ds(pl.multiple_of(j * tile + t * part, part), part)
            return _dot_nt(k_ref[h, rows, :], q_ref[h])

        def finish(item, s):
            h, t = item
            m_old, acc = state[0][h], state[1][h]
            m_new = jnp.maximum(m_old, jnp.max(s, axis=0, keepdims=True))
            p = jnp.exp2(s - m_new).astype(_BF16)
            state[0][h] = m_new
            state[1][h] = acc * jnp.exp2(m_old - m_new) + _dot(vt_ref[h, j, :, t * part:(t + 1) * part], p)

        _software_pipeline([(h, t) for h in range(MLA_HEADS) for t in range(KEY_PARTS)], issue, finish)
        return tuple(state[0]), tuple(state[1])

    _, accs = lax.fori_loop(0, i, body, (tuple(maxes), tuple(accs)))

    o_t = jnp.concatenate([acc[0:V_DIM] / acc[V_DIM:V_DIM + 1] for acc in accs], axis=0)
    o_ref[...] = _rms(o_t.T, g_om_ref[...]).astype(o_ref.dtype)


def _cast_block_rows(rows, steps):
    block = -(-rows // (steps * CAST_ROW_ALIGN)) * CAST_ROW_ALIGN
    while rows % block:
        block += CAST_ROW_ALIGN
    return block


def _prompt_attn(q, k, vt, g_om, to_cast, batch, seq):
    tile = vt.shape[3]
    nq = seq // tile
    width = g_om.shape[1]
    cast_specs = []
    for w in to_cast:
        block = _cast_block_rows(w.shape[0], batch * nq)
        cast_specs.append(pl.BlockSpec(
            (block, w.shape[1]),
            lambda b, i, last=w.shape[0] // block - 1: (jnp.minimum(b * nq + i, last), 0)))
    outs = pl.pallas_call(
        functools.partial(_prompt_attn_kernel, sub=min(ATTN_DIAG_TILE, tile)),
        grid=(batch, nq),
        in_specs=[pl.BlockSpec((MLA_HEADS, tile, LANES), lambda b, i: (0, b * nq + i, 0)),
                  pl.BlockSpec((MLA_HEADS, seq, LANES), lambda b, i: (0, b, 0)),
                  pl.BlockSpec((MLA_HEADS, nq, VT_ROWS, tile), lambda b, i: (0, b, 0, 0)),
                  _whole(g_om.shape)] + cast_specs,
        out_specs=[pl.BlockSpec((tile, width), lambda b, i: (b * nq + i, 0))] + cast_specs,
        out_shape=[jax.ShapeDtypeStruct((batch * seq, width), _BF16)]
                  + [jax.ShapeDtypeStruct(w.shape, _BF16) for w in to_cast],
        compiler_params=pltpu.CompilerParams(dimension_semantics=("arbitrary", "arbitrary"),
                                             vmem_limit_bytes=VMEM_LIMIT_BYTES),
        name="prompt_attn",
    )(q, k, vt, g_om, *to_cast)
    return outs[0], outs[1:]


def _sample_attn_kernel(q_ref, ckv_new_ref, kpe_new_ref, ckv_past_ref, kpe_past_t_ref, w_ukt_ref, w_uvh_ref,
                        g_om_ref, o_ref):
    seq = q_ref.shape[1]
    q_all = jnp.concatenate([q_ref[h] for h in range(MLA_HEADS)], axis=0)
    q_lat = jnp.concatenate([_dot(q_ref[h], w_ukt_ref[h]) for h in range(MLA_HEADS)], axis=0).astype(_BF16)
    q_pe = q_all[:, ROPE_LO:ROPE_HI]
    c_past = ckv_past_ref[0].astype(_BF16)
    c_new = ckv_new_ref[...].astype(_BF16)
    s_past = _dot_nt(q_lat, c_past) + _dot(q_pe, kpe_past_t_ref[0].astype(_BF16))
    s_new = _dot_nt(q_lat, c_new) + _dot_nt(q_all, kpe_new_ref[...].astype(_BF16))
    m = jnp.maximum(jnp.max(s_past, axis=-1, keepdims=True), jnp.max(s_new, axis=-1, keepdims=True))
    p_past = jnp.exp2(s_past - m)
    p_new = jnp.exp2(s_new - m)
    denom = jnp.sum(p_past, axis=-1, keepdims=True) + jnp.sum(p_new, axis=-1, keepdims=True)
    o_lat = ((_dot(p_past.astype(_BF16), c_past) + _dot(p_new.astype(_BF16), c_new)) / denom).astype(_BF16)
    o = sum(_dot(o_lat[h * seq:(h + 1) * seq], w_uvh_ref[h]) for h in range(MLA_HEADS))
    o_ref[...] = _rms(o, g_om_ref[...]).astype(o_ref.dtype)


def _sample_attn(q, ckv_new, kpe_new, ckv_past, kpe_past_t, w_ukt, w_uvh, g_om, batch, seq):
    past, kv_lora = ckv_past.shape[1:]
    width = g_om.shape[1]
    return pl.pallas_call(
        _sample_attn_kernel,
        grid=(batch,),
        in_specs=[pl.BlockSpec((MLA_HEADS, seq, LANES), lambda b: (0, b, 0)),
                  pl.BlockSpec((seq, kv_lora), lambda b: (b, 0)),
                  pl.BlockSpec((seq, LANES), lambda b: (b, 0)),
                  pl.BlockSpec((1, past, kv_lora), lambda b: (b, 0, 0)),
                  pl.BlockSpec((1, ROPE_DIM, past), lambda b: (b, 0, 0)),
                  _whole(w_ukt.shape), _whole(w_uvh.shape), _whole(g_om.shape)],
        out_specs=pl.BlockSpec((seq, width), lambda b: (b, 0)),
        out_shape=jax.ShapeDtypeStruct((batch * seq, width), _BF16),
        compiler_params=pltpu.CompilerParams(dimension_semantics=("arbitrary",),
                                             vmem_limit_bytes=VMEM_LIMIT_BYTES),
        name="sample_attn",
    )(q, ckv_new, kpe_new, ckv_past, kpe_past_t, w_ukt, w_uvh, g_om)


def _output_kernel(x_ref, an_ref, cn_ref, w_out_a_ref, w_out_c_ref, ln_ffn_ref, w_gate_ref, w_up_ref,
                   w_down_ref, g_final_ref, y_ref):
    h = x_ref[...] + _dot(an_ref[...], w_out_a_ref[...]) + _dot(cn_ref[...], w_out_c_ref[...])
    f = _rms(h, ln_ffn_ref[...]).astype(_BF16)
    gate = _dot(f, w_gate_ref[...])
    act = (gate * jax.nn.sigmoid(gate) * _dot(f, w_up_ref[...])).astype(_BF16)
    h = h + _dot(act, w_down_ref[...])
    y_ref[...] = _rms(h, g_final_ref[...])


def _output(x, an, cn, wts, name):
    n, d = x.shape
    tm = min(ROW_TILE, n)
    names = ("w_out_a", "w_out_c", "ln_ffn", "w_gate", "w_up", "w_down", "g_final")
    row_block = lambda w: pl.BlockSpec((tm, w), lambda i: (i, 0))
    resident = lambda shape: pl.BlockSpec(shape, lambda i: (0,) * len(shape), pipeline_mode=pl.Buffered(1))
    return pl.pallas_call(
        _output_kernel,
        grid=(n // tm,),
        in_specs=[row_block(d), row_block(an.shape[1]), row_block(cn.shape[1])]
                 + [resident(wts[k].shape) for k in names],
        out_specs=row_block(d),
        out_shape=jax.ShapeDtypeStruct((n, d), _F32),
        compiler_params=pltpu.CompilerParams(dimension_semantics=("arbitrary",),
                                             vmem_limit_bytes=VMEM_LIMIT_BYTES),
        name=name,
    )(x, an, cn, *[wts[k] for k in names])


def _pad_lanes(w, left):
    return jnp.pad(w, [(0, 0)] * (w.ndim - 1) + [(left, LANES - left - w.shape[-1])])


def _prepare_weights(ln_mix, w_in, g_q, w_uq, g_kv, w_uk, w_uv, w_dw, b_dw, g_cn, b_cn, g_om, g_oc, w_out,
                     ln_ffn, g_final):
    q_lora, kv_lora = g_q.shape[0], g_kv.shape[0]
    mla_width = g_om.shape[0]
    row = lambda v: v.reshape(1, -1)
    c1, c2 = q_lora + kv_lora, q_lora + kv_lora + ROPE_DIM
    with_swap = lambda w: jnp.concatenate([w, w[..., HALF_ROPE:], w[..., :HALF_ROPE]], axis=-1)
    w_in_p = jnp.concatenate([w_in[:, :c1], _pad_lanes(with_swap(w_in[:, c1:c2]), ROPE_LO), w_in[:, c2:]], axis=1)
    w_uq_h = w_uq.reshape(q_lora, MLA_HEADS, NOPE_DIM + ROPE_DIM)
    w_uq_p = jnp.concatenate([w_uq_h[..., :NOPE_DIM], with_swap(w_uq_h[..., NOPE_DIM:])], axis=-1)
    w_uq_p = w_uq_p.reshape(q_lora, -1)
    w_uk_p = _pad_lanes(w_uk, 0).reshape(kv_lora, -1)
    w_uvt = jnp.pad(jnp.transpose(w_uv, (1, 2, 0)), ((0, 0), (0, VT_ROWS - V_DIM), (0, 0)))
    v_one = jnp.zeros((MLA_HEADS, VT_ROWS, 1), _F32).at[:, V_DIM].set(1.0)
    w_ukt = jnp.pad(jnp.transpose(w_uk, (1, 2, 0)), ((0, 0), (0, LANES - NOPE_DIM), (0, 0)))
    w_uvh = jnp.stack([jnp.pad(w_uv[:, h], ((0, 0), (h * V_DIM, (MLA_HEADS - 1 - h) * V_DIM)))
                       for h in range(MLA_HEADS)])
    return {
        "ln_mix": row(ln_mix), "w_in": w_in_p.astype(_BF16), "g_q": row(g_q), "w_uq": w_uq_p.astype(_BF16),
        "g_kv": row(g_kv), "w_uk": w_uk_p.astype(_BF16), "w_uvt": w_uvt.reshape(-1, kv_lora).astype(_BF16),
        "v_one": v_one.reshape(-1, 1), "w_dw": w_dw, "b_dw": row(b_dw), "g_cn": row(g_cn), "b_cn": row(b_cn),
        "g_om": row(g_om), "g_oc": row(g_oc),
        "w_ukt": w_ukt.astype(_BF16), "w_uvh": w_uvh.astype(_BF16),
        "w_out_a": w_out[:mla_width].astype(_BF16), "w_out_c": w_out[mla_width:].astype(_BF16),
        "ln_ffn": row(ln_ffn), "g_final": row(g_final),
    }


def _layer(x_prompt, x_sample, ckv_past, kpe_past, conv_past, wts, ffn_f32):
    batch, seq, d = x_prompt.shape
    dec_batch, dec_seq, _ = x_sample.shape
    past = ckv_past.shape[1]

    xp = x_prompt.reshape(batch * seq, d)
    q, k, vt, kv_p, kr_p_t, cn, cv_p = _prompt_proj(xp, _rope_tables(0, seq, 1), wts, seq)
    an, ffn_bf16 = _prompt_attn(q, k, vt, wts["g_om"], list(ffn_f32.values()), batch, seq)
    wts = dict(wts, **dict(zip(ffn_f32, ffn_bf16)))
    y_p = _output(xp, an, cn, wts, "prompt_output")

    xs = x_sample.reshape(dec_batch * dec_seq, d)
    tabs = _rope_tables(past, dec_seq, dec_batch)
    q, kv_s, kr_s, cn, cv_s = _sample_proj(xs, tabs, conv_past, wts, dec_seq)
    an = _sample_attn(q, kv_s, kr_s, ckv_past, jnp.swapaxes(kpe_past, 1, 2), wts["w_ukt"], wts["w_uvh"],
                      wts["g_om"], dec_batch, dec_seq)
    y_s = _output(xs, an, cn, wts, "sample_output")

    return (y_p.reshape(batch, seq, d), y_s.reshape(dec_batch, dec_seq, d),
            kv_p.reshape(batch, seq, -1), jnp.swapaxes(kr_p_t, 1, 2), cv_p,
            kv_s.reshape(dec_batch, dec_seq, -1),
            kr_s[:, ROPE_LO:ROPE_HI].reshape(dec_batch, dec_seq, ROPE_DIM), cv_s)


def kernel(x_prompt, x_sample, cache_kv_latent, cache_k_rope, state_conv, ln_mix, w_in, g_q, w_uq, g_kv, w_uk, w_uv, w_dw, b_dw, g_cn, b_cn, g_om, g_oc, w_out, ln_ffn, w_gate, w_up, w_down, g_final):
    depth = w_in.shape[0]
    assert depth == 1, "the kernel implements the single-layer model of the problem"
    wts = _prepare_weights(ln_mix[0], w_in[0], g_q[0], w_uq[0], g_kv[0], w_uk[0], w_uv[0], w_dw[0], b_dw[0],
                           g_cn[0], b_cn[0], g_om[0], g_oc[0], w_out[0], ln_ffn[0], g_final)
    ffn_f32 = {"w_gate": w_gate[0], "w_up": w_up[0], "w_down": w_down[0]}
    outs = _layer(x_prompt, x_sample, cache_kv_latent[0], cache_k_rope[0], state_conv[0], wts, ffn_f32)
    y_p, y_s = outs[0], outs[1]
    return (y_p, y_s) + tuple(o[None] for o in outs[2:])
```

```python
import functools
import math

import jax
import jax.numpy as jnp
from jax import lax
from jax.experimental import pallas as pl
from jax.experimental.pallas import tpu as pltpu

CHUNK = 64
MLA_HEADS = 8
NOPE_DIM = 64
ROPE_DIM = 32
V_DIM = 64
ROPE_THETA = 10000.0
EPS = 1e-6
CONV_W = 31
CONV_STATE = CONV_W - 1
ATTN_SCALE = 1.0 / math.sqrt(NOPE_DIM + ROPE_DIM)
MASK_VALUE = -1e30

LANES = 128
SUBLANES = 8
HALF_ROPE = ROPE_DIM // 2
ROPE_LO = NOPE_DIM
ROPE_HI = ROPE_LO + ROPE_DIM
assert ROPE_HI + ROPE_DIM == LANES
VT_ROWS = 80
HALO_ROWS = 32
HALO_PAD = HALO_ROWS - CONV_STATE
CONV_STRIDE = 4
CONV_BLOCK = SUBLANES * CONV_STRIDE
Q_SCALE = ATTN_SCALE * math.log2(math.e)

ROPE_FINE = 32
GLU_PIECE = 256
HEADS_PER_PIECE = 2
ROW_TILE = 512
ATTN_TILE = 512
KEY_PARTS = 2
SCORE_LOOKAHEAD = 6
ATTN_DIAG_TILE = 256
CAST_ROW_ALIGN = 32
VMEM_LIMIT_BYTES = 56 * 1024 * 1024

_BF16 = jnp.bfloat16
_F32 = jnp.float32


def _rms(x, g):
    return x * lax.rsqrt(jnp.mean(x * x, axis=-1, keepdims=True) + EPS) * g


def _dot(a, b):
    return jnp.dot(a, b, preferred_element_type=_F32)


def _dot_nt(a, b):
    return lax.dot_general(a, b, (((1,), (1,)), ((), ())), preferred_element_type=_F32)


def _rope(x, table, rest):
    y = x * table
    lane = lax.broadcasted_iota(jnp.int32, x.shape, 1)
    in_rope = (lane >= ROPE_LO) & (lane < ROPE_HI)
    return jnp.where(in_rope, y + pltpu.roll(y, LANES - ROPE_DIM, 1), y if rest is None else rest)


def _rope_table_kernel(inv_ref, q_tab_ref, k_tab_ref, *, base, count, fine):
    inv = inv_ref[...].reshape(1, 1, LANES)
    coarse = base + fine * lax.broadcasted_iota(jnp.int32, (count // fine, 1, LANES), 0)
    offset = lax.broadcasted_iota(jnp.int32, (1, fine, LANES), 1)
    ang_a, ang_b = coarse.astype(_F32) * inv, offset.astype(_F32) * inv
    ca, sa, cb, sb = jnp.cos(ang_a), jnp.sin(ang_a), jnp.cos(ang_b), jnp.sin(ang_b)
    c = (ca * cb - sa * sb).reshape(count, LANES)
    s = (sa * cb + ca * sb).reshape(count, LANES)
    lane = lax.broadcasted_iota(jnp.int32, (count, LANES), 1)
    rot = jnp.where(lane < ROPE_HI, c, jnp.where(lane < ROPE_HI + HALF_ROPE, -s, s))
    q_tab = jnp.where(lane < ROPE_LO, Q_SCALE, rot * Q_SCALE)
    for r in range(k_tab_ref.shape[0] // count):
        k_tab_ref[r * count:(r + 1) * count, :] = rot
        q_tab_ref[r * count:(r + 1) * count, :] = q_tab


def _rope_tables(base, count, repeat):
    inv = 1.0 / (ROPE_THETA ** (jnp.arange(0, ROPE_DIM, 2, dtype=_F32) / ROPE_DIM))
    inv_lanes = jnp.zeros((1, LANES), _F32).at[0, ROPE_LO:].set(jnp.tile(inv, 4))
    fine = math.gcd(count, ROPE_FINE)
    assert fine % SUBLANES == 0
    out = jax.ShapeDtypeStruct((count * repeat, LANES), _F32)
    return pl.pallas_call(
        functools.partial(_rope_table_kernel, base=base, count=count, fine=fine),
        out_shape=(out, out),
        name="rope_table",
    )(inv_lanes)


def _in_proj(x_ref, ln_mix_ref, w_in_ref, widths):
    hn = _rms(x_ref[...], ln_mix_ref[...]).astype(_BF16)
    outs, start = [], 0
    for w in widths:
        outs.append(_dot(hn, w_in_ref[:, start:start + w]))
        start += w
    return outs


def _query(cq, g_q_ref, w_uq_ref, q_tab, q_ref):
    q = _dot(_rms(cq, g_q_ref[...]).astype(_BF16), w_uq_ref[...])
    for h in range(MLA_HEADS):
        q_ref[h] = _rope(q[:, h * LANES:(h + 1) * LANES], q_tab, None).astype(q_ref.dtype)


def _depthwise_conv(ubuf, obuf, w_dw_ref, b_dw_ref):
    groups, n, _ = obuf.shape

    def block(g, base):
        lanes = slice(g * LANES, (g + 1) * LANES)
        taps = [ubuf[g, pl.ds(base + HALO_PAD + s, SUBLANES, stride=CONV_STRIDE), :]
                for s in range(CONV_STRIDE + CONV_W - 1)]
        for c in range(CONV_STRIDE):
            acc = jnp.broadcast_to(b_dw_ref[:, lanes], (SUBLANES, LANES))
            for k in range(CONV_W):
                acc = acc + taps[c + k] * w_dw_ref[k:k + 1, lanes]
            obuf[g, pl.ds(base + c, SUBLANES, stride=CONV_STRIDE), :] = acc

    return [functools.partial(block, g, base) for g in range(groups) for base in range(0, n, CONV_BLOCK)]


def _interleave(stages, fillers):
    fillers = list(fillers)
    share = -(-len(fillers) // len(stages))
    for stage in stages:
        stage()
        for filler in fillers[:share]:
            filler()
        fillers = fillers[share:]


def _split_lanes(x):
    return jnp.stack([x[:, c:c + LANES] for c in range(0, x.shape[1], LANES)])


def _merge_lanes(x):
    return jnp.concatenate(list(x), axis=1)


def _conv_branch(dw, g_cn_ref, b_cn_ref, g_oc_ref):
    mu = jnp.mean(dw, axis=-1, keepdims=True)
    xc = dw - mu
    y = xc * lax.rsqrt(jnp.mean(xc * xc, axis=-1, keepdims=True) + EPS) * g_cn_ref[...] + b_cn_ref[...]
    return _rms(y * jax.nn.sigmoid(y), g_oc_ref[...])


def _prompt_proj_kernel(x_ref, q_tab_ref, k_tab_ref, ln_mix_ref, w_in_ref, g_q_ref, w_uq_ref,
                        g_kv_ref, w_uk_ref, w_uvt_ref, v_one_ref, w_dw_ref, b_dw_ref, g_cn_ref, b_cn_ref,
                        g_oc_ref,
                        q_ref, k_ref, vt_ref, ckv_ref, kpe_t_ref, cn_ref, ncv_ref, ubuf, obuf, *, tiles_per_seq):
    rows = x_ref.shape[0]
    q_lora, kv_lora, conv_ch = g_q_ref.shape[1], g_kv_ref.shape[1], g_cn_ref.shape[1]
    hn = _rms(x_ref[...], ln_mix_ref[...]).astype(_BF16)
    w_cq, w_ckv, w_kpe = 0, q_lora, q_lora + kv_lora
    w_a, w_gate = w_kpe + LANES, w_kpe + LANES + conv_ch

    @pl.when(lax.rem(pl.program_id(0), tiles_per_seq) == 0)
    def _():
        ubuf[:, 0:HALO_ROWS, :] = jnp.zeros((ubuf.shape[0], HALO_ROWS, LANES), _F32)

    def glu(c0):
        cols = slice(c0, c0 + GLU_PIECE)
        u = (_dot(hn, w_in_ref[:, w_a + c0:w_a + c0 + GLU_PIECE])
             * jax.nn.sigmoid(_dot(hn, w_in_ref[:, w_gate + c0:w_gate + c0 + GLU_PIECE])))
        ubuf[c0 // LANES:(c0 + GLU_PIECE) // LANES, HALO_ROWS:HALO_ROWS + rows, :] = _split_lanes(u)
        ncv_ref[0, :, cols] = u[rows - CONV_STATE:, :]

    state = {}

    def query_latent():
        cq = _dot(hn, w_in_ref[:, w_cq:w_cq + q_lora])
        state["cq"] = _rms(cq, g_q_ref[...]).astype(_BF16)

    def key_latent():
        ckv = _rms(_dot(hn, w_in_ref[:, w_ckv:w_ckv + kv_lora]), g_kv_ref[...])
        ckv_ref[...] = ckv
        state["ckv"] = ckv.astype(_BF16)
        kpe = _rope(_dot(hn, w_in_ref[:, w_kpe:w_kpe + LANES]), k_tab_ref[...], 0.0)
        kpe_t_ref[0] = kpe.T[ROPE_LO:ROPE_HI, :]
        state["kpe"] = kpe

    def query_heads(h0):
        q = _dot(state["cq"], w_uq_ref[:, h0 * LANES:(h0 + HEADS_PER_PIECE) * LANES])
        for j in range(HEADS_PER_PIECE):
            q_ref[h0 + j] = _rope(q[:, j * LANES:(j + 1) * LANES], q_tab_ref[...], None).astype(q_ref.dtype)

    def key_heads(h0):
        k_nope = _dot(state["ckv"], w_uk_ref[:, h0 * LANES:(h0 + HEADS_PER_PIECE) * LANES])
        for j in range(HEADS_PER_PIECE):
            k_ref[h0 + j] = (k_nope[:, j * LANES:(j + 1) * LANES] + state["kpe"]).astype(k_ref.dtype)

    def values():
        vt = _dot_nt(w_uvt_ref[...], state["ckv"]) + v_one_ref[...]
        key_tile = vt_ref.shape[3]
        for h in range(MLA_HEADS):
            for t in range(rows // key_tile):
                vt_ref[h, t] = vt[h * VT_ROWS:(h + 1) * VT_ROWS,
                                  t * key_tile:(t + 1) * key_tile].astype(vt_ref.dtype)

    stages = [query_latent, key_latent]
    for h0 in range(0, MLA_HEADS, HEADS_PER_PIECE):
        stages += [functools.partial(query_heads, h0), functools.partial(key_heads, h0)]
    stages.append(values)
    glu(0)
    later_glu = [functools.partial(glu, c0) for c0 in range(GLU_PIECE, conv_ch, GLU_PIECE)]
    _interleave(later_glu + stages, _depthwise_conv(ubuf, obuf, w_dw_ref, b_dw_ref))

    cn_ref[...] = _conv_branch(_merge_lanes(obuf[...]), g_cn_ref, b_cn_ref, g_oc_ref).astype(cn_ref.dtype)
    ubuf[:, 0:HALO_ROWS, :] = ubuf[:, rows:HALO_ROWS + rows, :]


def _sample_proj_kernel(x_ref, q_tab_ref, k_tab_ref, state_ref, ln_mix_ref, w_in_ref, g_q_ref,
                        w_uq_ref, g_kv_ref, w_dw_ref, b_dw_ref, g_cn_ref, b_cn_ref, g_oc_ref,
                        q_ref, ckv_ref, kpe_ref, cn_ref, ncv_ref, ubuf, obuf):
    segs = state_ref.shape[0]
    seq = x_ref.shape[0] // segs
    stride = HALO_ROWS + seq
    q_lora, kv_lora, conv_ch = g_q_ref.shape[1], g_kv_ref.shape[1], g_cn_ref.shape[1]
    cq, ckv, kpe, a, gate = _in_proj(x_ref, ln_mix_ref, w_in_ref, (q_lora, kv_lora, LANES, conv_ch, conv_ch))
    _query(cq, g_q_ref, w_uq_ref, q_tab_ref[...], q_ref)
    ckv_ref[...] = _rms(ckv, g_kv_ref[...])
    kpe_ref[...] = _rope(kpe, k_tab_ref[...], 0.0)

    u = a * jax.nn.sigmoid(gate)
    ubuf[...] = jnp.zeros(ubuf.shape, _F32)
    for s in range(segs):
        ubuf[:, s * stride + HALO_PAD:s * stride + HALO_ROWS, :] = _split_lanes(state_ref[s])
        ubuf[:, s * stride + HALO_ROWS:(s + 1) * stride, :] = _split_lanes(u[s * seq:(s + 1) * seq, :])
    for group in _depthwise_conv(ubuf, obuf, w_dw_ref, b_dw_ref):
        group()
    dw = jnp.concatenate([_merge_lanes(obuf[:, s * stride:s * stride + seq, :]) for s in range(segs)], axis=0)
    cn_ref[...] = _conv_branch(dw, g_cn_ref, b_cn_ref, g_oc_ref).astype(cn_ref.dtype)
    for s in range(segs):
        ncv_ref[s] = jnp.concatenate([state_ref[s][seq:, :], u[s * seq:(s + 1) * seq, :]], axis=0)


def _whole(shape):
    zeros = (0,) * len(shape)
    return pl.BlockSpec(shape, lambda *_: zeros)


def _prompt_proj(x, tabs, wts, seq):
    n, d = x.shape
    tm = min(2 * ROW_TILE, seq)
    key_tile = min(ATTN_TILE, seq)
    assert tm % key_tile == 0 and seq % tm == 0 and tm % CONV_BLOCK == 0
    tiles_per_seq = seq // tm
    conv_ch = wts["g_cn"].shape[1]
    kv_lora = wts["g_kv"].shape[1]
    row_block = lambda w: pl.BlockSpec((tm, w), lambda i: (i, 0))
    head_block = pl.BlockSpec((MLA_HEADS, tm, LANES), lambda i: (0, i, 0))
    vt_block = pl.BlockSpec((MLA_HEADS, tm // key_tile, VT_ROWS, key_tile), lambda i: (0, i, 0, 0))
    tab_block = pl.BlockSpec((tm, LANES), lambda i: (lax.rem(i, tiles_per_seq), 0))
    names = ("ln_mix", "w_in", "g_q", "w_uq", "g_kv", "w_uk", "w_uvt", "v_one", "w_dw", "b_dw", "g_cn",
             "b_cn", "g_oc")
    head_shape = jax.ShapeDtypeStruct((MLA_HEADS, n, LANES), _BF16)
    return pl.pallas_call(
        functools.partial(_prompt_proj_kernel, tiles_per_seq=tiles_per_seq),
        grid=(n // tm,),
        in_specs=[row_block(d), tab_block, tab_block] + [_whole(wts[k].shape) for k in names],
        out_specs=(head_block, head_block, vt_block, row_block(kv_lora),
                   pl.BlockSpec((1, ROPE_DIM, tm), lambda i: (i // tiles_per_seq, 0, lax.rem(i, tiles_per_seq))),
                   row_block(conv_ch), pl.BlockSpec((1, CONV_STATE, conv_ch), lambda i: (i // tiles_per_seq, 0, 0))),
        out_shape=(head_shape, head_shape,
                   jax.ShapeDtypeStruct((MLA_HEADS, n // key_tile, VT_ROWS, key_tile), _BF16),
                   jax.ShapeDtypeStruct((n, kv_lora), _F32), jax.ShapeDtypeStruct((n // seq, ROPE_DIM, seq), _F32),
                   jax.ShapeDtypeStruct((n, conv_ch), _BF16),
                   jax.ShapeDtypeStruct((n // seq, CONV_STATE, conv_ch), _F32)),
        scratch_shapes=[pltpu.VMEM((conv_ch // LANES, HALO_ROWS + tm, LANES), _F32),
                        pltpu.VMEM((conv_ch // LANES, tm, LANES), _F32)],
        compiler_params=pltpu.CompilerParams(dimension_semantics=("arbitrary",),
                                             vmem_limit_bytes=VMEM_LIMIT_BYTES),
        name="prompt_proj",
    )(x, *tabs, *[wts[k] for k in names])


def _sample_proj(x, tabs, state, wts, seq):
    n, d = x.shape
    conv_ch = wts["g_cn"].shape[1]
    kv_lora = wts["g_kv"].shape[1]
    names = ("ln_mix", "w_in", "g_q", "w_uq", "g_kv", "w_dw", "b_dw", "g_cn", "b_cn", "g_oc")
    conv_rows = (n // seq) * (HALO_ROWS + seq)
    assert conv_rows % CONV_BLOCK == 0
    return pl.pallas_call(
        _sample_proj_kernel,
        out_shape=(jax.ShapeDtypeStruct((MLA_HEADS, n, LANES), _BF16),
                   jax.ShapeDtypeStruct((n, kv_lora), _F32), jax.ShapeDtypeStruct((n, LANES), _F32),
                   jax.ShapeDtypeStruct((n, conv_ch), _BF16),
                   jax.ShapeDtypeStruct((n // seq, CONV_STATE, conv_ch), _F32)),
        scratch_shapes=[pltpu.VMEM((conv_ch // LANES, conv_rows + HALO_ROWS, LANES), _F32),
                        pltpu.VMEM((conv_ch // LANES, conv_rows, LANES), _F32)],
        compiler_params=pltpu.CompilerParams(vmem_limit_bytes=VMEM_LIMIT_BYTES),
        name="sample_proj",
    )(x, *tabs, state, *[wts[k] for k in names])


def _software_pipeline(items, issue, finish):
    items = list(items)
    issued, done = [], []
    for t in range(len(items) + SCORE_LOOKAHEAD):
        if t < len(items):
            issued.append(issue(items[t]))
        if t >= SCORE_LOOKAHEAD:
            done.append(finish(items[t - SCORE_LOOKAHEAD], issued[t - SCORE_LOOKAHEAD]))
    return done


def _prompt_attn_kernel(q_ref, k_ref, vt_ref, g_om_ref, *refs, sub):
    n_cast = (len(refs) - 1) // 2
    o_ref = refs[n_cast]
    for src_ref, dst_ref in zip(refs[:n_cast], refs[n_cast + 1:]):
        dst_ref[...] = src_ref[...].astype(dst_ref.dtype)
    tile = q_ref.shape[1]
    i = pl.program_id(1)
    chunk_of = lambda t: lax.shift_right_logical(t, CHUNK.bit_length() - 1)

    def key_rows(j):
        return pl.ds(pl.multiple_of(j * tile, tile), tile)

    spans = []
    for r in range(tile // sub):
        n_keys = (r + 1) * sub
        key = lax.broadcasted_iota(jnp.int32, (n_keys, sub), 0)
        query = lax.broadcasted_iota(jnp.int32, (n_keys, sub), 1) + r * sub
        spans.append((slice(r * sub, (r + 1) * sub), n_keys, chunk_of(key) <= chunk_of(query)))

    def diag_scores(h, span):
        q_rows, n_keys, mask = span
        keys = pl.ds(pl.multiple_of(i * tile, tile), n_keys)
        return jnp.where(mask, _dot_nt(k_ref[h, keys, :], q_ref[h, q_rows, :]), MASK_VALUE)

    def diag_finish(h, span, s):
        m = jnp.max(s, axis=0, keepdims=True)
        return m, _dot(vt_ref[h, i, :, 0:span[1]], jnp.exp2(s - m).astype(_BF16))

    items = [(h, span) for h in range(MLA_HEADS) for span in spans]
    done = _software_pipeline(items, lambda it: diag_scores(*it), lambda it, s: diag_finish(*it, s))
    per_head = len(spans)
    maxes = [jnp.concatenate([m for m, _ in done[h * per_head:(h + 1) * per_head]], axis=1)
             for h in range(MLA_HEADS)]
    accs = [jnp.concatenate([a for _, a in done[h * per_head:(h + 1) * per_head]], axis=1)
            for h in range(MLA_HEADS)]

    def body(j, carry):
        state = [list(carry[0]), list(carry[1])]
        part = tile // KEY_PARTS

        def issue(item):
            h, t = item
            rows = pl.ds(pl.multiple_of(j * tile + t * part, part), part)
            return _dot_nt(k_ref[h, rows, :], q_ref[h])

        def finish(item, s):
            h, t = item
            m_old, acc = state[0][h], state[1][h]
            m_new = jnp.maximum(m_old, jnp.max(s, axis=0, keepdims=True))
            p = jnp.exp2(s - m_new).astype(_BF16)
            state[0][h] = m_new
            state[1][h] = acc * jnp.exp2(m_old - m_new) + _dot(vt_ref[h, j, :, t * part:(t + 1) * part], p)

        _software_pipeline([(h, t) for h in range(MLA_HEADS) for t in range(KEY_PARTS)], issue, finish)
        return tuple(state[0]), tuple(state[1])

    _, accs = lax.fori_loop(0, i, body, (tuple(maxes), tuple(accs)))

    o_t = jnp.concatenate([acc[0:V_DIM] / acc[V_DIM:V_DIM + 1] for acc in accs], axis=0)
    o_ref[...] = _rms(o_t.T, g_om_ref[...]).astype(o_ref.dtype)


def _cast_block_rows(rows, steps):
    block = -(-rows // (steps * CAST_ROW_ALIGN)) * CAST_ROW_ALIGN
    while rows % block:
        block += CAST_ROW_ALIGN
    return block


def _prompt_attn(q, k, vt, g_om, to_cast, batch, seq):
    tile = vt.shape[3]
    nq = seq // tile
    width = g_om.shape[1]
    cast_specs = []
    for w in to_cast:
        block = _cast_block_rows(w.shape[0], batch * nq)
        cast_specs.append(pl.BlockSpec(
            (block, w.shape[1]),
            lambda b, i, last=w.shape[0] // block - 1: (jnp.minimum(b * nq + i, last), 0)))
    outs = pl.pallas_call(
        functools.partial(_prompt_attn_kernel, sub=min(ATTN_DIAG_TILE, tile)),
        grid=(batch, nq),
        in_specs=[pl.BlockSpec((MLA_HEADS, tile, LANES), lambda b, i: (0, b * nq + i, 0)),
                  pl.BlockSpec((MLA_HEADS, seq, LANES), lambda b, i: (0, b, 0)),
                  pl.BlockSpec((MLA_HEADS, nq, VT_ROWS, tile), lambda b, i: (0, b, 0, 0)),
                  _whole(g_om.shape)] + cast_specs,
        out_specs=[pl.BlockSpec((tile, width), lambda b, i: (b * nq + i, 0))] + cast_specs,
        out_shape=[jax.ShapeDtypeStruct((batch * seq, width), _BF16)]
                  + [jax.ShapeDtypeStruct(w.shape, _BF16) for w in to_cast],
        compiler_params=pltpu.CompilerParams(dimension_semantics=("arbitrary", "arbitrary"),
                                             vmem_limit_bytes=VMEM_LIMIT_BYTES),
        name="prompt_attn",
    )(q, k, vt, g_om, *to_cast)
    return outs[0], outs[1:]


def _sample_attn_kernel(q_ref, ckv_new_ref, kpe_new_ref, ckv_past_ref, kpe_past_t_ref, w_ukt_ref, w_uvh_ref,
                        g_om_ref, o_ref):
    seq = q_ref.shape[1]
    q_all = jnp.concatenate([q_ref[h] for h in range(MLA_HEADS)], axis=0)
    q_lat = jnp.concatenate([_dot(q_ref[h], w_ukt_ref[h]) for h in range(MLA_HEADS)], axis=0).astype(_BF16)
    q_pe = q_all[:, ROPE_LO:ROPE_HI]
    c_past = ckv_past_ref[0].astype(_BF16)
    c_new = ckv_new_ref[...].astype(_BF16)
    s_past = _dot_nt(q_lat, c_past) + _dot(q_pe, kpe_past_t_ref[0].astype(_BF16))
    s_new = _dot_nt(q_lat, c_new) + _dot_nt(q_all, kpe_new_ref[...].astype(_BF16))
    m = jnp.maximum(jnp.max(s_past, axis=-1, keepdims=True), jnp.max(s_new, axis=-1, keepdims=True))
    p_past = jnp.exp2(s_past - m)
    p_new = jnp.exp2(s_new - m)
    denom = jnp.sum(p_past, axis=-1, keepdims=True) + jnp.sum(p_new, axis=-1, keepdims=True)
    o_lat = ((_dot(p_past.astype(_BF16), c_past) + _dot(p_new.astype(_BF16), c_new)) / denom).astype(_BF16)
    o = sum(_dot(o_lat[h * seq:(h + 1) * seq], w_uvh_ref[h]) for h in range(MLA_HEADS))
    o_ref[...] = _rms(o, g_om_ref[...]).astype(o_ref.dtype)


def _sample_attn(q, ckv_new, kpe_new, ckv_past, kpe_past_t, w_ukt, w_uvh, g_om, batch, seq):
    past, kv_lora = ckv_past.shape[1:]
    width = g_om.shape[1]
    return pl.pallas_call(
        _sample_attn_kernel,
        grid=(batch,),
        in_specs=[pl.BlockSpec((MLA_HEADS, seq, LANES), lambda b: (0, b, 0)),
                  pl.BlockSpec((seq, kv_lora), lambda b: (b, 0)),
                  pl.BlockSpec((seq, LANES), lambda b: (b, 0)),
                  pl.BlockSpec((1, past, kv_lora), lambda b: (b, 0, 0)),
                  pl.BlockSpec((1, ROPE_DIM, past), lambda b: (b, 0, 0)),
                  _whole(w_ukt.shape), _whole(w_uvh.shape), _whole(g_om.shape)],
        out_specs=pl.BlockSpec((seq, width), lambda b: (b, 0)),
        out_shape=jax.ShapeDtypeStruct((batch * seq, width), _BF16),
        compiler_params=pltpu.CompilerParams(dimension_semantics=("arbitrary",),
                                             vmem_limit_bytes=VMEM_LIMIT_BYTES),
        name="sample_attn",
    )(q, ckv_new, kpe_new, ckv_past, kpe_past_t, w_ukt, w_uvh, g_om)


def _output_kernel(x_ref, an_ref, cn_ref, w_out_a_ref, w_out_c_ref, ln_ffn_ref, w_gate_ref, w_up_ref,
                   w_down_ref, g_final_ref, y_ref):
    h = x_ref[...] + _dot(an_ref[...], w_out_a_ref[...]) + _dot(cn_ref[...], w_out_c_ref[...])
    f = _rms(h, ln_ffn_ref[...]).astype(_BF16)
    gate = _dot(f, w_gate_ref[...])
    act = (gate * jax.nn.sigmoid(gate) * _dot(f, w_up_ref[...])).astype(_BF16)
    h = h + _dot(act, w_down_ref[...])
    y_ref[...] = _rms(h, g_final_ref[...])


def _output(x, an, cn, wts, name):
    n, d = x.shape
    tm = min(ROW_TILE, n)
    names = ("w_out_a", "w_out_c", "ln_ffn", "w_gate", "w_up", "w_down", "g_final")
    row_block = lambda w: pl.BlockSpec((tm, w), lambda i: (i, 0))
    resident = lambda shape: pl.BlockSpec(shape, lambda i: (0,) * len(shape), pipeline_mode=pl.Buffered(1))
    return pl.pallas_call(
        _output_kernel,
        grid=(n // tm,),
        in_specs=[row_block(d), row_block(an.shape[1]), row_block(cn.shape[1])]
                 + [resident(wts[k].shape) for k in names],
        out_specs=row_block(d),
        out_shape=jax.ShapeDtypeStruct((n, d), _F32),
        compiler_params=pltpu.CompilerParams(dimension_semantics=("arbitrary",),
                                             vmem_limit_bytes=VMEM_LIMIT_BYTES),
        name=name,
    )(x, an, cn, *[wts[k] for k in names])


def _pad_lanes(w, left):
    return jnp.pad(w, [(0, 0)] * (w.ndim - 1) + [(left, LANES - left - w.shape[-1])])


def _prepare_weights(ln_mix, w_in, g_q, w_uq, g_kv, w_uk, w_uv, w_dw, b_dw, g_cn, b_cn, g_om, g_oc, w_out,
                     ln_ffn, g_final):
    q_lora, kv_lora = g_q.shape[0], g_kv.shape[0]
    mla_width = g_om.shape[0]
    row = lambda v: v.reshape(1, -1)
    c1, c2 = q_lora + kv_lora, q_lora + kv_lora + ROPE_DIM
    with_swap = lambda w: jnp.concatenate([w, w[..., HALF_ROPE:], w[..., :HALF_ROPE]], axis=-1)
    w_in_p = jnp.concatenate([w_in[:, :c1], _pad_lanes(with_swap(w_in[:, c1:c2]), ROPE_LO), w_in[:, c2:]], axis=1)
    w_uq_h = w_uq.reshape(q_lora, MLA_HEADS, NOPE_DIM + ROPE_DIM)
    w_uq_p = jnp.concatenate([w_uq_h[..., :NOPE_DIM], with_swap(w_uq_h[..., NOPE_DIM:])], axis=-1)
    w_uq_p = w_uq_p.reshape(q_lora, -1)
    w_uk_p = _pad_lanes(w_uk, 0).reshape(kv_lora, -1)
    w_uvt = jnp.pad(jnp.transpose(w_uv, (1, 2, 0)), ((0, 0), (0, VT_ROWS - V_DIM), (0, 0)))
    v_one = jnp.zeros((MLA_HEADS, VT_ROWS, 1), _F32).at[:, V_DIM].set(1.0)
    w_ukt = jnp.pad(jnp.transpose(w_uk, (1, 2, 0)), ((0, 0), (0, LANES - NOPE_DIM), (0, 0)))
    w_uvh = jnp.stack([jnp.pad(w_uv[:, h], ((0, 0), (h * V_DIM, (MLA_HEADS - 1 - h) * V_DIM)))
                       for h in range(MLA_HEADS)])
    return {
        "ln_mix": row(ln_mix), "w_in": w_in_p.astype(_BF16), "g_q": row(g_q), "w_uq": w_uq_p.astype(_BF16),
        "g_kv": row(g_kv), "w_uk": w_uk_p.astype(_BF16), "w_uvt": w_uvt.reshape(-1, kv_lora).astype(_BF16),
        "v_one": v_one.reshape(-1, 1), "w_dw": w_dw, "b_dw": row(b_dw), "g_cn": row(g_cn), "b_cn": row(b_cn),
        "g_om": row(g_om), "g_oc": row(g_oc),
        "w_ukt": w_ukt.astype(_BF16), "w_uvh": w_uvh.astype(_BF16),
        "w_out_a": w_out[:mla_width].astype(_BF16), "w_out_c": w_out[mla_width:].astype(_BF16),
        "ln_ffn": row(ln_ffn), "g_final": row(g_final),
    }


def _layer(x_prompt, x_sample, ckv_past, kpe_past, conv_past, wts, ffn_f32):
    batch, seq, d = x_prompt.shape
    dec_batch, dec_seq, _ = x_sample.shape
    past = ckv_past.shape[1]

    xp = x_prompt.reshape(batch * seq, d)
    q, k, vt, kv_p, kr_p_t, cn, cv_p = _prompt_proj(xp, _rope_tables(0, seq, 1), wts, seq)
    an, ffn_bf16 = _prompt_attn(q, k, vt, wts["g_om"], list(ffn_f32.values()), batch, seq)
    wts = dict(wts, **dict(zip(ffn_f32, ffn_bf16)))
    y_p = _output(xp, an, cn, wts, "prompt_output")

    xs = x_sample.reshape(dec_batch * dec_seq, d)
    tabs = _rope_tables(past, dec_seq, dec_batch)
    q, kv_s, kr_s, cn, cv_s = _sample_proj(xs, tabs, conv_past, wts, dec_seq)
    an = _sample_attn(q, kv_s, kr_s, ckv_past, jnp.swapaxes(kpe_past, 1, 2), wts["w_ukt"], wts["w_uvh"],
                      wts["g_om"], dec_batch, dec_seq)
    y_s = _output(xs, an, cn, wts, "sample_output")

    return (y_p.reshape(batch, seq, d), y_s.reshape(dec_batch, dec_seq, d),
            kv_p.reshape(batch, seq, -1), jnp.swapaxes(kr_p_t, 1, 2), cv_p,
            kv_s.reshape(dec_batch, dec_seq, -1),
            kr_s[:, ROPE_LO:ROPE_HI].reshape(dec_batch, dec_seq, ROPE_DIM), cv_s)


def kernel(x_prompt, x_sample, cache_kv_latent, cache_k_rope, state_conv, ln_mix, w_in, g_q, w_uq, g_kv, w_uk, w_uv, w_dw, b_dw, g_cn, b_cn, g_om, g_oc, w_out, ln_ffn, w_gate, w_up, w_down, g_final):
    depth = w_in.shape[0]
    assert depth == 1, "the kernel implements the single-layer model of the problem"
    wts = _prepare_weights(ln_mix[0], w_in[0], g_q[0], w_uq[0], g_kv[0], w_uk[0], w_uv[0], w_dw[0], b_dw[0],
                           g_cn[0], b_cn[0], g_om[0], g_oc[0], w_out[0], ln_ffn[0], g_final)
    ffn_f32 = {"w_gate": w_gate[0], "w_up": w_up[0], "w_down": w_down[0]}
    outs = _layer(x_prompt, x_sample, cache_kv_latent[0], cache_k_rope[0], state_conv[0], wts, ffn_f32)
    y_p, y_s = outs[0], outs[1]
    return (y_p, y_s) + tuple(o[None] for o in outs[2:])
```

```python
import functools
import math

import jax
import jax.numpy as jnp
from jax import lax
from jax.experimental import pallas as pl
from jax.experimental.pallas import tpu as pltpu

CHUNK = 64
MLA_HEADS = 8
NOPE_DIM = 64
ROPE_DIM = 32
V_DIM = 64
ROPE_THETA = 10000.0
EPS = 1e-6
CONV_W = 31
CONV_STATE = CONV_W - 1
ATTN_SCALE = 1.0 / math.sqrt(NOPE_DIM + ROPE_DIM)
MASK_VALUE = -1e30

LANES = 128
SUBLANES = 8
HALF_ROPE = ROPE_DIM // 2
ROPE_LO = NOPE_DIM
ROPE_HI = ROPE_LO + ROPE_DIM
assert ROPE_HI + ROPE_DIM == LANES
VT_ROWS = 80
HALO_ROWS = 32
HALO_PAD = HALO_ROWS - CONV_STATE
CONV_STRIDE = 4
CONV_BLOCK = SUBLANES * CONV_STRIDE
Q_SCALE = ATTN_SCALE * math.log2(math.e)

ROPE_FINE = 32
GLU_PIECE = 256
HEADS_PER_PIECE = 2
ROW_TILE = 512
ATTN_TILE = 1024
KEY_PARTS = 4
SCORE_LOOKAHEAD = 6
ATTN_DIAG_TILE = 256
CAST_ROW_ALIGN = 32
VMEM_LIMIT_BYTES = 56 * 1024 * 1024

_BF16 = jnp.bfloat16
_F32 = jnp.float32


def _rms(x, g):
    return x * lax.rsqrt(jnp.mean(x * x, axis=-1, keepdims=True) + EPS) * g


def _dot(a, b):
    return jnp.dot(a, b, preferred_element_type=_F32)


def _dot_nt(a, b):
    return lax.dot_general(a, b, (((1,), (1,)), ((), ())), preferred_element_type=_F32)


def _rope(x, table, rest):
    y = x * table
    lane = lax.broadcasted_iota(jnp.int32, x.shape, 1)
    in_rope = (lane >= ROPE_LO) & (lane < ROPE_HI)
    return jnp.where(in_rope, y + pltpu.roll(y, LANES - ROPE_DIM, 1), y if rest is None else rest)


def _rope_table_kernel(inv_ref, q_tab_ref, k_tab_ref, *, base, count, fine):
    inv = inv_ref[...].reshape(1, 1, LANES)
    coarse = base + fine * lax.broadcasted_iota(jnp.int32, (count // fine, 1, LANES), 0)
    offset = lax.broadcasted_iota(jnp.int32, (1, fine, LANES), 1)
    ang_a, ang_b = coarse.astype(_F32) * inv, offset.astype(_F32) * inv
    ca, sa, cb, sb = jnp.cos(ang_a), jnp.sin(ang_a), jnp.cos(ang_b), jnp.sin(ang_b)
    c = (ca * cb - sa * sb).reshape(count, LANES)
    s = (sa * cb + ca * sb).reshape(count, LANES)
    lane = lax.broadcasted_iota(jnp.int32, (count, LANES), 1)
    rot = jnp.where(lane < ROPE_HI, c, jnp.where(lane < ROPE_HI + HALF_ROPE, -s, s))
    q_tab = jnp.where(lane < ROPE_LO, Q_SCALE, rot * Q_SCALE)
    for r in range(k_tab_ref.shape[0] // count):
        k_tab_ref[r * count:(r + 1) * count, :] = rot
        q_tab_ref[r * count:(r + 1) * count, :] = q_tab


def _rope_tables(base, count, repeat):
    inv = 1.0 / (ROPE_THETA ** (jnp.arange(0, ROPE_DIM, 2, dtype=_F32) / ROPE_DIM))
    inv_lanes = jnp.zeros((1, LANES), _F32).at[0, ROPE_LO:].set(jnp.tile(inv, 4))
    fine = math.gcd(count, ROPE_FINE)
    assert fine % SUBLANES == 0
    out = jax.ShapeDtypeStruct((count * repeat, LANES), _F32)
    return pl.pallas_call(
        functools.partial(_rope_table_kernel, base=base, count=count, fine=fine),
        out_shape=(out, out),
        name="rope_table",
    )(inv_lanes)


def _in_proj(x_ref, ln_mix_ref, w_in_ref, widths):
    hn = _rms(x_ref[...], ln_mix_ref[...]).astype(_BF16)
    outs, start = [], 0
    for w in widths:
        outs.append(_dot(hn, w_in_ref[:, start:start + w]))
        start += w
    return outs


def _query(cq, g_q_ref, w_uq_ref, q_tab, q_ref):
    q = _dot(_rms(cq, g_q_ref[...]).astype(_BF16), w_uq_ref[...])
    for h in range(MLA_HEADS):
        q_ref[h] = _rope(q[:, h * LANES:(h + 1) * LANES], q_tab, None).astype(q_ref.dtype)


def _depthwise_conv(ubuf, obuf, w_dw_ref, b_dw_ref):
    groups, n, _ = obuf.shape

    def block(g, base):
        lanes = slice(g * LANES, (g + 1) * LANES)
        taps = [ubuf[g, pl.ds(base + HALO_PAD + s, SUBLANES, stride=CONV_STRIDE), :]
                for s in range(CONV_STRIDE + CONV_W - 1)]
        for c in range(CONV_STRIDE):
            acc = jnp.broadcast_to(b_dw_ref[:, lanes], (SUBLANES, LANES))
            for k in range(CONV_W):
                acc = acc + taps[c + k] * w_dw_ref[k:k + 1, lanes]
            obuf[g, pl.ds(base + c, SUBLANES, stride=CONV_STRIDE), :] = acc

    return [functools.partial(block, g, base) for g in range(groups) for base in range(0, n, CONV_BLOCK)]


def _interleave(stages, fillers):
    fillers = list(fillers)
    share = -(-len(fillers) // len(stages))
    for stage in stages:
        stage()
        for filler in fillers[:share]:
            filler()
        fillers = fillers[share:]


def _split_lanes(x):
    return jnp.stack([x[:, c:c + LANES] for c in range(0, x.shape[1], LANES)])


def _merge_lanes(x):
    return jnp.concatenate(list(x), axis=1)


def _conv_branch(dw, g_cn_ref, b_cn_ref, g_oc_ref):
    mu = jnp.mean(dw, axis=-1, keepdims=True)
    xc = dw - mu
    y = xc * lax.rsqrt(jnp.mean(xc * xc, axis=-1, keepdims=True) + EPS) * g_cn_ref[...] + b_cn_ref[...]
    return _rms(y * jax.nn.sigmoid(y), g_oc_ref[...])


def _prompt_proj_kernel(x_ref, q_tab_ref, k_tab_ref, ln_mix_ref, w_in_ref, g_q_ref, w_uq_ref,
                        g_kv_ref, w_uk_ref, w_uvt_ref, v_one_ref, w_dw_ref, b_dw_ref, g_cn_ref, b_cn_ref,
                        g_oc_ref,
                        q_ref, k_ref, vt_ref, ckv_ref, kpe_t_ref, cn_ref, ncv_ref, ubuf, obuf, *, tiles_per_seq):
    rows = x_ref.shape[0]
    q_lora, kv_lora, conv_ch = g_q_ref.shape[1], g_kv_ref.shape[1], g_cn_ref.shape[1]
    hn = _rms(x_ref[...], ln_mix_ref[...]).astype(_BF16)
    w_cq, w_ckv, w_kpe = 0, q_lora, q_lora + kv_lora
    w_a, w_gate = w_kpe + LANES, w_kpe + LANES + conv_ch

    @pl.when(lax.rem(pl.program_id(0), tiles_per_seq) == 0)
    def _():
        ubuf[:, 0:HALO_ROWS, :] = jnp.zeros((ubuf.shape[0], HALO_ROWS, LANES), _F32)

    def glu(c0):
        cols = slice(c0, c0 + GLU_PIECE)
        u = (_dot(hn, w_in_ref[:, w_a + c0:w_a + c0 + GLU_PIECE])
             * jax.nn.sigmoid(_dot(hn, w_in_ref[:, w_gate + c0:w_gate + c0 + GLU_PIECE])))
        ubuf[c0 // LANES:(c0 + GLU_PIECE) // LANES, HALO_ROWS:HALO_ROWS + rows, :] = _split_lanes(u)
        ncv_ref[0, :, cols] = u[rows - CONV_STATE:, :]

    state = {}

    def query_latent():
        cq = _dot(hn, w_in_ref[:, w_cq:w_cq + q_lora])
        state["cq"] = _rms(cq, g_q_ref[...]).astype(_BF16)

    def key_latent():
        ckv = _rms(_dot(hn, w_in_ref[:, w_ckv:w_ckv + kv_lora]), g_kv_ref[...])
        ckv_ref[...] = ckv
        state["ckv"] = ckv.astype(_BF16)
        kpe = _rope(_dot(hn, w_in_ref[:, w_kpe:w_kpe + LANES]), k_tab_ref[...], 0.0)
        kpe_t_ref[0] = kpe.T[ROPE_LO:ROPE_HI, :]
        state["kpe"] = kpe

    def query_heads(h0):
        q = _dot(state["cq"], w_uq_ref[:, h0 * LANES:(h0 + HEADS_PER_PIECE) * LANES])
        for j in range(HEADS_PER_PIECE):
            q_ref[h0 + j] = _rope(q[:, j * LANES:(j + 1) * LANES], q_tab_ref[...], None).astype(q_ref.dtype)

    def key_heads(h0):
        k_nope = _dot(state["ckv"], w_uk_ref[:, h0 * LANES:(h0 + HEADS_PER_PIECE) * LANES])
        for j in range(HEADS_PER_PIECE):
            k_ref[h0 + j] = (k_nope[:, j * LANES:(j + 1) * LANES] + state["kpe"]).astype(k_ref.dtype)

    def values():
        vt = _dot_nt(w_uvt_ref[...], state["ckv"]) + v_one_ref[...]
        key_tile = vt_ref.shape[3]
        for h in range(MLA_HEADS):
            for t in range(rows // key_tile):
                vt_ref[h, t] = vt[h * VT_ROWS:(h + 1) * VT_ROWS,
                                  t * key_tile:(t + 1) * key_tile].astype(vt_ref.dtype)

    stages = [query_latent, key_latent]
    for h0 in range(0, MLA_HEADS, HEADS_PER_PIECE):
        stages += [functools.partial(query_heads, h0), functools.partial(key_heads, h0)]
    stages.append(values)
    glu(0)
    later_glu = [functools.partial(glu, c0) for c0 in range(GLU_PIECE, conv_ch, GLU_PIECE)]
    _interleave(later_glu + stages, _depthwise_conv(ubuf, obuf, w_dw_ref, b_dw_ref))

    cn_ref[...] = _conv_branch(_merge_lanes(obuf[...]), g_cn_ref, b_cn_ref, g_oc_ref).astype(cn_ref.dtype)
    ubuf[:, 0:HALO_ROWS, :] = ubuf[:, rows:HALO_ROWS + rows, :]


def _sample_proj_kernel(x_ref, q_tab_ref, k_tab_ref, state_ref, ln_mix_ref, w_in_ref, g_q_ref,
                        w_uq_ref, g_kv_ref, w_dw_ref, b_dw_ref, g_cn_ref, b_cn_ref, g_oc_ref,
                        q_ref, ckv_ref, kpe_ref, cn_ref, ncv_ref, ubuf, obuf):
    segs = state_ref.shape[0]
    seq = x_ref.shape[0] // segs
    stride = HALO_ROWS + seq
    q_lora, kv_lora, conv_ch = g_q_ref.shape[1], g_kv_ref.shape[1], g_cn_ref.shape[1]
    cq, ckv, kpe, a, gate = _in_proj(x_ref, ln_mix_ref, w_in_ref, (q_lora, kv_lora, LANES, conv_ch, conv_ch))
    _query(cq, g_q_ref, w_uq_ref, q_tab_ref[...], q_ref)
    ckv_ref[...] = _rms(ckv, g_kv_ref[...])
    kpe_ref[...] = _rope(kpe, k_tab_ref[...], 0.0)

    u = a * jax.nn.sigmoid(gate)
    ubuf[...] = jnp.zeros(ubuf.shape, _F32)
    for s in range(segs):
        ubuf[:, s * stride + HALO_PAD:s * stride + HALO_ROWS, :] = _split_lanes(state_ref[s])
        ubuf[:, s * stride + HALO_ROWS:(s + 1) * stride, :] = _split_lanes(u[s * seq:(s + 1) * seq, :])
    for group in _depthwise_conv(ubuf, obuf, w_dw_ref, b_dw_ref):
        group()
    dw = jnp.concatenate([_merge_lanes(obuf[:, s * stride:s * stride + seq, :]) for s in range(segs)], axis=0)
    cn_ref[...] = _conv_branch(dw, g_cn_ref, b_cn_ref, g_oc_ref).astype(cn_ref.dtype)
    for s in range(segs):
        ncv_ref[s] = jnp.concatenate([state_ref[s][seq:, :], u[s * seq:(s + 1) * seq, :]], axis=0)


def _whole(shape):
    zeros = (0,) * len(shape)
    return pl.BlockSpec(shape, lambda *_: zeros)


def _prompt_proj(x, tabs, wts, seq):
    n, d = x.shape
    tm = min(2 * ROW_TILE, seq)
    key_tile = min(ATTN_TILE, seq)
    assert tm % key_tile == 0 and seq % tm == 0 and tm % CONV_BLOCK == 0
    tiles_per_seq = seq // tm
    conv_ch = wts["g_cn"].shape[1]
    kv_lora = wts["g_kv"].shape[1]
    row_block = lambda w: pl.BlockSpec((tm, w), lambda i: (i, 0))
    head_block = pl.BlockSpec((MLA_HEADS, tm, LANES), lambda i: (0, i, 0))
    vt_block = pl.BlockSpec((MLA_HEADS, tm // key_tile, VT_ROWS, key_tile), lambda i: (0, i, 0, 0))
    tab_block = pl.BlockSpec((tm, LANES), lambda i: (lax.rem(i, tiles_per_seq), 0))
    names = ("ln_mix", "w_in", "g_q", "w_uq", "g_kv", "w_uk", "w_uvt", "v_one", "w_dw", "b_dw", "g_cn",
             "b_cn", "g_oc")
    head_shape = jax.ShapeDtypeStruct((MLA_HEADS, n, LANES), _BF16)
    return pl.pallas_call(
        functools.partial(_prompt_proj_kernel, tiles_per_seq=tiles_per_seq),
        grid=(n // tm,),
        in_specs=[row_block(d), tab_block, tab_block] + [_whole(wts[k].shape) for k in names],
        out_specs=(head_block, head_block, vt_block, row_block(kv_lora),
                   pl.BlockSpec((1, ROPE_DIM, tm), lambda i: (i // tiles_per_seq, 0, lax.rem(i, tiles_per_seq))),
                   row_block(conv_ch), pl.BlockSpec((1, CONV_STATE, conv_ch), lambda i: (i // tiles_per_seq, 0, 0))),
        out_shape=(head_shape, head_shape,
                   jax.ShapeDtypeStruct((MLA_HEADS, n // key_tile, VT_ROWS, key_tile), _BF16),
                   jax.ShapeDtypeStruct((n, kv_lora), _F32), jax.ShapeDtypeStruct((n // seq, ROPE_DIM, seq), _F32),
                   jax.ShapeDtypeStruct((n, conv_ch), _BF16),
                   jax.ShapeDtypeStruct((n // seq, CONV_STATE, conv_ch), _F32)),
        scratch_shapes=[pltpu.VMEM((conv_ch // LANES, HALO_ROWS + tm, LANES), _F32),
                        pltpu.VMEM((conv_ch // LANES, tm, LANES), _F32)],
        compiler_params=pltpu.CompilerParams(dimension_semantics=("arbitrary",),
                                             vmem_limit_bytes=VMEM_LIMIT_BYTES),
        name="prompt_proj",
    )(x, *tabs, *[wts[k] for k in names])


def _sample_proj(x, tabs, state, wts, seq):
    n, d = x.shape
    conv_ch = wts["g_cn"].shape[1]
    kv_lora = wts["g_kv"].shape[1]
    names = ("ln_mix", "w_in", "g_q", "w_uq", "g_kv", "w_dw", "b_dw", "g_cn", "b_cn", "g_oc")
    conv_rows = (n // seq) * (HALO_ROWS + seq)
    assert conv_rows % CONV_BLOCK == 0
    return pl.pallas_call(
        _sample_proj_kernel,
        out_shape=(jax.ShapeDtypeStruct((MLA_HEADS, n, LANES), _BF16),
                   jax.ShapeDtypeStruct((n, kv_lora), _F32), jax.ShapeDtypeStruct((n, LANES), _F32),
                   jax.ShapeDtypeStruct((n, conv_ch), _BF16),
                   jax.ShapeDtypeStruct((n // seq, CONV_STATE, conv_ch), _F32)),
        scratch_shapes=[pltpu.VMEM((conv_ch // LANES, conv_rows + HALO_ROWS, LANES), _F32),
                        pltpu.VMEM((conv_ch // LANES, conv_rows, LANES), _F32)],
        compiler_params=pltpu.CompilerParams(vmem_limit_bytes=VMEM_LIMIT_BYTES),
        name="sample_proj",
    )(x, *tabs, state, *[wts[k] for k in names])


def _software_pipeline(items, issue, finish):
    items = list(items)
    issued, done = [], []
    for t in range(len(items) + SCORE_LOOKAHEAD):
        if t < len(items):
            issued.append(issue(items[t]))
        if t >= SCORE_LOOKAHEAD:
            done.append(finish(items[t - SCORE_LOOKAHEAD], issued[t - SCORE_LOOKAHEAD]))
    return done


def _prompt_attn_kernel(q_ref, k_ref, vt_ref, g_om_ref, *refs, sub):
    n_cast = (len(refs) - 1) // 2
    o_ref = refs[n_cast]
    for src_ref, dst_ref in zip(refs[:n_cast], refs[n_cast + 1:]):
        dst_ref[...] = src_ref[...].astype(dst_ref.dtype)
    tile = q_ref.shape[1]
    i = pl.program_id(1)
    chunk_of = lambda t: lax.shift_right_logical(t, CHUNK.bit_length() - 1)

    def key_rows(j):
        return pl.ds(pl.multiple_of(j * tile, tile), tile)

    spans = []
    for r in range(tile // sub):
        n_keys = (r + 1) * sub
        key = lax.broadcasted_iota(jnp.int32, (n_keys, sub), 0)
        query = lax.broadcasted_iota(jnp.int32, (n_keys, sub), 1) + r * sub
        spans.append((slice(r * sub, (r + 1) * sub), n_keys, chunk_of(key) <= chunk_of(query)))

    def diag_scores(h, span):
        q_rows, n_keys, mask = span
        keys = pl.ds(pl.multiple_of(i * tile, tile), n_keys)
        return jnp.where(mask, _dot_nt(k_ref[h, keys, :], q_ref[h, q_rows, :]), MASK_VALUE)

    def diag_finish(h, span, s):
        m = jnp.max(s, axis=0, keepdims=True)
        return m, _dot(vt_ref[h, i, :, 0:span[1]], jnp.exp2(s - m).astype(_BF16))

    items = [(h, span) for h in range(MLA_HEADS) for span in spans]
    done = _software_pipeline(items, lambda it: diag_scores(*it), lambda it, s: diag_finish(*it, s))
    per_head = len(spans)
    maxes = [jnp.concatenate([m for m, _ in done[h * per_head:(h + 1) * per_head]], axis=1)
             for h in range(MLA_HEADS)]
    accs = [jnp.concatenate([a for _, a in done[h * per_head:(h + 1) * per_head]], axis=1)
            for h in range(MLA_HEADS)]

    def body(j, carry):
        state = [list(carry[0]), list(carry[1])]
        part = tile // KEY_PARTS

        def issue(item):
            h, t = item
            rows = pl.ds(pl.multiple_of(j * tile + t * part, part), part)
            return _dot_nt(k_ref[h, rows, :], q_ref[h])

        def finish(item, s):
            h, t = item
            m_old, acc = state[0][h], state[1][h]
            m_new = jnp.maximum(m_old, jnp.max(s, axis=0, keepdims=True))
            p = jnp.exp2(s - m_new).astype(_BF16)
            state[0][h] = m_new
            state[1][h] = acc * jnp.exp2(m_old - m_new) + _dot(vt_ref[h, j, :, t * part:(t + 1) * part], p)

        _software_pipeline([(h, t) for h in range(MLA_HEADS) for t in range(KEY_PARTS)], issue, finish)
        return tuple(state[0]), tuple(state[1])

    _, accs = lax.fori_loop(0, i, body, (tuple(maxes), tuple(accs)))

    o_t = jnp.concatenate([acc[0:V_DIM] / acc[V_DIM:V_DIM + 1] for acc in accs], axis=0)
    o_ref[...] = _rms(o_t.T, g_om_ref[...]).astype(o_ref.dtype)


def _cast_block_rows(rows, steps):
    block = -(-rows // (steps * CAST_ROW_ALIGN)) * CAST_ROW_ALIGN
    while rows % block:
        block += CAST_ROW_ALIGN
    return block


def _prompt_attn(q, k, vt, g_om, to_cast, batch, seq):
    tile = vt.shape[3]
    nq = seq // tile
    width = g_om.shape[1]
    cast_specs = []
    for w in to_cast:
        block = _cast_block_rows(w.shape[0], batch * nq)
        cast_specs.append(pl.BlockSpec(
            (block, w.shape[1]),
            lambda b, i, last=w.shape[0] // block - 1: (jnp.minimum(b * nq + i, last), 0)))
    outs = pl.pallas_call(
        functools.partial(_prompt_attn_kernel, sub=min(ATTN_DIAG_TILE, tile)),
        grid=(batch, nq),
        in_specs=[pl.BlockSpec((MLA_HEADS, tile, LANES), lambda b, i: (0, b * nq + i, 0)),
                  pl.BlockSpec((MLA_HEADS, seq, LANES), lambda b, i: (0, b, 0)),
                  pl.BlockSpec((MLA_HEADS, nq, VT_ROWS, tile), lambda b, i: (0, b, 0, 0)),
                  _whole(g_om.shape)] + cast_specs,
        out_specs=[pl.BlockSpec((tile, width), lambda b, i: (b * nq + i, 0))] + cast_specs,
        out_shape=[jax.ShapeDtypeStruct((batch * seq, width), _BF16)]
                  + [jax.ShapeDtypeStruct(w.shape, _BF16) for w in to_cast],
        compiler_params=pltpu.CompilerParams(dimension_semantics=("arbitrary", "arbitrary"),
                                             vmem_limit_bytes=VMEM_LIMIT_BYTES),
        name="prompt_attn",
    )(q, k, vt, g_om, *to_cast)
    return outs[0], outs[1:]


def _sample_attn_kernel(q_ref, ckv_new_ref, kpe_new_ref, ckv_past_ref, kpe_past_t_ref, w_ukt_ref, w_uvh_ref,
                        g_om_ref, o_ref):
    seq = q_ref.shape[1]
    q_all = jnp.concatenate([q_ref[h] for h in range(MLA_HEADS)], axis=0)
    q_lat = jnp.concatenate([_dot(q_ref[h], w_ukt_ref[h]) for h in range(MLA_HEADS)], axis=0).astype(_BF16)
    q_pe = q_all[:, ROPE_LO:ROPE_HI]
    c_past = ckv_past_ref[0].astype(_BF16)
    c_new = ckv_new_ref[...].astype(_BF16)
    s_past = _dot_nt(q_lat, c_past) + _dot(q_pe, kpe_past_t_ref[0].astype(_BF16))
    s_new = _dot_nt(q_lat, c_new) + _dot_nt(q_all, kpe_new_ref[...].astype(_BF16))
    m = jnp.maximum(jnp.max(s_past, axis=-1, keepdims=True), jnp.max(s_new, axis=-1, keepdims=True))
    p_past = jnp.exp2(s_past - m)
    p_new = jnp.exp2(s_new - m)
    denom = jnp.sum(p_past, axis=-1, keepdims=True) + jnp.sum(p_new, axis=-1, keepdims=True)
    o_lat = ((_dot(p_past.astype(_BF16), c_past) + _dot(p_new.astype(_BF16), c_new)) / denom).astype(_BF16)
    o = sum(_dot(o_lat[h * seq:(h + 1) * seq], w_uvh_ref[h]) for h in range(MLA_HEADS))
    o_ref[...] = _rms(o, g_om_ref[...]).astype(o_ref.dtype)


def _sample_attn(q, ckv_new, kpe_new, ckv_past, kpe_past_t, w_ukt, w_uvh, g_om, batch, seq):
    past, kv_lora = ckv_past.shape[1:]
    width = g_om.shape[1]
    return pl.pallas_call(
        _sample_attn_kernel,
        grid=(batch,),
        in_specs=[pl.BlockSpec((MLA_HEADS, seq, LANES), lambda b: (0, b, 0)),
                  pl.BlockSpec((seq, kv_lora), lambda b: (b, 0)),
                  pl.BlockSpec((seq, LANES), lambda b: (b, 0)),
                  pl.BlockSpec((1, past, kv_lora), lambda b: (b, 0, 0)),
                  pl.BlockSpec((1, ROPE_DIM, past), lambda b: (b, 0, 0)),
                  _whole(w_ukt.shape), _whole(w_uvh.shape), _whole(g_om.shape)],
        out_specs=pl.BlockSpec((seq, width), lambda b: (b, 0)),
        out_shape=jax.ShapeDtypeStruct((batch * seq, width), _BF16),
        compiler_params=pltpu.CompilerParams(dimension_semantics=("arbitrary",),
                                             vmem_limit_bytes=VMEM_LIMIT_BYTES),
        name="sample_attn",
    )(q, ckv_new, kpe_new, ckv_past, kpe_past_t, w_ukt, w_uvh, g_om)


def _output_kernel(x_ref, an_ref, cn_ref, w_out_a_ref, w_out_c_ref, ln_ffn_ref, w_gate_ref, w_up_ref,
                   w_down_ref, g_final_ref, y_ref):
    h = x_ref[...] + _dot(an_ref[...], w_out_a_ref[...]) + _dot(cn_ref[...], w_out_c_ref[...])
    f = _rms(h, ln_ffn_ref[...]).astype(_BF16)
    gate = _dot(f, w_gate_ref[...])
    act = (gate * jax.nn.sigmoid(gate) * _dot(f, w_up_ref[...])).astype(_BF16)
    h = h + _dot(act, w_down_ref[...])
    y_ref[...] = _rms(h, g_final_ref[...])


def _output(x, an, cn, wts, name):
    n, d = x.shape
    tm = min(ROW_TILE, n)
    names = ("w_out_a", "w_out_c", "ln_ffn", "w_gate", "w_up", "w_down", "g_final")
    row_block = lambda w: pl.BlockSpec((tm, w), lambda i: (i, 0))
    resident = lambda shape: pl.BlockSpec(shape, lambda i: (0,) * len(shape), pipeline_mode=pl.Buffered(1))
    return pl.pallas_call(
        _output_kernel,
        grid=(n // tm,),
        in_specs=[row_block(d), row_block(an.shape[1]), row_block(cn.shape[1])]
                 + [resident(wts[k].shape) for k in names],
        out_specs=row_block(d),
        out_shape=jax.ShapeDtypeStruct((n, d), _F32),
        compiler_params=pltpu.CompilerParams(dimension_semantics=("arbitrary",),
                                             vmem_limit_bytes=VMEM_LIMIT_BYTES),
        name=name,
    )(x, an, cn, *[wts[k] for k in names])


def _pad_lanes(w, left):
    return jnp.pad(w, [(0, 0)] * (w.ndim - 1) + [(left, LANES - left - w.shape[-1])])


def _prepare_weights(ln_mix, w_in, g_q, w_uq, g_kv, w_uk, w_uv, w_dw, b_dw, g_cn, b_cn, g_om, g_oc, w_out,
                     ln_ffn, g_final):
    q_lora, kv_lora = g_q.shape[0], g_kv.shape[0]
    mla_width = g_om.shape[0]
    row = lambda v: v.reshape(1, -1)
    c1, c2 = q_lora + kv_lora, q_lora + kv_lora + ROPE_DIM
    with_swap = lambda w: jnp.concatenate([w, w[..., HALF_ROPE:], w[..., :HALF_ROPE]], axis=-1)
    w_in_p = jnp.concatenate([w_in[:, :c1], _pad_lanes(with_swap(w_in[:, c1:c2]), ROPE_LO), w_in[:, c2:]], axis=1)
    w_uq_h = w_uq.reshape(q_lora, MLA_HEADS, NOPE_DIM + ROPE_DIM)
    w_uq_p = jnp.concatenate([w_uq_h[..., :NOPE_DIM], with_swap(w_uq_h[..., NOPE_DIM:])], axis=-1)
    w_uq_p = w_uq_p.reshape(q_lora, -1)
    w_uk_p = _pad_lanes(w_uk, 0).reshape(kv_lora, -1)
    w_uvt = jnp.pad(jnp.transpose(w_uv, (1, 2, 0)), ((0, 0), (0, VT_ROWS - V_DIM), (0, 0)))
    v_one = jnp.zeros((MLA_HEADS, VT_ROWS, 1), _F32).at[:, V_DIM].set(1.0)
    w_ukt = jnp.pad(jnp.transpose(w_uk, (1, 2, 0)), ((0, 0), (0, LANES - NOPE_DIM), (0, 0)))
    w_uvh = jnp.stack([jnp.pad(w_uv[:, h], ((0, 0), (h * V_DIM, (MLA_HEADS - 1 - h) * V_DIM)))
                       for h in range(MLA_HEADS)])
    return {
        "ln_mix": row(ln_mix), "w_in": w_in_p.astype(_BF16), "g_q": row(g_q), "w_uq": w_uq_p.astype(_BF16),
        "g_kv": row(g_kv), "w_uk": w_uk_p.astype(_BF16), "w_uvt": w_uvt.reshape(-1, kv_lora).astype(_BF16),
        "v_one": v_one.reshape(-1, 1), "w_dw": w_dw, "b_dw": row(b_dw), "g_cn": row(g_cn), "b_cn": row(b_cn),
        "g_om": row(g_om), "g_oc": row(g_oc),
        "w_ukt": w_ukt.astype(_BF16), "w_uvh": w_uvh.astype(_BF16),
        "w_out_a": w_out[:mla_width].astype(_BF16), "w_out_c": w_out[mla_width:].astype(_BF16),
        "ln_ffn": row(ln_ffn), "g_final": row(g_final),
    }


def _layer(x_prompt, x_sample, ckv_past, kpe_past, conv_past, wts, ffn_f32):
    batch, seq, d = x_prompt.shape
    dec_batch, dec_seq, _ = x_sample.shape
    past = ckv_past.shape[1]

    xp = x_prompt.reshape(batch * seq, d)
    q, k, vt, kv_p, kr_p_t, cn, cv_p = _prompt_proj(xp, _rope_tables(0, seq, 1), wts, seq)
    an, ffn_bf16 = _prompt_attn(q, k, vt, wts["g_om"], list(ffn_f32.values()), batch, seq)
    wts = dict(wts, **dict(zip(ffn_f32, ffn_bf16)))
    y_p = _output(xp, an, cn, wts, "prompt_output")

    xs = x_sample.reshape(dec_batch * dec_seq, d)
    tabs = _rope_tables(past, dec_seq, dec_batch)
    q, kv_s, kr_s, cn, cv_s = _sample_proj(xs, tabs, conv_past, wts, dec_seq)
    an = _sample_attn(q, kv_s, kr_s, ckv_past, jnp.swapaxes(kpe_past, 1, 2), wts["w_ukt"], wts["w_uvh"],
                      wts["g_om"], dec_batch, dec_seq)
    y_s = _output(xs, an, cn, wts, "sample_output")

    return (y_p.reshape(batch, seq, d), y_s.reshape(dec_batch, dec_seq, d),
            kv_p.reshape(batch, seq, -1), jnp.swapaxes(kr_p_t, 1, 2), cv_p,
            kv_s.reshape(dec_batch, dec_seq, -1),
            kr_s[:, ROPE_LO:ROPE_HI].reshape(dec_batch, dec_seq, ROPE_DIM), cv_s)


def kernel(x_prompt, x_sample, cache_kv_latent, cache_k_rope, state_conv, ln_mix, w_in, g_q, w_uq, g_kv, w_uk, w_uv, w_dw, b_dw, g_cn, b_cn, g_om, g_oc, w_out, ln_ffn, w_gate, w_up, w_down, g_final):
    depth = w_in.shape[0]
    assert depth == 1, "the kernel implements the single-layer model of the problem"
    wts = _prepare_weights(ln_mix[0], w_in[0], g_q[0], w_uq[0], g_kv[0], w_uk[0], w_uv[0], w_dw[0], b_dw[0],
                           g_cn[0], b_cn[0], g_om[0], g_oc[0], w_out[0], ln_ffn[0], g_final)
    ffn_f32 = {"w_gate": w_gate[0], "w_up": w_up[0], "w_down": w_down[0]}
    outs = _layer(x_prompt, x_sample, cache_kv_latent[0], cache_k_rope[0], state_conv[0], wts, ffn_f32)
    y_p, y_s = outs[0], outs[1]
    return (y_p, y_s) + tuple(o[None] for o in outs[2:])
```

```python
import functools
import math

import jax
import jax.numpy as jnp
from jax import lax
from jax.experimental import pallas as pl
from jax.experimental.pallas import tpu as pltpu

CHUNK = 64
MLA_HEADS = 8
NOPE_DIM = 64
ROPE_DIM = 32
V_DIM = 64
ROPE_THETA = 10000.0
EPS = 1e-6
CONV_W = 31
CONV_STATE = CONV_W - 1
ATTN_SCALE = 1.0 / math.sqrt(NOPE_DIM + ROPE_DIM)
MASK_VALUE = -1e30

LANES = 128
SUBLANES = 8
HALF_ROPE = ROPE_DIM // 2
ROPE_LO = NOPE_DIM
ROPE_HI = ROPE_LO + ROPE_DIM
assert ROPE_HI + ROPE_DIM == LANES
VT_ROWS = 80
HALO_ROWS = 32
HALO_PAD = HALO_ROWS - CONV_STATE
CONV_STRIDE = 4
CONV_BLOCK = SUBLANES * CONV_STRIDE
Q_SCALE = ATTN_SCALE * math.log2(math.e)

ROPE_FINE = 32
GLU_PIECE = 256
HEADS_PER_PIECE = 2
ROW_TILE = 512
ATTN_TILE = 1024
KEY_PARTS = 4
SCORE_LOOKAHEAD = 6
STREAM_FFN_CHUNK = 256
SAMPLE_KEY_CHUNKS = 4
ATTN_DIAG_TILE = 256
CAST_ROW_ALIGN = 32
VMEM_LIMIT_BYTES = 56 * 1024 * 1024

_BF16 = jnp.bfloat16
_F32 = jnp.float32


def _rms(x, g):
    return x * lax.rsqrt(jnp.mean(x * x, axis=-1, keepdims=True) + EPS) * g


def _dot(a, b):
    return jnp.dot(a, b, preferred_element_type=_F32)


def _dot_nt(a, b):
    return lax.dot_general(a, b, (((1,), (1,)), ((), ())), preferred_element_type=_F32)


def _rope(x, table, rest):
    y = x * table
    lane = lax.broadcasted_iota(jnp.int32, x.shape, 1)
    in_rope = (lane >= ROPE_LO) & (lane < ROPE_HI)
    return jnp.where(in_rope, y + pltpu.roll(y, LANES - ROPE_DIM, 1), y if rest is None else rest)


def _rope_table_kernel(inv_ref, q_tab_ref, k_tab_ref, *, base, count, fine):
    inv = inv_ref[...].reshape(1, 1, LANES)
    coarse = base + fine * lax.broadcasted_iota(jnp.int32, (count // fine, 1, LANES), 0)
    offset = lax.broadcasted_iota(jnp.int32, (1, fine, LANES), 1)
    ang_a, ang_b = coarse.astype(_F32) * inv, offset.astype(_F32) * inv
    ca, sa, cb, sb = jnp.cos(ang_a), jnp.sin(ang_a), jnp.cos(ang_b), jnp.sin(ang_b)
    c = (ca * cb - sa * sb).reshape(count, LANES)
    s = (sa * cb + ca * sb).reshape(count, LANES)
    lane = lax.broadcasted_iota(jnp.int32, (count, LANES), 1)
    rot = jnp.where(lane < ROPE_HI, c, jnp.where(lane < ROPE_HI + HALF_ROPE, -s, s))
    q_tab = jnp.where(lane < ROPE_LO, Q_SCALE, rot * Q_SCALE)
    for r in range(k_tab_ref.shape[0] // count):
        k_tab_ref[r * count:(r + 1) * count, :] = rot
        q_tab_ref[r * count:(r + 1) * count, :] = q_tab


def _rope_tables(base, count, repeat):
    inv = 1.0 / (ROPE_THETA ** (jnp.arange(0, ROPE_DIM, 2, dtype=_F32) / ROPE_DIM))
    inv_lanes = jnp.zeros((1, LANES), _F32).at[0, ROPE_LO:].set(jnp.tile(inv, 4))
    fine = math.gcd(count, ROPE_FINE)
    assert fine % SUBLANES == 0
    out = jax.ShapeDtypeStruct((count * repeat, LANES), _F32)
    return pl.pallas_call(
        functools.partial(_rope_table_kernel, base=base, count=count, fine=fine),
        out_shape=(out, out),
        name="rope_table",
    )(inv_lanes)


def _in_proj(x_ref, ln_mix_ref, w_in_ref, widths):
    hn = _rms(x_ref[...], ln_mix_ref[...]).astype(_BF16)
    outs, start = [], 0
    for w in widths:
        outs.append(_dot(hn, w_in_ref[:, start:start + w]))
        start += w
    return outs


def _query(cq, g_q_ref, w_uq_ref, q_tab, q_ref):
    q = _dot(_rms(cq, g_q_ref[...]).astype(_BF16), w_uq_ref[...])
    for h in range(MLA_HEADS):
        q_ref[h] = _rope(q[:, h * LANES:(h + 1) * LANES], q_tab, None).astype(q_ref.dtype)


def _depthwise_conv(ubuf, obuf, w_dw_ref, b_dw_ref):
    groups, n, _ = obuf.shape

    def block(g, base):
        lanes = slice(g * LANES, (g + 1) * LANES)
        taps = [ubuf[g, pl.ds(base + HALO_PAD + s, SUBLANES, stride=CONV_STRIDE), :]
                for s in range(CONV_STRIDE + CONV_W - 1)]
        for c in range(CONV_STRIDE):
            acc = jnp.broadcast_to(b_dw_ref[:, lanes], (SUBLANES, LANES))
            for k in range(CONV_W):
                acc = acc + taps[c + k] * w_dw_ref[k:k + 1, lanes]
            obuf[g, pl.ds(base + c, SUBLANES, stride=CONV_STRIDE), :] = acc

    return [functools.partial(block, g, base) for g in range(groups) for base in range(0, n, CONV_BLOCK)]


def _interleave(stages, fillers):
    fillers = list(fillers)
    share = -(-len(fillers) // len(stages))
    for stage in stages:
        stage()
        for filler in fillers[:share]:
            filler()
        fillers = fillers[share:]


def _split_lanes(x):
    return jnp.stack([x[:, c:c + LANES] for c in range(0, x.shape[1], LANES)])


def _merge_lanes(x):
    return jnp.concatenate(list(x), axis=1)


def _conv_branch(dw, g_cn_ref, b_cn_ref, g_oc_ref):
    mu = jnp.mean(dw, axis=-1, keepdims=True)
    xc = dw - mu
    y = xc * lax.rsqrt(jnp.mean(xc * xc, axis=-1, keepdims=True) + EPS) * g_cn_ref[...] + b_cn_ref[...]
    return _rms(y * jax.nn.sigmoid(y), g_oc_ref[...])


def _prompt_proj_kernel(x_ref, q_tab_ref, k_tab_ref, ln_mix_ref, w_in_ref, g_q_ref, w_uq_ref,
                        g_kv_ref, w_uk_ref, w_uvt_ref, v_one_ref, w_dw_ref, b_dw_ref, g_cn_ref, b_cn_ref,
                        g_oc_ref,
                        q_ref, k_ref, vt_ref, ckv_ref, kpe_t_ref, cn_ref, ncv_ref, ubuf, obuf, *, tiles_per_seq):
    rows = x_ref.shape[0]
    q_lora, kv_lora, conv_ch = g_q_ref.shape[1], g_kv_ref.shape[1], g_cn_ref.shape[1]
    hn = _rms(x_ref[...], ln_mix_ref[...]).astype(_BF16)
    w_cq, w_ckv, w_kpe = 0, q_lora, q_lora + kv_lora
    w_a, w_gate = w_kpe + LANES, w_kpe + LANES + conv_ch

    @pl.when(lax.rem(pl.program_id(0), tiles_per_seq) == 0)
    def _():
        ubuf[:, 0:HALO_ROWS, :] = jnp.zeros((ubuf.shape[0], HALO_ROWS, LANES), _F32)

    def glu(c0):
        cols = slice(c0, c0 + GLU_PIECE)
        u = (_dot(hn, w_in_ref[:, w_a + c0:w_a + c0 + GLU_PIECE])
             * jax.nn.sigmoid(_dot(hn, w_in_ref[:, w_gate + c0:w_gate + c0 + GLU_PIECE])))
        ubuf[c0 // LANES:(c0 + GLU_PIECE) // LANES, HALO_ROWS:HALO_ROWS + rows, :] = _split_lanes(u)
        ncv_ref[0, :, cols] = u[rows - CONV_STATE:, :]

    state = {}

    def query_latent():
        cq = _dot(hn, w_in_ref[:, w_cq:w_cq + q_lora])
        state["cq"] = _rms(cq, g_q_ref[...]).astype(_BF16)

    def key_latent():
        ckv = _rms(_dot(hn, w_in_ref[:, w_ckv:w_ckv + kv_lora]), g_kv_ref[...])
        ckv_ref[...] = ckv
        state["ckv"] = ckv.astype(_BF16)
        kpe = _rope(_dot(hn, w_in_ref[:, w_kpe:w_kpe + LANES]), k_tab_ref[...], 0.0)
        kpe_t_ref[0] = kpe.T[ROPE_LO:ROPE_HI, :]
        state["kpe"] = kpe

    def query_heads(h0):
        q = _dot(state["cq"], w_uq_ref[:, h0 * LANES:(h0 + HEADS_PER_PIECE) * LANES])
        for j in range(HEADS_PER_PIECE):
            q_ref[h0 + j] = _rope(q[:, j * LANES:(j + 1) * LANES], q_tab_ref[...], None).astype(q_ref.dtype)

    def key_heads(h0):
        k_nope = _dot(state["ckv"], w_uk_ref[:, h0 * LANES:(h0 + HEADS_PER_PIECE) * LANES])
        for j in range(HEADS_PER_PIECE):
            k_ref[h0 + j] = (k_nope[:, j * LANES:(j + 1) * LANES] + state["kpe"]).astype(k_ref.dtype)

    def values():
        vt = _dot_nt(w_uvt_ref[...], state["ckv"]) + v_one_ref[...]
        key_tile = vt_ref.shape[3]
        for h in range(MLA_HEADS):
            for t in range(rows // key_tile):
                vt_ref[h, t] = vt[h * VT_ROWS:(h + 1) * VT_ROWS,
                                  t * key_tile:(t + 1) * key_tile].astype(vt_ref.dtype)

    stages = [query_latent, key_latent]
    for h0 in range(0, MLA_HEADS, HEADS_PER_PIECE):
        stages += [functools.partial(query_heads, h0), functools.partial(key_heads, h0)]
    stages.append(values)
    glu(0)
    later_glu = [functools.partial(glu, c0) for c0 in range(GLU_PIECE, conv_ch, GLU_PIECE)]
    _interleave(later_glu + stages, _depthwise_conv(ubuf, obuf, w_dw_ref, b_dw_ref))

    cn_ref[...] = _conv_branch(_merge_lanes(obuf[...]), g_cn_ref, b_cn_ref, g_oc_ref).astype(cn_ref.dtype)
    ubuf[:, 0:HALO_ROWS, :] = ubuf[:, rows:HALO_ROWS + rows, :]


def _sample_proj_kernel(x_ref, q_tab_ref, k_tab_ref, state_ref, ln_mix_ref, w_in_ref, g_q_ref,
                        w_uq_ref, g_kv_ref, w_dw_ref, b_dw_ref, g_cn_ref, b_cn_ref, g_oc_ref,
                        q_ref, ckv_ref, kpe_ref, cn_ref, ncv_ref, ubuf, obuf):
    segs = state_ref.shape[0]
    seq = x_ref.shape[0] // segs
    stride = HALO_ROWS + seq
    q_lora, kv_lora, conv_ch = g_q_ref.shape[1], g_kv_ref.shape[1], g_cn_ref.shape[1]
    cq, ckv, kpe, a, gate = _in_proj(x_ref, ln_mix_ref, w_in_ref, (q_lora, kv_lora, LANES, conv_ch, conv_ch))
    _query(cq, g_q_ref, w_uq_ref, q_tab_ref[...], q_ref)
    ckv_ref[...] = _rms(ckv, g_kv_ref[...])
    kpe_ref[...] = _rope(kpe, k_tab_ref[...], 0.0)

    u = a * jax.nn.sigmoid(gate)
    ubuf[...] = jnp.zeros(ubuf.shape, _F32)
    for s in range(segs):
        ubuf[:, s * stride + HALO_PAD:s * stride + HALO_ROWS, :] = _split_lanes(state_ref[s])
        ubuf[:, s * stride + HALO_ROWS:(s + 1) * stride, :] = _split_lanes(u[s * seq:(s + 1) * seq, :])
    for group in _depthwise_conv(ubuf, obuf, w_dw_ref, b_dw_ref):
        group()
    dw = jnp.concatenate([_merge_lanes(obuf[:, s * stride:s * stride + seq, :]) for s in range(segs)], axis=0)
    cn_ref[...] = _conv_branch(dw, g_cn_ref, b_cn_ref, g_oc_ref).astype(cn_ref.dtype)
    for s in range(segs):
        ncv_ref[s] = jnp.concatenate([state_ref[s][seq:, :], u[s * seq:(s + 1) * seq, :]], axis=0)


def _whole(shape):
    zeros = (0,) * len(shape)
    return pl.BlockSpec(shape, lambda *_: zeros)


def _prompt_proj(x, tabs, wts, seq):
    n, d = x.shape
    tm = min(2 * ROW_TILE, seq)
    key_tile = min(ATTN_TILE, seq)
    assert tm % key_tile == 0 and seq % tm == 0 and tm % CONV_BLOCK == 0
    tiles_per_seq = seq // tm
    conv_ch = wts["g_cn"].shape[1]
    kv_lora = wts["g_kv"].shape[1]
    row_block = lambda w: pl.BlockSpec((tm, w), lambda i: (i, 0))
    head_block = pl.BlockSpec((MLA_HEADS, tm, LANES), lambda i: (0, i, 0))
    vt_block = pl.BlockSpec((MLA_HEADS, tm // key_tile, VT_ROWS, key_tile), lambda i: (0, i, 0, 0))
    tab_block = pl.BlockSpec((tm, LANES), lambda i: (lax.rem(i, tiles_per_seq), 0))
    names = ("ln_mix", "w_in", "g_q", "w_uq", "g_kv", "w_uk", "w_uvt", "v_one", "w_dw", "b_dw", "g_cn",
             "b_cn", "g_oc")
    head_shape = jax.ShapeDtypeStruct((MLA_HEADS, n, LANES), _BF16)
    return pl.pallas_call(
        functools.partial(_prompt_proj_kernel, tiles_per_seq=tiles_per_seq),
        grid=(n // tm,),
        in_specs=[row_block(d), tab_block, tab_block] + [_whole(wts[k].shape) for k in names],
        out_specs=(head_block, head_block, vt_block, row_block(kv_lora),
                   pl.BlockSpec((1, ROPE_DIM, tm), lambda i: (i // tiles_per_seq, 0, lax.rem(i, tiles_per_seq))),
                   row_block(conv_ch), pl.BlockSpec((1, CONV_STATE, conv_ch), lambda i: (i // tiles_per_seq, 0, 0))),
        out_shape=(head_shape, head_shape,
                   jax.ShapeDtypeStruct((MLA_HEADS, n // key_tile, VT_ROWS, key_tile), _BF16),
                   jax.ShapeDtypeStruct((n, kv_lora), _F32), jax.ShapeDtypeStruct((n // seq, ROPE_DIM, seq), _F32),
                   jax.ShapeDtypeStruct((n, conv_ch), _BF16),
                   jax.ShapeDtypeStruct((n // seq, CONV_STATE, conv_ch), _F32)),
        scratch_shapes=[pltpu.VMEM((conv_ch // LANES, HALO_ROWS + tm, LANES), _F32),
                        pltpu.VMEM((conv_ch // LANES, tm, LANES), _F32)],
        compiler_params=pltpu.CompilerParams(dimension_semantics=("arbitrary",),
                                             vmem_limit_bytes=VMEM_LIMIT_BYTES),
        name="prompt_proj",
    )(x, *tabs, *[wts[k] for k in names])


def _sample_proj(x, tabs, state, wts, seq):
    n, d = x.shape
    conv_ch = wts["g_cn"].shape[1]
    kv_lora = wts["g_kv"].shape[1]
    names = ("ln_mix", "w_in", "g_q", "w_uq", "g_kv", "w_dw", "b_dw", "g_cn", "b_cn", "g_oc")
    conv_rows = (n // seq) * (HALO_ROWS + seq)
    assert conv_rows % CONV_BLOCK == 0
    return pl.pallas_call(
        _sample_proj_kernel,
        out_shape=(jax.ShapeDtypeStruct((MLA_HEADS, n, LANES), _BF16),
                   jax.ShapeDtypeStruct((n, kv_lora), _F32), jax.ShapeDtypeStruct((n, LANES), _F32),
                   jax.ShapeDtypeStruct((n, conv_ch), _BF16),
                   jax.ShapeDtypeStruct((n // seq, CONV_STATE, conv_ch), _F32)),
        scratch_shapes=[pltpu.VMEM((conv_ch // LANES, conv_rows + HALO_ROWS, LANES), _F32),
                        pltpu.VMEM((conv_ch // LANES, conv_rows, LANES), _F32)],
        compiler_params=pltpu.CompilerParams(vmem_limit_bytes=VMEM_LIMIT_BYTES),
        name="sample_proj",
    )(x, *tabs, state, *[wts[k] for k in names])


def _software_pipeline(items, issue, finish):
    items = list(items)
    issued, done = [], []
    for t in range(len(items) + SCORE_LOOKAHEAD):
        if t < len(items):
            issued.append(issue(items[t]))
        if t >= SCORE_LOOKAHEAD:
            done.append(finish(items[t - SCORE_LOOKAHEAD], issued[t - SCORE_LOOKAHEAD]))
    return done


def _prompt_attn_kernel(q_ref, k_ref, vt_ref, g_om_ref, *refs, sub):
    n_cast = (len(refs) - 1) // 2
    o_ref = refs[n_cast]
    for src_ref, dst_ref in zip(refs[:n_cast], refs[n_cast + 1:]):
        dst_ref[...] = src_ref[...].astype(dst_ref.dtype)
    tile = q_ref.shape[1]
    i = pl.program_id(1)
    chunk_of = lambda t: lax.shift_right_logical(t, CHUNK.bit_length() - 1)

    def key_rows(j):
        return pl.ds(pl.multiple_of(j * tile, tile), tile)

    spans = []
    for r in range(tile // sub):
        n_keys = (r + 1) * sub
        key = lax.broadcasted_iota(jnp.int32, (n_keys, sub), 0)
        query = lax.broadcasted_iota(jnp.int32, (n_keys, sub), 1) + r * sub
        spans.append((slice(r * sub, (r + 1) * sub), n_keys, chunk_of(key) <= chunk_of(query)))

    def diag_scores(h, span):
        q_rows, n_keys, mask = span
        keys = pl.ds(pl.multiple_of(i * tile, tile), n_keys)
        return jnp.where(mask, _dot_nt(k_ref[h, keys, :], q_ref[h, q_rows, :]), MASK_VALUE)

    def diag_finish(h, span, s):
        m = jnp.max(s, axis=0, keepdims=True)
        return m, _dot(vt_ref[h, i, :, 0:span[1]], jnp.exp2(s - m).astype(_BF16))

    items = [(h, span) for h in range(MLA_HEADS) for span in spans]
    done = _software_pipeline(items, lambda it: diag_scores(*it), lambda it, s: diag_finish(*it, s))
    per_head = len(spans)
    maxes = [jnp.concatenate([m for m, _ in done[h * per_head:(h + 1) * per_head]], axis=1)
             for h in range(MLA_HEADS)]
    accs = [jnp.concatenate([a for _, a in done[h * per_head:(h + 1) * per_head]], axis=1)
            for h in range(MLA_HEADS)]

    def body(j, carry):
        state = [list(carry[0]), list(carry[1])]
        part = tile // KEY_PARTS

        def issue(item):
            h, t = item
            rows = pl.ds(pl.multiple_of(j * tile + t * part, part), part)
            return _dot_nt(k_ref[h, rows, :], q_ref[h])

        def finish(item, s):
            h, t = item
            m_old, acc = state[0][h], state[1][h]
            m_new = jnp.maximum(m_old, jnp.max(s, axis=0, keepdims=True))
            p = jnp.exp2(s - m_new).astype(_BF16)
            state[0][h] = m_new
            state[1][h] = acc * jnp.exp2(m_old - m_new) + _dot(vt_ref[h, j, :, t * part:(t + 1) * part], p)

        _software_pipeline([(h, t) for h in range(MLA_HEADS) for t in range(KEY_PARTS)], issue, finish)
        return tuple(state[0]), tuple(state[1])

    _, accs = lax.fori_loop(0, i, body, (tuple(maxes), tuple(accs)))

    o_t = jnp.concatenate([acc[0:V_DIM] / acc[V_DIM:V_DIM + 1] for acc in accs], axis=0)
    o_ref[...] = _rms(o_t.T, g_om_ref[...]).astype(o_ref.dtype)


def _cast_block_rows(rows, steps):
    block = -(-rows // (steps * CAST_ROW_ALIGN)) * CAST_ROW_ALIGN
    while rows % block:
        block += CAST_ROW_ALIGN
    return block


def _prompt_attn(q, k, vt, g_om, to_cast, batch, seq):
    tile = vt.shape[3]
    nq = seq // tile
    width = g_om.shape[1]
    cast_specs = []
    for w in to_cast:
        block = _cast_block_rows(w.shape[0], batch * nq)
        cast_specs.append(pl.BlockSpec(
            (block, w.shape[1]),
            lambda b, i, last=w.shape[0] // block - 1: (jnp.minimum(b * nq + i, last), 0)))
    outs = pl.pallas_call(
        functools.partial(_prompt_attn_kernel, sub=min(ATTN_DIAG_TILE, tile)),
        grid=(batch, nq),
        in_specs=[pl.BlockSpec((MLA_HEADS, tile, LANES), lambda b, i: (0, b * nq + i, 0)),
                  pl.BlockSpec((MLA_HEADS, seq, LANES), lambda b, i: (0, b, 0)),
                  pl.BlockSpec((MLA_HEADS, nq, VT_ROWS, tile), lambda b, i: (0, b, 0, 0)),
                  _whole(g_om.shape)] + cast_specs,
        out_specs=[pl.BlockSpec((tile, width), lambda b, i: (b * nq + i, 0))] + cast_specs,
        out_shape=[jax.ShapeDtypeStruct((batch * seq, width), _BF16)]
                  + [jax.ShapeDtypeStruct(w.shape, _BF16) for w in to_cast],
        compiler_params=pltpu.CompilerParams(dimension_semantics=("arbitrary", "arbitrary"),
                                             vmem_limit_bytes=VMEM_LIMIT_BYTES),
        name="prompt_attn",
    )(q, k, vt, g_om, *to_cast)
    return outs[0], outs[1:]


def _sample_attn_kernel(q_ref, ckv_new_ref, kpe_new_ref, ckv_past_ref, kpe_past_t_ref, w_ukt_ref, w_uvh_ref,
                        g_om_ref, o_ref):
    seq = q_ref.shape[1]
    q_all = jnp.concatenate([q_ref[h] for h in range(MLA_HEADS)], axis=0)
    q_lat = jnp.concatenate([_dot(q_ref[h], w_ukt_ref[h]) for h in range(MLA_HEADS)], axis=0).astype(_BF16)
    q_pe = q_all[:, ROPE_LO:ROPE_HI]
    past = ckv_past_ref.shape[1]
    chunk = past // SAMPLE_KEY_CHUNKS

    def issue(t):
        if t == SAMPLE_KEY_CHUNKS:
            c = ckv_new_ref[...].astype(_BF16)
            return c, _dot_nt(q_lat, c) + _dot_nt(q_all, kpe_new_ref[...].astype(_BF16))
        rows = slice(t * chunk, (t + 1) * chunk)
        c = ckv_past_ref[0, rows, :].astype(_BF16)
        return c, _dot_nt(q_lat, c) + _dot(q_pe, kpe_past_t_ref[0, :, rows].astype(_BF16))

    state = {}

    def finish(t, issued):
        c, s = issued
        tile_max = jnp.max(s, axis=-1, keepdims=True)
        if not state:
            m_new = tile_max
            p = jnp.exp2(s - m_new)
            state["l"] = jnp.sum(p, axis=-1, keepdims=True)
            state["acc"] = _dot(p.astype(_BF16), c)
        else:
            m_new = jnp.maximum(state["m"], tile_max)
            alpha = jnp.exp2(state["m"] - m_new)
            p = jnp.exp2(s - m_new)
            state["l"] = state["l"] * alpha + jnp.sum(p, axis=-1, keepdims=True)
            state["acc"] = state["acc"] * alpha + _dot(p.astype(_BF16), c)
        state["m"] = m_new

    _software_pipeline(range(SAMPLE_KEY_CHUNKS + 1), issue, finish)
    o_lat = (state["acc"] / state["l"]).astype(_BF16)
    o = sum(_dot(o_lat[h * seq:(h + 1) * seq], w_uvh_ref[h]) for h in range(MLA_HEADS))
    o_ref[...] = _rms(o, g_om_ref[...]).astype(o_ref.dtype)


def _sample_attn(q, ckv_new, kpe_new, ckv_past, kpe_past_t, w_ukt, w_uvh, g_om, batch, seq):
    past, kv_lora = ckv_past.shape[1:]
    width = g_om.shape[1]
    return pl.pallas_call(
        _sample_attn_kernel,
        grid=(batch,),
        in_specs=[pl.BlockSpec((MLA_HEADS, seq, LANES), lambda b: (0, b, 0)),
                  pl.BlockSpec((seq, kv_lora), lambda b: (b, 0)),
                  pl.BlockSpec((seq, LANES), lambda b: (b, 0)),
                  pl.BlockSpec((1, past, kv_lora), lambda b: (b, 0, 0)),
                  pl.BlockSpec((1, ROPE_DIM, past), lambda b: (b, 0, 0)),
                  _whole(w_ukt.shape), _whole(w_uvh.shape), _whole(g_om.shape)],
        out_specs=pl.BlockSpec((seq, width), lambda b: (b, 0)),
        out_shape=jax.ShapeDtypeStruct((batch * seq, width), _BF16),
        compiler_params=pltpu.CompilerParams(dimension_semantics=("arbitrary",),
                                             vmem_limit_bytes=VMEM_LIMIT_BYTES),
        name="sample_attn",
    )(q, ckv_new, kpe_new, ckv_past, kpe_past_t, w_ukt, w_uvh, g_om)


def _output_kernel(x_ref, an_ref, cn_ref, w_out_a_ref, w_out_c_ref, ln_ffn_ref, w_gate_ref, w_up_ref,
                   w_down_ref, g_final_ref, y_ref):
    h = x_ref[...] + _dot(an_ref[...], w_out_a_ref[...]) + _dot(cn_ref[...], w_out_c_ref[...])
    f = _rms(h, ln_ffn_ref[...]).astype(_BF16)
    gate = _dot(f, w_gate_ref[...])
    act = (gate * jax.nn.sigmoid(gate) * _dot(f, w_up_ref[...])).astype(_BF16)
    h = h + _dot(act, w_down_ref[...])
    y_ref[...] = _rms(h, g_final_ref[...])


def _streamed_output_kernel(x_ref, an_ref, cn_ref, w_out_a_ref, w_out_c_ref, ln_ffn_ref, w_gate_ref, w_up_ref,
                            w_down_ref, g_final_ref, y_ref, h_buf, f_buf, down_buf):
    c = pl.program_id(0)

    @pl.when(c == 0)
    def _():
        h = x_ref[...] + _dot(an_ref[...], w_out_a_ref[...]) + _dot(cn_ref[...], w_out_c_ref[...])
        h_buf[...] = h
        f_buf[...] = _rms(h, ln_ffn_ref[...]).astype(f_buf.dtype)
        down_buf[...] = jnp.zeros(down_buf.shape, _F32)

    f = f_buf[...]
    gate = _dot(f, w_gate_ref[...])
    act = (gate * jax.nn.sigmoid(gate) * _dot(f, w_up_ref[...])).astype(_BF16)
    down_buf[...] += _dot(act, w_down_ref[...])

    @pl.when(c == pl.num_programs(0) - 1)
    def _():
        y_ref[...] = _rms(h_buf[...] + down_buf[...], g_final_ref[...])


def _streamed_output(x, an, cn, wts, name):
    n, d = x.shape
    d_ff = wts["w_gate"].shape[1]
    chunk = STREAM_FFN_CHUNK
    assert d_ff % chunk == 0
    names = ("w_out_a", "w_out_c", "ln_ffn", "w_gate", "w_up", "w_down", "g_final")
    specs = {k: _whole(wts[k].shape) for k in names}
    specs["w_gate"] = pl.BlockSpec((d, chunk), lambda c: (0, c))
    specs["w_up"] = pl.BlockSpec((d, chunk), lambda c: (0, c))
    specs["w_down"] = pl.BlockSpec((chunk, d), lambda c: (c, 0))
    return pl.pallas_call(
        _streamed_output_kernel,
        grid=(d_ff // chunk,),
        in_specs=[_whole(x.shape), _whole(an.shape), _whole(cn.shape)] + [specs[k] for k in names],
        out_specs=_whole((n, d)),
        out_shape=jax.ShapeDtypeStruct((n, d), _F32),
        scratch_shapes=[pltpu.VMEM((n, d), _F32), pltpu.VMEM((n, d), _BF16), pltpu.VMEM((n, d), _F32)],
        compiler_params=pltpu.CompilerParams(dimension_semantics=("arbitrary",),
                                             vmem_limit_bytes=VMEM_LIMIT_BYTES),
        name=name,
    )(x, an, cn, *[wts[k] for k in names])


def _output(x, an, cn, wts, name):
    n, d = x.shape
    tm = min(ROW_TILE, n)
    names = ("w_out_a", "w_out_c", "ln_ffn", "w_gate", "w_up", "w_down", "g_final")
    row_block = lambda w: pl.BlockSpec((tm, w), lambda i: (i, 0))
    resident = lambda shape: pl.BlockSpec(shape, lambda i: (0,) * len(shape), pipeline_mode=pl.Buffered(1))
    return pl.pallas_call(
        _output_kernel,
        grid=(n // tm,),
        in_specs=[row_block(d), row_block(an.shape[1]), row_block(cn.shape[1])]
                 + [resident(wts[k].shape) for k in names],
        out_specs=row_block(d),
        out_shape=jax.ShapeDtypeStruct((n, d), _F32),
        compiler_params=pltpu.CompilerParams(dimension_semantics=("arbitrary",),
                                             vmem_limit_bytes=VMEM_LIMIT_BYTES),
        name=name,
    )(x, an, cn, *[wts[k] for k in names])


def _pad_lanes(w, left):
    return jnp.pad(w, [(0, 0)] * (w.ndim - 1) + [(left, LANES - left - w.shape[-1])])


def _prepare_weights(ln_mix, w_in, g_q, w_uq, g_kv, w_uk, w_uv, w_dw, b_dw, g_cn, b_cn, g_om, g_oc, w_out,
                     ln_ffn, g_final):
    q_lora, kv_lora = g_q.shape[0], g_kv.shape[0]
    mla_width = g_om.shape[0]
    row = lambda v: v.reshape(1, -1)
    c1, c2 = q_lora + kv_lora, q_lora + kv_lora + ROPE_DIM
    with_swap = lambda w: jnp.concatenate([w, w[..., HALF_ROPE:], w[..., :HALF_ROPE]], axis=-1)
    w_in_p = jnp.concatenate([w_in[:, :c1], _pad_lanes(with_swap(w_in[:, c1:c2]), ROPE_LO), w_in[:, c2:]], axis=1)
    w_uq_h = w_uq.reshape(q_lora, MLA_HEADS, NOPE_DIM + ROPE_DIM)
    w_uq_p = jnp.concatenate([w_uq_h[..., :NOPE_DIM], with_swap(w_uq_h[..., NOPE_DIM:])], axis=-1)
    w_uq_p = w_uq_p.reshape(q_lora, -1)
    w_uk_p = _pad_lanes(w_uk, 0).reshape(kv_lora, -1)
    w_uvt = jnp.pad(jnp.transpose(w_uv, (1, 2, 0)), ((0, 0), (0, VT_ROWS - V_DIM), (0, 0)))
    v_one = jnp.zeros((MLA_HEADS, VT_ROWS, 1), _F32).at[:, V_DIM].set(1.0)
    w_ukt = jnp.pad(jnp.transpose(w_uk, (1, 2, 0)), ((0, 0), (0, LANES - NOPE_DIM), (0, 0)))
    w_uvh = jnp.stack([jnp.pad(w_uv[:, h], ((0, 0), (h * V_DIM, (MLA_HEADS - 1 - h) * V_DIM)))
                       for h in range(MLA_HEADS)])
    return {
        "ln_mix": row(ln_mix), "w_in": w_in_p.astype(_BF16), "g_q": row(g_q), "w_uq": w_uq_p.astype(_BF16),
        "g_kv": row(g_kv), "w_uk": w_uk_p.astype(_BF16), "w_uvt": w_uvt.reshape(-1, kv_lora).astype(_BF16),
        "v_one": v_one.reshape(-1, 1), "w_dw": w_dw, "b_dw": row(b_dw), "g_cn": row(g_cn), "b_cn": row(b_cn),
        "g_om": row(g_om), "g_oc": row(g_oc),
        "w_ukt": w_ukt.astype(_BF16), "w_uvh": w_uvh.astype(_BF16),
        "w_out_a": w_out[:mla_width].astype(_BF16), "w_out_c": w_out[mla_width:].astype(_BF16),
        "ln_ffn": row(ln_ffn), "g_final": row(g_final),
    }


def _layer(x_prompt, x_sample, ckv_past, kpe_past, conv_past, wts, ffn_f32):
    batch, seq, d = x_prompt.shape
    dec_batch, dec_seq, _ = x_sample.shape
    past = ckv_past.shape[1]

    xp = x_prompt.reshape(batch * seq, d)
    q, k, vt, kv_p, kr_p_t, cn, cv_p = _prompt_proj(xp, _rope_tables(0, seq, 1), wts, seq)
    an, ffn_bf16 = _prompt_attn(q, k, vt, wts["g_om"], list(ffn_f32.values()), batch, seq)
    wts = dict(wts, **dict(zip(ffn_f32, ffn_bf16)))
    y_p = _output(xp, an, cn, wts, "prompt_output")

    xs = x_sample.reshape(dec_batch * dec_seq, d)
    tabs = _rope_tables(past, dec_seq, dec_batch)
    q, kv_s, kr_s, cn, cv_s = _sample_proj(xs, tabs, conv_past, wts, dec_seq)
    an = _sample_attn(q, kv_s, kr_s, ckv_past, jnp.swapaxes(kpe_past, 1, 2), wts["w_ukt"], wts["w_uvh"],
                      wts["g_om"], dec_batch, dec_seq)
    y_s = _streamed_output(xs, an, cn, wts, "sample_output")

    return (y_p.reshape(batch, seq, d), y_s.reshape(dec_batch, dec_seq, d),
            kv_p.reshape(batch, seq, -1), jnp.swapaxes(kr_p_t, 1, 2), cv_p,
            kv_s.reshape(dec_batch, dec_seq, -1),
            kr_s[:, ROPE_LO:ROPE_HI].reshape(dec_batch, dec_seq, ROPE_DIM), cv_s)


def kernel(x_prompt, x_sample, cache_kv_latent, cache_k_rope, state_conv, ln_mix, w_in, g_q, w_uq, g_kv, w_uk, w_uv, w_dw, b_dw, g_cn, b_cn, g_om, g_oc, w_out, ln_ffn, w_gate, w_up, w_down, g_final):
    depth = w_in.shape[0]
    assert depth == 1, "the kernel implements the single-layer model of the problem"
    wts = _prepare_weights(ln_mix[0], w_in[0], g_q[0], w_uq[0], g_kv[0], w_uk[0], w_uv[0], w_dw[0], b_dw[0],
                           g_cn[0], b_cn[0], g_om[0], g_oc[0], w_out[0], ln_ffn[0], g_final)
    ffn_f32 = {"w_gate": w_gate[0], "w_up": w_up[0], "w_down": w_down[0]}
    outs = _layer(x_prompt, x_sample, cache_kv_latent[0], cache_k_rope[0], state_conv[0], wts, ffn_f32)
    y_p, y_s = outs[0], outs[1]
    return (y_p, y_s) + tuple(o[None] for o in outs[2:])
```

```python
import functools
import math

import jax
import jax.numpy as jnp
from jax import lax
from jax.experimental import pallas as pl
from jax.experimental.pallas import tpu as pltpu

CHUNK = 64
MLA_HEADS = 8
NOPE_DIM = 64
ROPE_DIM = 32
V_DIM = 64
ROPE_THETA = 10000.0
EPS = 1e-6
CONV_W = 31
CONV_STATE = CONV_W - 1
ATTN_SCALE = 1.0 / math.sqrt(NOPE_DIM + ROPE_DIM)
MASK_VALUE = -1e30

LANES = 128
SUBLANES = 8
HALF_ROPE = ROPE_DIM // 2
ROPE_LO = NOPE_DIM
ROPE_HI = ROPE_LO + ROPE_DIM
assert ROPE_HI + ROPE_DIM == LANES
VT_ROWS = 80
HALO_ROWS = 32
HALO_PAD = HALO_ROWS - CONV_STATE
CONV_STRIDE = 4
CONV_BLOCK = SUBLANES * CONV_STRIDE
Q_SCALE = ATTN_SCALE * math.log2(math.e)

ROPE_FINE = 32
GLU_PIECE = 256
HEADS_PER_PIECE = 2
ROW_TILE = 512
ATTN_TILE = 1024
KEY_PARTS = 4
SCORE_LOOKAHEAD = 6
SAMPLE_KEY_CHUNKS = 4
ATTN_DIAG_TILE = 256
CAST_ROW_ALIGN = 32
VMEM_LIMIT_BYTES = 56 * 1024 * 1024

_BF16 = jnp.bfloat16
_F32 = jnp.float32


def _rms(x, g):
    return x * lax.rsqrt(jnp.mean(x * x, axis=-1, keepdims=True) + EPS) * g


def _dot(a, b):
    return jnp.dot(a, b, preferred_element_type=_F32)


def _dot_nt(a, b):
    return lax.dot_general(a, b, (((1,), (1,)), ((), ())), preferred_element_type=_F32)


def _rope(x, table, rest):
    y = x * table
    lane = lax.broadcasted_iota(jnp.int32, x.shape, 1)
    in_rope = (lane >= ROPE_LO) & (lane < ROPE_HI)
    return jnp.where(in_rope, y + pltpu.roll(y, LANES - ROPE_DIM, 1), y if rest is None else rest)


def _rope_table_kernel(inv_ref, q_tab_ref, k_tab_ref, *, base, count, fine):
    inv = inv_ref[...].reshape(1, 1, LANES)
    coarse = base + fine * lax.broadcasted_iota(jnp.int32, (count // fine, 1, LANES), 0)
    offset = lax.broadcasted_iota(jnp.int32, (1, fine, LANES), 1)
    ang_a, ang_b = coarse.astype(_F32) * inv, offset.astype(_F32) * inv
    ca, sa, cb, sb = jnp.cos(ang_a), jnp.sin(ang_a), jnp.cos(ang_b), jnp.sin(ang_b)
    c = (ca * cb - sa * sb).reshape(count, LANES)
    s = (sa * cb + ca * sb).reshape(count, LANES)
    lane = lax.broadcasted_iota(jnp.int32, (count, LANES), 1)
    rot = jnp.where(lane < ROPE_HI, c, jnp.where(lane < ROPE_HI + HALF_ROPE, -s, s))
    q_tab = jnp.where(lane < ROPE_LO, Q_SCALE, rot * Q_SCALE)
    for r in range(k_tab_ref.shape[0] // count):
        k_tab_ref[r * count:(r + 1) * count, :] = rot
        q_tab_ref[r * count:(r + 1) * count, :] = q_tab


def _rope_tables(base, count, repeat):
    inv = 1.0 / (ROPE_THETA ** (jnp.arange(0, ROPE_DIM, 2, dtype=_F32) / ROPE_DIM))
    inv_lanes = jnp.zeros((1, LANES), _F32).at[0, ROPE_LO:].set(jnp.tile(inv, 4))
    fine = math.gcd(count, ROPE_FINE)
    assert fine % SUBLANES == 0
    out = jax.ShapeDtypeStruct((count * repeat, LANES), _F32)
    return pl.pallas_call(
        functools.partial(_rope_table_kernel, base=base, count=count, fine=fine),
        out_shape=(out, out),
        name="rope_table",
    )(inv_lanes)


def _in_proj(x_ref, ln_mix_ref, w_in_ref, widths):
    hn = _rms(x_ref[...], ln_mix_ref[...]).astype(_BF16)
    outs, start = [], 0
    for w in widths:
        outs.append(_dot(hn, w_in_ref[:, start:start + w]))
        start += w
    return outs


def _query(cq, g_q_ref, w_uq_ref, q_tab, q_ref):
    q = _dot(_rms(cq, g_q_ref[...]).astype(_BF16), w_uq_ref[...])
    for h in range(MLA_HEADS):
        q_ref[h] = _rope(q[:, h * LANES:(h + 1) * LANES], q_tab, None).astype(q_ref.dtype)


def _depthwise_conv(ubuf, obuf, w_dw_ref, b_dw_ref):
    groups, n, _ = obuf.shape

    def block(g, base):
        lanes = slice(g * LANES, (g + 1) * LANES)
        taps = [ubuf[g, pl.ds(base + HALO_PAD + s, SUBLANES, stride=CONV_STRIDE), :]
                for s in range(CONV_STRIDE + CONV_W - 1)]
        for c in range(CONV_STRIDE):
            acc = jnp.broadcast_to(b_dw_ref[:, lanes], (SUBLANES, LANES))
            for k in range(CONV_W):
                acc = acc + taps[c + k] * w_dw_ref[k:k + 1, lanes]
            obuf[g, pl.ds(base + c, SUBLANES, stride=CONV_STRIDE), :] = acc

    return [functools.partial(block, g, base) for g in range(groups) for base in range(0, n, CONV_BLOCK)]


def _interleave(stages, fillers):
    fillers = list(fillers)
    share = -(-len(fillers) // len(stages))
    for stage in stages:
        stage()
        for filler in fillers[:share]:
            filler()
        fillers = fillers[share:]


def _split_lanes(x):
    return jnp.stack([x[:, c:c + LANES] for c in range(0, x.shape[1], LANES)])


def _merge_lanes(x):
    return jnp.concatenate(list(x), axis=1)


def _conv_branch(dw, g_cn_ref, b_cn_ref, g_oc_ref):
    mu = jnp.mean(dw, axis=-1, keepdims=True)
    xc = dw - mu
    y = xc * lax.rsqrt(jnp.mean(xc * xc, axis=-1, keepdims=True) + EPS) * g_cn_ref[...] + b_cn_ref[...]
    return _rms(y * jax.nn.sigmoid(y), g_oc_ref[...])


def _prompt_proj_kernel(x_ref, q_tab_ref, k_tab_ref, ln_mix_ref, w_in_ref, g_q_ref, w_uq_ref,
                        g_kv_ref, w_uk_ref, w_uvt_ref, v_one_ref, w_dw_ref, b_dw_ref, g_cn_ref, b_cn_ref,
                        g_oc_ref,
                        q_ref, k_ref, vt_ref, ckv_ref, kpe_t_ref, cn_ref, ncv_ref, ubuf, obuf, *, tiles_per_seq):
    rows = x_ref.shape[0]
    q_lora, kv_lora, conv_ch = g_q_ref.shape[1], g_kv_ref.shape[1], g_cn_ref.shape[1]
    hn = _rms(x_ref[...], ln_mix_ref[...]).astype(_BF16)
    w_cq, w_ckv, w_kpe = 0, q_lora, q_lora + kv_lora
    w_a, w_gate = w_kpe + LANES, w_kpe + LANES + conv_ch

    @pl.when(lax.rem(pl.program_id(0), tiles_per_seq) == 0)
    def _():
        ubuf[:, 0:HALO_ROWS, :] = jnp.zeros((ubuf.shape[0], HALO_ROWS, LANES), _F32)

    def glu(c0):
        cols = slice(c0, c0 + GLU_PIECE)
        u = (_dot(hn, w_in_ref[:, w_a + c0:w_a + c0 + GLU_PIECE])
             * jax.nn.sigmoid(_dot(hn, w_in_ref[:, w_gate + c0:w_gate + c0 + GLU_PIECE])))
        ubuf[c0 // LANES:(c0 + GLU_PIECE) // LANES, HALO_ROWS:HALO_ROWS + rows, :] = _split_lanes(u)
        ncv_ref[0, :, cols] = u[rows - CONV_STATE:, :]

    state = {}

    def query_latent():
        cq = _dot(hn, w_in_ref[:, w_cq:w_cq + q_lora])
        state["cq"] = _rms(cq, g_q_ref[...]).astype(_BF16)

    def key_latent():
        ckv = _rms(_dot(hn, w_in_ref[:, w_ckv:w_ckv + kv_lora]), g_kv_ref[...])
        ckv_ref[...] = ckv
        state["ckv"] = ckv.astype(_BF16)
        kpe = _rope(_dot(hn, w_in_ref[:, w_kpe:w_kpe + LANES]), k_tab_ref[...], 0.0)
        kpe_t_ref[0] = kpe.T[ROPE_LO:ROPE_HI, :]
        state["kpe"] = kpe

    def query_heads(h0):
        q = _dot(state["cq"], w_uq_ref[:, h0 * LANES:(h0 + HEADS_PER_PIECE) * LANES])
        for j in range(HEADS_PER_PIECE):
            q_ref[h0 + j] = _rope(q[:, j * LANES:(j + 1) * LANES], q_tab_ref[...], None).astype(q_ref.dtype)

    def key_heads(h0):
        k_nope = _dot(state["ckv"], w_uk_ref[:, h0 * LANES:(h0 + HEADS_PER_PIECE) * LANES])
        for j in range(HEADS_PER_PIECE):
            k_ref[h0 + j] = (k_nope[:, j * LANES:(j + 1) * LANES] + state["kpe"]).astype(k_ref.dtype)

    def values():
        vt = _dot_nt(w_uvt_ref[...], state["ckv"]) + v_one_ref[...]
        key_tile = vt_ref.shape[3]
        for h in range(MLA_HEADS):
            for t in range(rows // key_tile):
                vt_ref[h, t] = vt[h * VT_ROWS:(h + 1) * VT_ROWS,
                                  t * key_tile:(t + 1) * key_tile].astype(vt_ref.dtype)

    stages = [query_latent, key_latent]
    for h0 in range(0, MLA_HEADS, HEADS_PER_PIECE):
        stages += [functools.partial(query_heads, h0), functools.partial(key_heads, h0)]
    stages.append(values)
    glu(0)
    later_glu = [functools.partial(glu, c0) for c0 in range(GLU_PIECE, conv_ch, GLU_PIECE)]
    _interleave(later_glu + stages, _depthwise_conv(ubuf, obuf, w_dw_ref, b_dw_ref))

    cn_ref[...] = _conv_branch(_merge_lanes(obuf[...]), g_cn_ref, b_cn_ref, g_oc_ref).astype(cn_ref.dtype)
    ubuf[:, 0:HALO_ROWS, :] = ubuf[:, rows:HALO_ROWS + rows, :]


def _sample_proj_kernel(x_ref, q_tab_ref, k_tab_ref, state_ref, ln_mix_ref, w_in_ref, g_q_ref,
                        w_uq_ref, g_kv_ref, w_dw_ref, b_dw_ref, g_cn_ref, b_cn_ref, g_oc_ref,
                        q_ref, ckv_ref, kpe_ref, cn_ref, ncv_ref, ubuf, obuf):
    segs = state_ref.shape[0]
    seq = x_ref.shape[0] // segs
    stride = HALO_ROWS + seq
    q_lora, kv_lora, conv_ch = g_q_ref.shape[1], g_kv_ref.shape[1], g_cn_ref.shape[1]
    cq, ckv, kpe, a, gate = _in_proj(x_ref, ln_mix_ref, w_in_ref, (q_lora, kv_lora, LANES, conv_ch, conv_ch))
    _query(cq, g_q_ref, w_uq_ref, q_tab_ref[...], q_ref)
    ckv_ref[...] = _rms(ckv, g_kv_ref[...])
    kpe_ref[...] = _rope(kpe, k_tab_ref[...], 0.0)

    u = a * jax.nn.sigmoid(gate)
    ubuf[...] = jnp.zeros(ubuf.shape, _F32)
    for s in range(segs):
        ubuf[:, s * stride + HALO_PAD:s * stride + HALO_ROWS, :] = _split_lanes(state_ref[s])
        ubuf[:, s * stride + HALO_ROWS:(s + 1) * stride, :] = _split_lanes(u[s * seq:(s + 1) * seq, :])
    for group in _depthwise_conv(ubuf, obuf, w_dw_ref, b_dw_ref):
        group()
    dw = jnp.concatenate([_merge_lanes(obuf[:, s * stride:s * stride + seq, :]) for s in range(segs)], axis=0)
    cn_ref[...] = _conv_branch(dw, g_cn_ref, b_cn_ref, g_oc_ref).astype(cn_ref.dtype)
    for s in range(segs):
        ncv_ref[s] = jnp.concatenate([state_ref[s][seq:, :], u[s * seq:(s + 1) * seq, :]], axis=0)


def _whole(shape):
    zeros = (0,) * len(shape)
    return pl.BlockSpec(shape, lambda *_: zeros)


def _prompt_proj(x, tabs, wts, seq):
    n, d = x.shape
    tm = min(2 * ROW_TILE, seq)
    key_tile = min(ATTN_TILE, seq)
    assert tm % key_tile == 0 and seq % tm == 0 and tm % CONV_BLOCK == 0
    tiles_per_seq = seq // tm
    conv_ch = wts["g_cn"].shape[1]
    kv_lora = wts["g_kv"].shape[1]
    row_block = lambda w: pl.BlockSpec((tm, w), lambda i: (i, 0))
    head_block = pl.BlockSpec((MLA_HEADS, tm, LANES), lambda i: (0, i, 0))
    vt_block = pl.BlockSpec((MLA_HEADS, tm // key_tile, VT_ROWS, key_tile), lambda i: (0, i, 0, 0))
    tab_block = pl.BlockSpec((tm, LANES), lambda i: (lax.rem(i, tiles_per_seq), 0))
    names = ("ln_mix", "w_in", "g_q", "w_uq", "g_kv", "w_uk", "w_uvt", "v_one", "w_dw", "b_dw", "g_cn",
             "b_cn", "g_oc")
    head_shape = jax.ShapeDtypeStruct((MLA_HEADS, n, LANES), _BF16)
    return pl.pallas_call(
        functools.partial(_prompt_proj_kernel, tiles_per_seq=tiles_per_seq),
        grid=(n // tm,),
        in_specs=[row_block(d), tab_block, tab_block] + [_whole(wts[k].shape) for k in names],
        out_specs=(head_block, head_block, vt_block, row_block(kv_lora),
                   pl.BlockSpec((1, ROPE_DIM, tm), lambda i: (i // tiles_per_seq, 0, lax.rem(i, tiles_per_seq))),
                   row_block(conv_ch), pl.BlockSpec((1, CONV_STATE, conv_ch), lambda i: (i // tiles_per_seq, 0, 0))),
        out_shape=(head_shape, head_shape,
                   jax.ShapeDtypeStruct((MLA_HEADS, n // key_tile, VT_ROWS, key_tile), _BF16),
                   jax.ShapeDtypeStruct((n, kv_lora), _F32), jax.ShapeDtypeStruct((n // seq, ROPE_DIM, seq), _F32),
                   jax.ShapeDtypeStruct((n, conv_ch), _BF16),
                   jax.ShapeDtypeStruct((n // seq, CONV_STATE, conv_ch), _F32)),
        scratch_shapes=[pltpu.VMEM((conv_ch // LANES, HALO_ROWS + tm, LANES), _F32),
                        pltpu.VMEM((conv_ch // LANES, tm, LANES), _F32)],
        compiler_params=pltpu.CompilerParams(dimension_semantics=("arbitrary",),
                                             vmem_limit_bytes=VMEM_LIMIT_BYTES),
        name="prompt_proj",
    )(x, *tabs, *[wts[k] for k in names])


def _sample_proj(x, tabs, state, wts, seq):
    n, d = x.shape
    conv_ch = wts["g_cn"].shape[1]
    kv_lora = wts["g_kv"].shape[1]
    names = ("ln_mix", "w_in", "g_q", "w_uq", "g_kv", "w_dw", "b_dw", "g_cn", "b_cn", "g_oc")
    conv_rows = (n // seq) * (HALO_ROWS + seq)
    assert conv_rows % CONV_BLOCK == 0
    return pl.pallas_call(
        _sample_proj_kernel,
        out_shape=(jax.ShapeDtypeStruct((MLA_HEADS, n, LANES), _BF16),
                   jax.ShapeDtypeStruct((n, kv_lora), _F32), jax.ShapeDtypeStruct((n, LANES), _F32),
                   jax.ShapeDtypeStruct((n, conv_ch), _BF16),
                   jax.ShapeDtypeStruct((n // seq, CONV_STATE, conv_ch), _F32)),
        scratch_shapes=[pltpu.VMEM((conv_ch // LANES, conv_rows + HALO_ROWS, LANES), _F32),
                        pltpu.VMEM((conv_ch // LANES, conv_rows, LANES), _F32)],
        compiler_params=pltpu.CompilerParams(vmem_limit_bytes=VMEM_LIMIT_BYTES),
        name="sample_proj",
    )(x, *tabs, state, *[wts[k] for k in names])


def _software_pipeline(items, issue, finish):
    items = list(items)
    issued, done = [], []
    for t in range(len(items) + SCORE_LOOKAHEAD):
        if t < len(items):
            issued.append(issue(items[t]))
        if t >= SCORE_LOOKAHEAD:
            done.append(finish(items[t - SCORE_LOOKAHEAD], issued[t - SCORE_LOOKAHEAD]))
    return done


def _prompt_attn_kernel(q_ref, k_ref, vt_ref, g_om_ref, *refs, sub):
    n_cast = (len(refs) - 1) // 2
    o_ref = refs[n_cast]
    for src_ref, dst_ref in zip(refs[:n_cast], refs[n_cast + 1:]):
        dst_ref[...] = src_ref[...].astype(dst_ref.dtype)
    tile = q_ref.shape[1]
    i = pl.program_id(1)
    chunk_of = lambda t: lax.shift_right_logical(t, CHUNK.bit_length() - 1)

    def key_rows(j):
        return pl.ds(pl.multiple_of(j * tile, tile), tile)

    spans = []
    for r in range(tile // sub):
        n_keys = (r + 1) * sub
        key = lax.broadcasted_iota(jnp.int32, (n_keys, sub), 0)
        query = lax.broadcasted_iota(jnp.int32, (n_keys, sub), 1) + r * sub
        spans.append((slice(r * sub, (r + 1) * sub), n_keys, chunk_of(key) <= chunk_of(query)))

    def diag_scores(h, span):
        q_rows, n_keys, mask = span
        keys = pl.ds(pl.multiple_of(i * tile, tile), n_keys)
        return jnp.where(mask, _dot_nt(k_ref[h, keys, :], q_ref[h, q_rows, :]), MASK_VALUE)

    def diag_finish(h, span, s):
        m = jnp.max(s, axis=0, keepdims=True)
        return m, _dot(vt_ref[h, i, :, 0:span[1]], jnp.exp2(s - m).astype(_BF16))

    items = [(h, span) for h in range(MLA_HEADS) for span in spans]
    done = _software_pipeline(items, lambda it: diag_scores(*it), lambda it, s: diag_finish(*it, s))
    per_head = len(spans)
    maxes = [jnp.concatenate([m for m, _ in done[h * per_head:(h + 1) * per_head]], axis=1)
             for h in range(MLA_HEADS)]
    accs = [jnp.concatenate([a for _, a in done[h * per_head:(h + 1) * per_head]], axis=1)
            for h in range(MLA_HEADS)]

    def body(j, carry):
        state = [list(carry[0]), list(carry[1])]
        part = tile // KEY_PARTS

        def issue(item):
            h, t = item
            rows = pl.ds(pl.multiple_of(j * tile + t * part, part), part)
            return _dot_nt(k_ref[h, rows, :], q_ref[h])

        def finish(item, s):
            h, t = item
            m_old, acc = state[0][h], state[1][h]
            m_new = jnp.maximum(m_old, jnp.max(s, axis=0, keepdims=True))
            p = jnp.exp2(s - m_new).astype(_BF16)
            state[0][h] = m_new
            state[1][h] = acc * jnp.exp2(m_old - m_new) + _dot(vt_ref[h, j, :, t * part:(t + 1) * part], p)

        _software_pipeline([(h, t) for h in range(MLA_HEADS) for t in range(KEY_PARTS)], issue, finish)
        return tuple(state[0]), tuple(state[1])

    _, accs = lax.fori_loop(0, i, body, (tuple(maxes), tuple(accs)))

    o_t = jnp.concatenate([acc[0:V_DIM] / acc[V_DIM:V_DIM + 1] for acc in accs], axis=0)
    o_ref[...] = _rms(o_t.T, g_om_ref[...]).astype(o_ref.dtype)


def _cast_block_rows(rows, steps):
    block = -(-rows // (steps * CAST_ROW_ALIGN)) * CAST_ROW_ALIGN
    while rows % block:
        block += CAST_ROW_ALIGN
    return block


def _prompt_attn(q, k, vt, g_om, to_cast, batch, seq):
    tile = vt.shape[3]
    nq = seq // tile
    width = g_om.shape[1]
    cast_specs = []
    for w in to_cast:
        block = _cast_block_rows(w.shape[0], batch * nq)
        cast_specs.append(pl.BlockSpec(
            (block, w.shape[1]),
            lambda b, i, last=w.shape[0] // block - 1: (jnp.minimum(b * nq + i, last), 0)))
    outs = pl.pallas_call(
        functools.partial(_prompt_attn_kernel, sub=min(ATTN_DIAG_TILE, tile)),
        grid=(batch, nq),
        in_specs=[pl.BlockSpec((MLA_HEADS, tile, LANES), lambda b, i: (0, b * nq + i, 0)),
                  pl.BlockSpec((MLA_HEADS, seq, LANES), lambda b, i: (0, b, 0)),
                  pl.BlockSpec((MLA_HEADS, nq, VT_ROWS, tile), lambda b, i: (0, b, 0, 0)),
                  _whole(g_om.shape)] + cast_specs,
        out_specs=[pl.BlockSpec((tile, width), lambda b, i: (b * nq + i, 0))] + cast_specs,
        out_shape=[jax.ShapeDtypeStruct((batch * seq, width), _BF16)]
                  + [jax.ShapeDtypeStruct(w.shape, _BF16) for w in to_cast],
        compiler_params=pltpu.CompilerParams(dimension_semantics=("arbitrary", "arbitrary"),
                                             vmem_limit_bytes=VMEM_LIMIT_BYTES),
        name="prompt_attn",
    )(q, k, vt, g_om, *to_cast)
    return outs[0], outs[1:]


def _sample_attn_kernel(q_ref, ckv_new_ref, kpe_new_ref, ckv_past_ref, kpe_past_t_ref, w_ukt_ref, w_uvh_ref,
                        g_om_ref, o_ref):
    seq = q_ref.shape[1]
    q_all = jnp.concatenate([q_ref[h] for h in range(MLA_HEADS)], axis=0)
    q_lat = jnp.concatenate([_dot(q_ref[h], w_ukt_ref[h]) for h in range(MLA_HEADS)], axis=0).astype(_BF16)
    q_pe = q_all[:, ROPE_LO:ROPE_HI]
    past = ckv_past_ref.shape[1]
    chunk = past // SAMPLE_KEY_CHUNKS

    def issue(t):
        if t == SAMPLE_KEY_CHUNKS:
            c = ckv_new_ref[...].astype(_BF16)
            return c, _dot_nt(q_lat, c) + _dot_nt(q_all, kpe_new_ref[...].astype(_BF16))
        rows = slice(t * chunk, (t + 1) * chunk)
        c = ckv_past_ref[0, rows, :].astype(_BF16)
        return c, _dot_nt(q_lat, c) + _dot(q_pe, kpe_past_t_ref[0, :, rows].astype(_BF16))

    state = {}

    def finish(t, issued):
        c, s = issued
        tile_max = jnp.max(s, axis=-1, keepdims=True)
        if not state:
            m_new = tile_max
            p = jnp.exp2(s - m_new)
            state["l"] = jnp.sum(p, axis=-1, keepdims=True)
            state["acc"] = _dot(p.astype(_BF16), c)
        else:
            m_new = jnp.maximum(state["m"], tile_max)
            alpha = jnp.exp2(state["m"] - m_new)
            p = jnp.exp2(s - m_new)
            state["l"] = state["l"] * alpha + jnp.sum(p, axis=-1, keepdims=True)
            state["acc"] = state["acc"] * alpha + _dot(p.astype(_BF16), c)
        state["m"] = m_new

    _software_pipeline(range(SAMPLE_KEY_CHUNKS + 1), issue, finish)
    o_lat = (state["acc"] / state["l"]).astype(_BF16)
    o = sum(_dot(o_lat[h * seq:(h + 1) * seq], w_uvh_ref[h]) for h in range(MLA_HEADS))
    o_ref[...] = _rms(o, g_om_ref[...]).astype(o_ref.dtype)


def _sample_attn(q, ckv_new, kpe_new, ckv_past, kpe_past_t, w_ukt, w_uvh, g_om, batch, seq):
    past, kv_lora = ckv_past.shape[1:]
    width = g_om.shape[1]
    return pl.pallas_call(
        _sample_attn_kernel,
        grid=(batch,),
        in_specs=[pl.BlockSpec((MLA_HEADS, seq, LANES), lambda b: (0, b, 0)),
                  pl.BlockSpec((seq, kv_lora), lambda b: (b, 0)),
                  pl.BlockSpec((seq, LANES), lambda b: (b, 0)),
                  pl.BlockSpec((1, past, kv_lora), lambda b: (b, 0, 0)),
                  pl.BlockSpec((1, ROPE_DIM, past), lambda b: (b, 0, 0)),
                  _whole(w_ukt.shape), _whole(w_uvh.shape), _whole(g_om.shape)],
        out_specs=pl.BlockSpec((seq, width), lambda b: (b, 0)),
        out_shape=jax.ShapeDtypeStruct((batch * seq, width), _BF16),
        compiler_params=pltpu.CompilerParams(dimension_semantics=("arbitrary",),
                                             vmem_limit_bytes=VMEM_LIMIT_BYTES),
        name="sample_attn",
    )(q, ckv_new, kpe_new, ckv_past, kpe_past_t, w_ukt, w_uvh, g_om)


def _mix_ffn(x_ref, an_ref, cn_ref, weight_refs, y_ref):
    w_out_a_ref, w_out_c_ref, ln_ffn_ref, w_gate_ref, w_up_ref, w_down_ref, g_final_ref = weight_refs
    h = x_ref[...] + _dot(an_ref[...], w_out_a_ref[...]) + _dot(cn_ref[...], w_out_c_ref[...])
    f = _rms(h, ln_ffn_ref[...]).astype(_BF16)
    gate = _dot(f, w_gate_ref[...])
    act = (gate * jax.nn.sigmoid(gate) * _dot(f, w_up_ref[...])).astype(_BF16)
    h = h + _dot(act, w_down_ref[...])
    y_ref[...] = _rms(h, g_final_ref[...])


def _output_kernel(x_ref, an_ref, cn_ref, xs_ref, ans_ref, cns_ref, *refs):
    weight_refs, (y_ref, ys_ref) = refs[:-2], refs[-2:]
    last = pl.num_programs(0) - 1

    @pl.when(pl.program_id(0) < last)
    def _():
        _mix_ffn(x_ref, an_ref, cn_ref, weight_refs, y_ref)

    @pl.when(pl.program_id(0) == last)
    def _():
        _mix_ffn(xs_ref, ans_ref, cns_ref, weight_refs, ys_ref)


def _output(x, an, cn, xs, ans, cns, wts):
    n, d = x.shape
    tm = min(ROW_TILE, n)
    n_tiles = n // tm
    names = ("w_out_a", "w_out_c", "ln_ffn", "w_gate", "w_up", "w_down", "g_final")
    row_block = lambda w: pl.BlockSpec((tm, w), lambda i: (jnp.minimum(i, n_tiles - 1), 0))
    resident = lambda shape: pl.BlockSpec(shape, lambda i: (0,) * len(shape), pipeline_mode=pl.Buffered(1))
    return pl.pallas_call(
        _output_kernel,
        grid=(n_tiles + 1,),
        in_specs=[row_block(d), row_block(an.shape[1]), row_block(cn.shape[1]),
                  resident(xs.shape), resident(ans.shape), resident(cns.shape)]
                 + [resident(wts[k].shape) for k in names],
        out_specs=(row_block(d), _whole(xs.shape)),
        out_shape=(jax.ShapeDtypeStruct((n, d), _F32), jax.ShapeDtypeStruct(xs.shape, _F32)),
        compiler_params=pltpu.CompilerParams(dimension_semantics=("arbitrary",),
                                             vmem_limit_bytes=VMEM_LIMIT_BYTES),
        name="output",
    )(x, an, cn, xs, ans, cns, *[wts[k] for k in names])


def _pad_lanes(w, left):
    return jnp.pad(w, [(0, 0)] * (w.ndim - 1) + [(left, LANES - left - w.shape[-1])])


def _prepare_weights(ln_mix, w_in, g_q, w_uq, g_kv, w_uk, w_uv, w_dw, b_dw, g_cn, b_cn, g_om, g_oc, w_out,
                     ln_ffn, g_final):
    q_lora, kv_lora = g_q.shape[0], g_kv.shape[0]
    mla_width = g_om.shape[0]
    row = lambda v: v.reshape(1, -1)
    c1, c2 = q_lora + kv_lora, q_lora + kv_lora + ROPE_DIM
    with_swap = lambda w: jnp.concatenate([w, w[..., HALF_ROPE:], w[..., :HALF_ROPE]], axis=-1)
    w_in_p = jnp.concatenate([w_in[:, :c1], _pad_lanes(with_swap(w_in[:, c1:c2]), ROPE_LO), w_in[:, c2:]], axis=1)
    w_uq_h = w_uq.reshape(q_lora, MLA_HEADS, NOPE_DIM + ROPE_DIM)
    w_uq_p = jnp.concatenate([w_uq_h[..., :NOPE_DIM], with_swap(w_uq_h[..., NOPE_DIM:])], axis=-1)
    w_uq_p = w_uq_p.reshape(q_lora, -1)
    w_uk_p = _pad_lanes(w_uk, 0).reshape(kv_lora, -1)
    w_uvt = jnp.pad(jnp.transpose(w_uv, (1, 2, 0)), ((0, 0), (0, VT_ROWS - V_DIM), (0, 0)))
    v_one = jnp.zeros((MLA_HEADS, VT_ROWS, 1), _F32).at[:, V_DIM].set(1.0)
    w_ukt = jnp.pad(jnp.transpose(w_uk, (1, 2, 0)), ((0, 0), (0, LANES - NOPE_DIM), (0, 0)))
    w_uvh = jnp.stack([jnp.pad(w_uv[:, h], ((0, 0), (h * V_DIM, (MLA_HEADS - 1 - h) * V_DIM)))
                       for h in range(MLA_HEADS)])
    return {
        "ln_mix": row(ln_mix), "w_in": w_in_p.astype(_BF16), "g_q": row(g_q), "w_uq": w_uq_p.astype(_BF16),
        "g_kv": row(g_kv), "w_uk": w_uk_p.astype(_BF16), "w_uvt": w_uvt.reshape(-1, kv_lora).astype(_BF16),
        "v_one": v_one.reshape(-1, 1), "w_dw": w_dw, "b_dw": row(b_dw), "g_cn": row(g_cn), "b_cn": row(b_cn),
        "g_om": row(g_om), "g_oc": row(g_oc),
        "w_ukt": w_ukt.astype(_BF16), "w_uvh": w_uvh.astype(_BF16),
        "w_out_a": w_out[:mla_width].astype(_BF16), "w_out_c": w_out[mla_width:].astype(_BF16),
        "ln_ffn": row(ln_ffn), "g_final": row(g_final),
    }


def _layer(x_prompt, x_sample, ckv_past, kpe_past, conv_past, wts, ffn_f32):
    batch, seq, d = x_prompt.shape
    dec_batch, dec_seq, _ = x_sample.shape
    past = ckv_past.shape[1]

    xp = x_prompt.reshape(batch * seq, d)
    q, k, vt, kv_p, kr_p_t, cn, cv_p = _prompt_proj(xp, _rope_tables(0, seq, 1), wts, seq)
    an, ffn_bf16 = _prompt_attn(q, k, vt, wts["g_om"], list(ffn_f32.values()), batch, seq)
    wts = dict(wts, **dict(zip(ffn_f32, ffn_bf16)))

    xs = x_sample.reshape(dec_batch * dec_seq, d)
    tabs = _rope_tables(past, dec_seq, dec_batch)
    q, kv_s, kr_s, cn_s, cv_s = _sample_proj(xs, tabs, conv_past, wts, dec_seq)
    an_s = _sample_attn(q, kv_s, kr_s, ckv_past, jnp.swapaxes(kpe_past, 1, 2), wts["w_ukt"], wts["w_uvh"],
                        wts["g_om"], dec_batch, dec_seq)
    y_p, y_s = _output(xp, an, cn, xs, an_s, cn_s, wts)

    return (y_p.reshape(batch, seq, d), y_s.reshape(dec_batch, dec_seq, d),
            kv_p.reshape(batch, seq, -1), jnp.swapaxes(kr_p_t, 1, 2), cv_p,
            kv_s.reshape(dec_batch, dec_seq, -1),
            kr_s[:, ROPE_LO:ROPE_HI].reshape(dec_batch, dec_seq, ROPE_DIM), cv_s)


def kernel(x_prompt, x_sample, cache_kv_latent, cache_k_rope, state_conv, ln_mix, w_in, g_q, w_uq, g_kv, w_uk, w_uv, w_dw, b_dw, g_cn, b_cn, g_om, g_oc, w_out, ln_ffn, w_gate, w_up, w_down, g_final):
    depth = w_in.shape[0]
    assert depth == 1, "the kernel implements the single-layer model of the problem"
    wts = _prepare_weights(ln_mix[0], w_in[0], g_q[0], w_uq[0], g_kv[0], w_uk[0], w_uv[0], w_dw[0], b_dw[0],
                           g_cn[0], b_cn[0], g_om[0], g_oc[0], w_out[0], ln_ffn[0], g_final)
    ffn_f32 = {"w_gate": w_gate[0], "w_up": w_up[0], "w_down": w_down[0]}
    outs = _layer(x_prompt, x_sample, cache_kv_latent[0], cache_k_rope[0], state_conv[0], wts, ffn_f32)
    y_p, y_s = outs[0], outs[1]
    return (y_p, y_s) + tuple(o[None] for o in outs[2:])
```

```python
import functools
import math

import jax
import jax.numpy as jnp
from jax import lax
from jax.experimental import pallas as pl
from jax.experimental.pallas import tpu as pltpu

CHUNK = 64
MLA_HEADS = 8
NOPE_DIM = 64
ROPE_DIM = 32
V_DIM = 64
ROPE_THETA = 10000.0
EPS = 1e-6
CONV_W = 31
CONV_STATE = CONV_W - 1
ATTN_SCALE = 1.0 / math.sqrt(NOPE_DIM + ROPE_DIM)
MASK_VALUE = -1e30

LANES = 128
SUBLANES = 8
HALF_ROPE = ROPE_DIM // 2
ROPE_LO = NOPE_DIM
ROPE_HI = ROPE_LO + ROPE_DIM
assert ROPE_HI + ROPE_DIM == LANES
VT_ROWS = 80
HALO_ROWS = 32
HALO_PAD = HALO_ROWS - CONV_STATE
CONV_STRIDE = 4
CONV_BLOCK = SUBLANES * CONV_STRIDE
Q_SCALE = ATTN_SCALE * math.log2(math.e)

ROPE_FINE = 32
GLU_PIECE = 256
HEADS_PER_PIECE = 2
ROW_TILE = 512
ATTN_TILE = 1024
KEY_PARTS = 4
SCORE_LOOKAHEAD = 8
SAMPLE_KEY_CHUNKS = 4
ATTN_DIAG_TILE = 256
CAST_ROW_ALIGN = 32
VMEM_LIMIT_BYTES = 56 * 1024 * 1024

_BF16 = jnp.bfloat16
_F32 = jnp.float32


def _rms(x, g):
    return x * lax.rsqrt(jnp.mean(x * x, axis=-1, keepdims=True) + EPS) * g


def _dot(a, b):
    return jnp.dot(a, b, preferred_element_type=_F32)


def _dot_nt(a, b):
    return lax.dot_general(a, b, (((1,), (1,)), ((), ())), preferred_element_type=_F32)


def _rope(x, table, rest):
    y = x * table
    lane = lax.broadcasted_iota(jnp.int32, x.shape, 1)
    in_rope = (lane >= ROPE_LO) & (lane < ROPE_HI)
    return jnp.where(in_rope, y + pltpu.roll(y, LANES - ROPE_DIM, 1), y if rest is None else rest)


def _rope_table_kernel(inv_ref, q_tab_ref, k_tab_ref, *, base, count, fine):
    inv = inv_ref[...].reshape(1, 1, LANES)
    coarse = base + fine * lax.broadcasted_iota(jnp.int32, (count // fine, 1, LANES), 0)
    offset = lax.broadcasted_iota(jnp.int32, (1, fine, LANES), 1)
    ang_a, ang_b = coarse.astype(_F32) * inv, offset.astype(_F32) * inv
    ca, sa, cb, sb = jnp.cos(ang_a), jnp.sin(ang_a), jnp.cos(ang_b), jnp.sin(ang_b)
    c = (ca * cb - sa * sb).reshape(count, LANES)
    s = (sa * cb + ca * sb).reshape(count, LANES)
    lane = lax.broadcasted_iota(jnp.int32, (count, LANES), 1)
    rot = jnp.where(lane < ROPE_HI, c, jnp.where(lane < ROPE_HI + HALF_ROPE, -s, s))
    q_tab = jnp.where(lane < ROPE_LO, Q_SCALE, rot * Q_SCALE)
    for r in range(k_tab_ref.shape[0] // count):
        k_tab_ref[r * count:(r + 1) * count, :] = rot
        q_tab_ref[r * count:(r + 1) * count, :] = q_tab


def _rope_tables(base, count, repeat):
    inv = 1.0 / (ROPE_THETA ** (jnp.arange(0, ROPE_DIM, 2, dtype=_F32) / ROPE_DIM))
    inv_lanes = jnp.zeros((1, LANES), _F32).at[0, ROPE_LO:].set(jnp.tile(inv, 4))
    fine = math.gcd(count, ROPE_FINE)
    assert fine % SUBLANES == 0
    out = jax.ShapeDtypeStruct((count * repeat, LANES), _F32)
    return pl.pallas_call(
        functools.partial(_rope_table_kernel, base=base, count=count, fine=fine),
        out_shape=(out, out),
        name="rope_table",
    )(inv_lanes)


def _in_proj(x_ref, ln_mix_ref, w_in_ref, widths):
    hn = _rms(x_ref[...], ln_mix_ref[...]).astype(_BF16)
    outs, start = [], 0
    for w in widths:
        outs.append(_dot(hn, w_in_ref[:, start:start + w]))
        start += w
    return outs


def _query(cq, g_q_ref, w_uq_ref, q_tab, q_ref):
    q = _dot(_rms(cq, g_q_ref[...]).astype(_BF16), w_uq_ref[...])
    for h in range(MLA_HEADS):
        q_ref[h] = _rope(q[:, h * LANES:(h + 1) * LANES], q_tab, None).astype(q_ref.dtype)


def _depthwise_conv(ubuf, obuf, w_dw_ref, b_dw_ref):
    groups, n, _ = obuf.shape

    def block(g, base):
        lanes = slice(g * LANES, (g + 1) * LANES)
        taps = [ubuf[g, pl.ds(base + HALO_PAD + s, SUBLANES, stride=CONV_STRIDE), :]
                for s in range(CONV_STRIDE + CONV_W - 1)]
        for c in range(CONV_STRIDE):
            acc = jnp.broadcast_to(b_dw_ref[:, lanes], (SUBLANES, LANES))
            for k in range(CONV_W):
                acc = acc + taps[c + k] * w_dw_ref[k:k + 1, lanes]
            obuf[g, pl.ds(base + c, SUBLANES, stride=CONV_STRIDE), :] = acc

    return [functools.partial(block, g, base) for g in range(groups) for base in range(0, n, CONV_BLOCK)]


def _interleave(stages, fillers):
    fillers = list(fillers)
    share = -(-len(fillers) // len(stages))
    for stage in stages:
        stage()
        for filler in fillers[:share]:
            filler()
        fillers = fillers[share:]


def _split_lanes(x):
    return jnp.stack([x[:, c:c + LANES] for c in range(0, x.shape[1], LANES)])


def _merge_lanes(x):
    return jnp.concatenate(list(x), axis=1)


def _conv_branch(dw, g_cn_ref, b_cn_ref, g_oc_ref):
    mu = jnp.mean(dw, axis=-1, keepdims=True)
    xc = dw - mu
    y = xc * lax.rsqrt(jnp.mean(xc * xc, axis=-1, keepdims=True) + EPS) * g_cn_ref[...] + b_cn_ref[...]
    return _rms(y * jax.nn.sigmoid(y), g_oc_ref[...])


def _prompt_proj_kernel(x_ref, q_tab_ref, k_tab_ref, ln_mix_ref, w_in_ref, g_q_ref, w_uq_ref,
                        g_kv_ref, w_uk_ref, w_uvt_ref, v_one_ref, w_dw_ref, b_dw_ref, g_cn_ref, b_cn_ref,
                        g_oc_ref,
                        q_ref, k_ref, vt_ref, ckv_ref, kpe_t_ref, cn_ref, ncv_ref, ubuf, obuf, *, tiles_per_seq):
    rows = x_ref.shape[0]
    q_lora, kv_lora, conv_ch = g_q_ref.shape[1], g_kv_ref.shape[1], g_cn_ref.shape[1]
    hn = _rms(x_ref[...], ln_mix_ref[...]).astype(_BF16)
    w_cq, w_ckv, w_kpe = 0, q_lora, q_lora + kv_lora
    w_a, w_gate = w_kpe + LANES, w_kpe + LANES + conv_ch

    @pl.when(lax.rem(pl.program_id(0), tiles_per_seq) == 0)
    def _():
        ubuf[:, 0:HALO_ROWS, :] = jnp.zeros((ubuf.shape[0], HALO_ROWS, LANES), _F32)

    def glu(c0):
        cols = slice(c0, c0 + GLU_PIECE)
        u = (_dot(hn, w_in_ref[:, w_a + c0:w_a + c0 + GLU_PIECE])
             * jax.nn.sigmoid(_dot(hn, w_in_ref[:, w_gate + c0:w_gate + c0 + GLU_PIECE])))
        ubuf[c0 // LANES:(c0 + GLU_PIECE) // LANES, HALO_ROWS:HALO_ROWS + rows, :] = _split_lanes(u)
        ncv_ref[0, :, cols] = u[rows - CONV_STATE:, :]

    state = {}

    def query_latent():
        cq = _dot(hn, w_in_ref[:, w_cq:w_cq + q_lora])
        state["cq"] = _rms(cq, g_q_ref[...]).astype(_BF16)

    def key_latent():
        ckv = _rms(_dot(hn, w_in_ref[:, w_ckv:w_ckv + kv_lora]), g_kv_ref[...])
        ckv_ref[...] = ckv
        state["ckv"] = ckv.astype(_BF16)
        kpe = _rope(_dot(hn, w_in_ref[:, w_kpe:w_kpe + LANES]), k_tab_ref[...], 0.0)
        kpe_t_ref[0] = kpe.T[ROPE_LO:ROPE_HI, :]
        state["kpe"] = kpe

    def query_heads(h0):
        q = _dot(state["cq"], w_uq_ref[:, h0 * LANES:(h0 + HEADS_PER_PIECE) * LANES])
        for j in range(HEADS_PER_PIECE):
            q_ref[h0 + j] = _rope(q[:, j * LANES:(j + 1) * LANES], q_tab_ref[...], None).astype(q_ref.dtype)

    def key_heads(h0):
        k_nope = _dot(state["ckv"], w_uk_ref[:, h0 * LANES:(h0 + HEADS_PER_PIECE) * LANES])
        for j in range(HEADS_PER_PIECE):
            k_ref[h0 + j] = (k_nope[:, j * LANES:(j + 1) * LANES] + state["kpe"]).astype(k_ref.dtype)

    def values():
        vt = _dot_nt(w_uvt_ref[...], state["ckv"]) + v_one_ref[...]
        key_tile = vt_ref.shape[3]
        for h in range(MLA_HEADS):
            for t in range(rows // key_tile):
                vt_ref[h, t] = vt[h * VT_ROWS:(h + 1) * VT_ROWS,
                                  t * key_tile:(t + 1) * key_tile].astype(vt_ref.dtype)

    stages = [query_latent, key_latent]
    for h0 in range(0, MLA_HEADS, HEADS_PER_PIECE):
        stages += [functools.partial(query_heads, h0), functools.partial(key_heads, h0)]
    stages.append(values)
    glu(0)
    later_glu = [functools.partial(glu, c0) for c0 in range(GLU_PIECE, conv_ch, GLU_PIECE)]
    _interleave(stages[:2] + later_glu + stages[2:], _depthwise_conv(ubuf, obuf, w_dw_ref, b_dw_ref))

    cn_ref[...] = _conv_branch(_merge_lanes(obuf[...]), g_cn_ref, b_cn_ref, g_oc_ref).astype(cn_ref.dtype)
    ubuf[:, 0:HALO_ROWS, :] = ubuf[:, rows:HALO_ROWS + rows, :]


def _sample_proj_kernel(x_ref, q_tab_ref, k_tab_ref, state_ref, ln_mix_ref, w_in_ref, g_q_ref,
                        w_uq_ref, g_kv_ref, w_dw_ref, b_dw_ref, g_cn_ref, b_cn_ref, g_oc_ref,
                        q_ref, ckv_ref, kpe_ref, cn_ref, ncv_ref, ubuf, obuf):
    segs = state_ref.shape[0]
    seq = x_ref.shape[0] // segs
    stride = HALO_ROWS + seq
    q_lora, kv_lora, conv_ch = g_q_ref.shape[1], g_kv_ref.shape[1], g_cn_ref.shape[1]
    cq, ckv, kpe, a, gate = _in_proj(x_ref, ln_mix_ref, w_in_ref, (q_lora, kv_lora, LANES, conv_ch, conv_ch))
    _query(cq, g_q_ref, w_uq_ref, q_tab_ref[...], q_ref)
    ckv_ref[...] = _rms(ckv, g_kv_ref[...])
    kpe_ref[...] = _rope(kpe, k_tab_ref[...], 0.0)

    u = a * jax.nn.sigmoid(gate)
    ubuf[...] = jnp.zeros(ubuf.shape, _F32)
    for s in range(segs):
        ubuf[:, s * stride + HALO_PAD:s * stride + HALO_ROWS, :] = _split_lanes(state_ref[s])
        ubuf[:, s * stride + HALO_ROWS:(s + 1) * stride, :] = _split_lanes(u[s * seq:(s + 1) * seq, :])
    for group in _depthwise_conv(ubuf, obuf, w_dw_ref, b_dw_ref):
        group()
    dw = jnp.concatenate([_merge_lanes(obuf[:, s * stride:s * stride + seq, :]) for s in range(segs)], axis=0)
    cn_ref[...] = _conv_branch(dw, g_cn_ref, b_cn_ref, g_oc_ref).astype(cn_ref.dtype)
    for s in range(segs):
        ncv_ref[s] = jnp.concatenate([state_ref[s][seq:, :], u[s * seq:(s + 1) * seq, :]], axis=0)


def _whole(shape):
    zeros = (0,) * len(shape)
    return pl.BlockSpec(shape, lambda *_: zeros)


def _prompt_proj(x, tabs, wts, seq):
    n, d = x.shape
    tm = min(2 * ROW_TILE, seq)
    key_tile = min(ATTN_TILE, seq)
    assert tm % key_tile == 0 and seq % tm == 0 and tm % CONV_BLOCK == 0
    tiles_per_seq = seq // tm
    conv_ch = wts["g_cn"].shape[1]
    kv_lora = wts["g_kv"].shape[1]
    row_block = lambda w: pl.BlockSpec((tm, w), lambda i: (i, 0))
    head_block = pl.BlockSpec((MLA_HEADS, tm, LANES), lambda i: (0, i, 0))
    vt_block = pl.BlockSpec((MLA_HEADS, tm // key_tile, VT_ROWS, key_tile), lambda i: (0, i, 0, 0))
    tab_block = pl.BlockSpec((tm, LANES), lambda i: (lax.rem(i, tiles_per_seq), 0))
    names = ("ln_mix", "w_in", "g_q", "w_uq", "g_kv", "w_uk", "w_uvt", "v_one", "w_dw", "b_dw", "g_cn",
             "b_cn", "g_oc")
    head_shape = jax.ShapeDtypeStruct((MLA_HEADS, n, LANES), _BF16)
    return pl.pallas_call(
        functools.partial(_prompt_proj_kernel, tiles_per_seq=tiles_per_seq),
        grid=(n // tm,),
        in_specs=[row_block(d), tab_block, tab_block] + [_whole(wts[k].shape) for k in names],
        out_specs=(head_block, head_block, vt_block, row_block(kv_lora),
                   pl.BlockSpec((1, ROPE_DIM, tm), lambda i: (i // tiles_per_seq, 0, lax.rem(i, tiles_per_seq))),
                   row_block(conv_ch), pl.BlockSpec((1, CONV_STATE, conv_ch), lambda i: (i // tiles_per_seq, 0, 0))),
        out_shape=(head_shape, head_shape,
                   jax.ShapeDtypeStruct((MLA_HEADS, n // key_tile, VT_ROWS, key_tile), _BF16),
                   jax.ShapeDtypeStruct((n, kv_lora), _F32), jax.ShapeDtypeStruct((n // seq, ROPE_DIM, seq), _F32),
                   jax.ShapeDtypeStruct((n, conv_ch), _BF16),
                   jax.ShapeDtypeStruct((n // seq, CONV_STATE, conv_ch), _F32)),
        scratch_shapes=[pltpu.VMEM((conv_ch // LANES, HALO_ROWS + tm, LANES), _F32),
                        pltpu.VMEM((conv_ch // LANES, tm, LANES), _F32)],
        compiler_params=pltpu.CompilerParams(dimension_semantics=("arbitrary",),
                                             vmem_limit_bytes=VMEM_LIMIT_BYTES),
        name="prompt_proj",
    )(x, *tabs, *[wts[k] for k in names])


def _sample_proj(x, tabs, state, wts, seq):
    n, d = x.shape
    conv_ch = wts["g_cn"].shape[1]
    kv_lora = wts["g_kv"].shape[1]
    names = ("ln_mix", "w_in", "g_q", "w_uq", "g_kv", "w_dw", "b_dw", "g_cn", "b_cn", "g_oc")
    conv_rows = (n // seq) * (HALO_ROWS + seq)
    assert conv_rows % CONV_BLOCK == 0
    return pl.pallas_call(
        _sample_proj_kernel,
        out_shape=(jax.ShapeDtypeStruct((MLA_HEADS, n, LANES), _BF16),
                   jax.ShapeDtypeStruct((n, kv_lora), _F32), jax.ShapeDtypeStruct((n, LANES), _F32),
                   jax.ShapeDtypeStruct((n, conv_ch), _BF16),
                   jax.ShapeDtypeStruct((n // seq, CONV_STATE, conv_ch), _F32)),
        scratch_shapes=[pltpu.VMEM((conv_ch // LANES, conv_rows + HALO_ROWS, LANES), _F32),
                        pltpu.VMEM((conv_ch // LANES, conv_rows, LANES), _F32)],
        compiler_params=pltpu.CompilerParams(vmem_limit_bytes=VMEM_LIMIT_BYTES),
        name="sample_proj",
    )(x, *tabs, state, *[wts[k] for k in names])


def _software_pipeline(items, issue, finish):
    items = list(items)
    issued, done = [], []
    for t in range(len(items) + SCORE_LOOKAHEAD):
        if t < len(items):
            issued.append(issue(items[t]))
        if t >= SCORE_LOOKAHEAD:
            done.append(finish(items[t - SCORE_LOOKAHEAD], issued[t - SCORE_LOOKAHEAD]))
    return done


def _prompt_attn_kernel(q_ref, k_ref, vt_ref, g_om_ref, *refs, sub):
    n_cast = (len(refs) - 1) // 2
    o_ref = refs[n_cast]
    for src_ref, dst_ref in zip(refs[:n_cast], refs[n_cast + 1:]):
        dst_ref[...] = src_ref[...].astype(dst_ref.dtype)
    tile = q_ref.shape[1]
    i = pl.program_id(1)
    chunk_of = lambda t: lax.shift_right_logical(t, CHUNK.bit_length() - 1)

    def key_rows(j):
        return pl.ds(pl.multiple_of(j * tile, tile), tile)

    spans = []
    for r in range(tile // sub):
        n_keys = (r + 1) * sub
        key = lax.broadcasted_iota(jnp.int32, (n_keys, sub), 0)
        query = lax.broadcasted_iota(jnp.int32, (n_keys, sub), 1) + r * sub
        spans.append((slice(r * sub, (r + 1) * sub), n_keys, chunk_of(key) <= chunk_of(query)))

    def diag_scores(h, span):
        q_rows, n_keys, mask = span
        keys = pl.ds(pl.multiple_of(i * tile, tile), n_keys)
        return jnp.where(mask, _dot_nt(k_ref[h, keys, :], q_ref[h, q_rows, :]), MASK_VALUE)

    def diag_finish(h, span, s):
        m = jnp.max(s, axis=0, keepdims=True)
        return m, _dot(vt_ref[h, i, :, 0:span[1]], jnp.exp2(s - m).astype(_BF16))

    items = [(h, span) for h in range(MLA_HEADS) for span in spans]
    done = _software_pipeline(items, lambda it: diag_scores(*it), lambda it, s: diag_finish(*it, s))
    per_head = len(spans)
    maxes = [jnp.concatenate([m for m, _ in done[h * per_head:(h + 1) * per_head]], axis=1)
             for h in range(MLA_HEADS)]
    accs = [jnp.concatenate([a for _, a in done[h * per_head:(h + 1) * per_head]], axis=1)
            for h in range(MLA_HEADS)]

    def body(j, carry):
        state = [list(carry[0]), list(carry[1])]
        part = tile // KEY_PARTS

        def issue(item):
            h, t = item
            rows = pl.ds(pl.multiple_of(j * tile + t * part, part), part)
            return _dot_nt(k_ref[h, rows, :], q_ref[h])

        def finish(item, s):
            h, t = item
            m_old, acc = state[0][h], state[1][h]
            m_new = jnp.maximum(m_old, jnp.max(s, axis=0, keepdims=True))
            p = jnp.exp2(s - m_new).astype(_BF16)
            state[0][h] = m_new
            state[1][h] = acc * jnp.exp2(m_old - m_new) + _dot(vt_ref[h, j, :, t * part:(t + 1) * part], p)

        _software_pipeline([(h, t) for h in range(MLA_HEADS) for t in range(KEY_PARTS)], issue, finish)
        return tuple(state[0]), tuple(state[1])

    _, accs = lax.fori_loop(0, i, body, (tuple(maxes), tuple(accs)))

    o_t = jnp.concatenate([acc[0:V_DIM] / acc[V_DIM:V_DIM + 1] for acc in accs], axis=0)
    o_ref[...] = _rms(o_t.T, g_om_ref[...]).astype(o_ref.dtype)


def _cast_block_rows(rows, steps):
    block = -(-rows // (steps * CAST_ROW_ALIGN)) * CAST_ROW_ALIGN
    while rows % block:
        block += CAST_ROW_ALIGN
    return block


def _prompt_attn(q, k, vt, g_om, to_cast, batch, seq):
    tile = vt.shape[3]
    nq = seq // tile
    width = g_om.shape[1]
    cast_specs = []
    for w in to_cast:
        block = _cast_block_rows(w.shape[0], batch * nq)
        cast_specs.append(pl.BlockSpec(
            (block, w.shape[1]),
            lambda b, i, last=w.shape[0] // block - 1: (jnp.minimum(b * nq + i, last), 0)))
    outs = pl.pallas_call(
        functools.partial(_prompt_attn_kernel, sub=min(ATTN_DIAG_TILE, tile)),
        grid=(batch, nq),
        in_specs=[pl.BlockSpec((MLA_HEADS, tile, LANES), lambda b, i: (0, b * nq + i, 0)),
                  pl.BlockSpec((MLA_HEADS, seq, LANES), lambda b, i: (0, b, 0)),
                  pl.BlockSpec((MLA_HEADS, nq, VT_ROWS, tile), lambda b, i: (0, b, 0, 0)),
                  _whole(g_om.shape)] + cast_specs,
        out_specs=[pl.BlockSpec((tile, width), lambda b, i: (b * nq + i, 0))] + cast_specs,
        out_shape=[jax.ShapeDtypeStruct((batch * seq, width), _BF16)]
                  + [jax.ShapeDtypeStruct(w.shape, _BF16) for w in to_cast],
        compiler_params=pltpu.CompilerParams(dimension_semantics=("arbitrary", "arbitrary"),
                                             vmem_limit_bytes=VMEM_LIMIT_BYTES),
        name="prompt_attn",
    )(q, k, vt, g_om, *to_cast)
    return outs[0], outs[1:]


def _sample_attn_kernel(q_ref, ckv_new_ref, kpe_new_ref, ckv_past_ref, kpe_past_t_ref, w_ukt_ref, w_uvh_ref,
                        g_om_ref, o_ref):
    seq = q_ref.shape[1]
    q_all = jnp.concatenate([q_ref[h] for h in range(MLA_HEADS)], axis=0)
    q_lat = jnp.concatenate([_dot(q_ref[h], w_ukt_ref[h]) for h in range(MLA_HEADS)], axis=0).astype(_BF16)
    q_pe = q_all[:, ROPE_LO:ROPE_HI]
    past = ckv_past_ref.shape[1]
    chunk = past // SAMPLE_KEY_CHUNKS

    def issue(t):
        if t == SAMPLE_KEY_CHUNKS:
            c = ckv_new_ref[...].astype(_BF16)
            return c, _dot_nt(q_lat, c) + _dot_nt(q_all, kpe_new_ref[...].astype(_BF16))
        rows = slice(t * chunk, (t + 1) * chunk)
        c = ckv_past_ref[0, rows, :].astype(_BF16)
        return c, _dot_nt(q_lat, c) + _dot(q_pe, kpe_past_t_ref[0, :, rows].astype(_BF16))

    state = {}

    def finish(t, issued):
        c, s = issued
        tile_max = jnp.max(s, axis=-1, keepdims=True)
        if not state:
            m_new = tile_max
            p = jnp.exp2(s - m_new)
            state["l"] = jnp.sum(p, axis=-1, keepdims=True)
            state["acc"] = _dot(p.astype(_BF16), c)
        else:
            m_new = jnp.maximum(state["m"], tile_max)
            alpha = jnp.exp2(state["m"] - m_new)
            p = jnp.exp2(s - m_new)
            state["l"] = state["l"] * alpha + jnp.sum(p, axis=-1, keepdims=True)
            state["acc"] = state["acc"] * alpha + _dot(p.astype(_BF16), c)
        state["m"] = m_new

    _software_pipeline(range(SAMPLE_KEY_CHUNKS + 1), issue, finish)
    o_lat = (state["acc"] / state["l"]).astype(_BF16)
    o = sum(_dot(o_lat[h * seq:(h + 1) * seq], w_uvh_ref[h]) for h in range(MLA_HEADS))
    o_ref[...] = _rms(o, g_om_ref[...]).astype(o_ref.dtype)


def _sample_attn(q, ckv_new, kpe_new, ckv_past, kpe_past_t, w_ukt, w_uvh, g_om, batch, seq):
    past, kv_lora = ckv_past.shape[1:]
    width = g_om.shape[1]
    return pl.pallas_call(
        _sample_attn_kernel,
        grid=(batch,),
        in_specs=[pl.BlockSpec((MLA_HEADS, seq, LANES), lambda b: (0, b, 0)),
                  pl.BlockSpec((seq, kv_lora), lambda b: (b, 0)),
                  pl.BlockSpec((seq, LANES), lambda b: (b, 0)),
                  pl.BlockSpec((1, past, kv_lora), lambda b: (b, 0, 0)),
                  pl.BlockSpec((1, ROPE_DIM, past), lambda b: (b, 0, 0)),
                  _whole(w_ukt.shape), _whole(w_uvh.shape), _whole(g_om.shape)],
        out_specs=pl.BlockSpec((seq, width), lambda b: (b, 0)),
        out_shape=jax.ShapeDtypeStruct((batch * seq, width), _BF16),
        compiler_params=pltpu.CompilerParams(dimension_semantics=("arbitrary",),
                                             vmem_limit_bytes=VMEM_LIMIT_BYTES),
        name="sample_attn",
    )(q, ckv_new, kpe_new, ckv_past, kpe_past_t, w_ukt, w_uvh, g_om)


def _mix_ffn(x_ref, an_ref, cn_ref, weight_refs, y_ref):
    w_out_a_ref, w_out_c_ref, ln_ffn_ref, w_gate_ref, w_up_ref, w_down_ref, g_final_ref = weight_refs
    h = x_ref[...] + _dot(an_ref[...], w_out_a_ref[...]) + _dot(cn_ref[...], w_out_c_ref[...])
    f = _rms(h, ln_ffn_ref[...]).astype(_BF16)
    gate = _dot(f, w_gate_ref[...])
    act = (gate * jax.nn.sigmoid(gate) * _dot(f, w_up_ref[...])).astype(_BF16)
    h = h + _dot(act, w_down_ref[...])
    y_ref[...] = _rms(h, g_final_ref[...])


def _output_kernel(x_ref, an_ref, cn_ref, xs_ref, ans_ref, cns_ref, *refs):
    weight_refs, (y_ref, ys_ref) = refs[:-2], refs[-2:]
    last = pl.num_programs(0) - 1

    @pl.when(pl.program_id(0) < last)
    def _():
        _mix_ffn(x_ref, an_ref, cn_ref, weight_refs, y_ref)

    @pl.when(pl.program_id(0) == last)
    def _():
        _mix_ffn(xs_ref, ans_ref, cns_ref, weight_refs, ys_ref)


def _output(x, an, cn, xs, ans, cns, wts):
    n, d = x.shape
    tm = min(ROW_TILE, n)
    n_tiles = n // tm
    names = ("w_out_a", "w_out_c", "ln_ffn", "w_gate", "w_up", "w_down", "g_final")
    row_block = lambda w: pl.BlockSpec((tm, w), lambda i: (jnp.minimum(i, n_tiles - 1), 0))
    resident = lambda shape: pl.BlockSpec(shape, lambda i: (0,) * len(shape), pipeline_mode=pl.Buffered(1))
    return pl.pallas_call(
        _output_kernel,
        grid=(n_tiles + 1,),
        in_specs=[row_block(d), row_block(an.shape[1]), row_block(cn.shape[1]),
                  resident(xs.shape), resident(ans.shape), resident(cns.shape)]
                 + [resident(wts[k].shape) for k in names],
        out_specs=(row_block(d), _whole(xs.shape)),
        out_shape=(jax.ShapeDtypeStruct((n, d), _F32), jax.ShapeDtypeStruct(xs.shape, _F32)),
        compiler_params=pltpu.CompilerParams(dimension_semantics=("arbitrary",),
                                             vmem_limit_bytes=VMEM_LIMIT_BYTES),
        name="output",
    )(x, an, cn, xs, ans, cns, *[wts[k] for k in names])


def _pad_lanes(w, left):
    return jnp.pad(w, [(0, 0)] * (w.ndim - 1) + [(left, LANES - left - w.shape[-1])])


def _prepare_weights(ln_mix, w_in, g_q, w_uq, g_kv, w_uk, w_uv, w_dw, b_dw, g_cn, b_cn, g_om, g_oc, w_out,
                     ln_ffn, g_final):
    q_lora, kv_lora = g_q.shape[0], g_kv.shape[0]
    mla_width = g_om.shape[0]
    row = lambda v: v.reshape(1, -1)
    c1, c2 = q_lora + kv_lora, q_lora + kv_lora + ROPE_DIM
    with_swap = lambda w: jnp.concatenate([w, w[..., HALF_ROPE:], w[..., :HALF_ROPE]], axis=-1)
    w_in_p = jnp.concatenate([w_in[:, :c1], _pad_lanes(with_swap(w_in[:, c1:c2]), ROPE_LO), w_in[:, c2:]], axis=1)
    w_uq_h = w_uq.reshape(q_lora, MLA_HEADS, NOPE_DIM + ROPE_DIM)
    w_uq_p = jnp.concatenate([w_uq_h[..., :NOPE_DIM], with_swap(w_uq_h[..., NOPE_DIM:])], axis=-1)
    w_uq_p = w_uq_p.reshape(q_lora, -1)
    w_uk_p = _pad_lanes(w_uk, 0).reshape(kv_lora, -1)
    w_uvt = jnp.pad(jnp.transpose(w_uv, (1, 2, 0)), ((0, 0), (0, VT_ROWS - V_DIM), (0, 0)))
    v_one = jnp.zeros((MLA_HEADS, VT_ROWS, 1), _F32).at[:, V_DIM].set(1.0)
    w_ukt = jnp.pad(jnp.transpose(w_uk, (1, 2, 0)), ((0, 0), (0, LANES - NOPE_DIM), (0, 0)))
    w_uvh = jnp.stack([jnp.pad(w_uv[:, h], ((0, 0), (h * V_DIM, (MLA_HEADS - 1 - h) * V_DIM)))
                       for h in range(MLA_HEADS)])
    return {
        "ln_mix": row(ln_mix), "w_in": w_in_p.astype(_BF16), "g_q": row(g_q), "w_uq": w_uq_p.astype(_BF16),
        "g_kv": row(g_kv), "w_uk": w_uk_p.astype(_BF16), "w_uvt": w_uvt.reshape(-1, kv_lora).astype(_BF16),
        "v_one": v_one.reshape(-1, 1), "w_dw": w_dw, "b_dw": row(b_dw), "g_cn": row(g_cn), "b_cn": row(b_cn),
        "g_om": row(g_om), "g_oc": row(g_oc),
        "w_ukt": w_ukt.astype(_BF16), "w_uvh": w_uvh.astype(_BF16),
        "w_out_a": w_out[:mla_width].astype(_BF16), "w_out_c": w_out[mla_width:].astype(_BF16),
        "ln_ffn": row(ln_ffn), "g_final": row(g_final),
    }


def _layer(x_prompt, x_sample, ckv_past, kpe_past, conv_past, wts, ffn_f32):
    batch, seq, d = x_prompt.shape
    dec_batch, dec_seq, _ = x_sample.shape
    past = ckv_past.shape[1]

    xp = x_prompt.reshape(batch * seq, d)
    q, k, vt, kv_p, kr_p_t, cn, cv_p = _prompt_proj(xp, _rope_tables(0, seq, 1), wts, seq)
    an, ffn_bf16 = _prompt_attn(q, k, vt, wts["g_om"], list(ffn_f32.values()), batch, seq)
    wts = dict(wts, **dict(zip(ffn_f32, ffn_bf16)))

    xs = x_sample.reshape(dec_batch * dec_seq, d)
    tabs = _rope_tables(past, dec_seq, dec_batch)
    q, kv_s, kr_s, cn_s, cv_s = _sample_proj(xs, tabs, conv_past, wts, dec_seq)
    an_s = _sample_attn(q, kv_s, kr_s, ckv_past, jnp.swapaxes(kpe_past, 1, 2), wts["w_ukt"], wts["w_uvh"],
                        wts["g_om"], dec_batch, dec_seq)
    y_p, y_s = _output(xp, an, cn, xs, an_s, cn_s, wts)

    return (y_p.reshape(batch, seq, d), y_s.reshape(dec_batch, dec_seq, d),
            kv_p.reshape(batch, seq, -1), jnp.swapaxes(kr_p_t, 1, 2), cv_p,
            kv_s.reshape(dec_batch, dec_seq, -1),
            kr_s[:, ROPE_LO:ROPE_HI].reshape(dec_batch, dec_seq, ROPE_DIM), cv_s)


def kernel(x_prompt, x_sample, cache_kv_latent, cache_k_rope, state_conv, ln_mix, w_in, g_q, w_uq, g_kv, w_uk, w_uv, w_dw, b_dw, g_cn, b_cn, g_om, g_oc, w_out, ln_ffn, w_gate, w_up, w_down, g_final):
    depth = w_in.shape[0]
    assert depth == 1, "the kernel implements the single-layer model of the problem"
    wts = _prepare_weights(ln_mix[0], w_in[0], g_q[0], w_uq[0], g_kv[0], w_uk[0], w_uv[0], w_dw[0], b_dw[0],
                           g_cn[0], b_cn[0], g_om[0], g_oc[0], w_out[0], ln_ffn[0], g_final)
    ffn_f32 = {"w_gate": w_gate[0], "w_up": w_up[0], "w_down": w_down[0]}
    outs = _layer(x_prompt, x_sample, cache_kv_latent[0], cache_k_rope[0], state_conv[0], wts, ffn_f32)
    y_p, y_s = outs[0], outs[1]
    return (y_p, y_s) + tuple(o[None] for o in outs[2:])
```

```python
import functools
import math

import jax
import jax.numpy as jnp
from jax import lax
from jax.experimental import pallas as pl
from jax.experimental.pallas import tpu as pltpu

CHUNK = 64
MLA_HEADS = 8
NOPE_DIM = 64
ROPE_DIM = 32
V_DIM = 64
ROPE_THETA = 10000.0
EPS = 1e-6
CONV_W = 31
CONV_STATE = CONV_W - 1
ATTN_SCALE = 1.0 / math.sqrt(NOPE_DIM + ROPE_DIM)
MASK_VALUE = -1e30

LANES = 128
SUBLANES = 8
HALF_ROPE = ROPE_DIM // 2
ROPE_LO = NOPE_DIM
ROPE_HI = ROPE_LO + ROPE_DIM
assert ROPE_HI + ROPE_DIM == LANES
VT_ROWS = 80
HALO_ROWS = 32
HALO_PAD = HALO_ROWS - CONV_STATE
CONV_STRIDE = 4
CONV_BLOCK = SUBLANES * CONV_STRIDE
Q_SCALE = ATTN_SCALE * math.log2(math.e)

ROPE_FINE = 32
GLU_PIECE = 256
HEADS_PER_PIECE = 2
ROW_TILE = 512
ATTN_TILE = 1024
KEY_PARTS = 4
SCORE_LOOKAHEAD = 8
SAMPLE_KEY_CHUNKS = 4
ATTN_DIAG_TILE = 256
CAST_ROW_ALIGN = 32
VMEM_LIMIT_BYTES = 56 * 1024 * 1024

_BF16 = jnp.bfloat16
_F32 = jnp.float32


def _rms(x, g):
    return x * lax.rsqrt(jnp.mean(x * x, axis=-1, keepdims=True) + EPS) * g


def _dot(a, b):
    return jnp.dot(a, b, preferred_element_type=_F32)


def _dot_nt(a, b):
    return lax.dot_general(a, b, (((1,), (1,)), ((), ())), preferred_element_type=_F32)


def _rope(x, table, rest):
    y = x * table
    lane = lax.broadcasted_iota(jnp.int32, x.shape, 1)
    in_rope = (lane >= ROPE_LO) & (lane < ROPE_HI)
    return jnp.where(in_rope, y + pltpu.roll(y, LANES - ROPE_DIM, 1), y if rest is None else rest)


def _rope_table_kernel(inv_ref, *tab_refs, ranges):
    for n, (base, count) in enumerate(ranges):
        _fill_rope_tables(inv_ref, tab_refs[2 * n], tab_refs[2 * n + 1], base, count, math.gcd(count, ROPE_FINE))


def _fill_rope_tables(inv_ref, q_tab_ref, k_tab_ref, base, count, fine):
    inv = inv_ref[...].reshape(1, 1, LANES)
    coarse = base + fine * lax.broadcasted_iota(jnp.int32, (count // fine, 1, LANES), 0)
    offset = lax.broadcasted_iota(jnp.int32, (1, fine, LANES), 1)
    ang_a, ang_b = coarse.astype(_F32) * inv, offset.astype(_F32) * inv
    ca, sa, cb, sb = jnp.cos(ang_a), jnp.sin(ang_a), jnp.cos(ang_b), jnp.sin(ang_b)
    c = (ca * cb - sa * sb).reshape(count, LANES)
    s = (sa * cb + ca * sb).reshape(count, LANES)
    lane = lax.broadcasted_iota(jnp.int32, (count, LANES), 1)
    rot = jnp.where(lane < ROPE_HI, c, jnp.where(lane < ROPE_HI + HALF_ROPE, -s, s))
    q_tab = jnp.where(lane < ROPE_LO, Q_SCALE, rot * Q_SCALE)
    for r in range(k_tab_ref.shape[0] // count):
        k_tab_ref[r * count:(r + 1) * count, :] = rot
        q_tab_ref[r * count:(r + 1) * count, :] = q_tab


def _rope_tables(specs):
    inv = 1.0 / (ROPE_THETA ** (jnp.arange(0, ROPE_DIM, 2, dtype=_F32) / ROPE_DIM))
    inv_lanes = jnp.zeros((1, LANES), _F32).at[0, ROPE_LO:].set(jnp.tile(inv, 4))
    assert all(math.gcd(count, ROPE_FINE) % SUBLANES == 0 for _, count, _ in specs)
    shapes = [jax.ShapeDtypeStruct((count * repeat, LANES), _F32) for _, count, repeat in specs for _ in range(2)]
    tabs = pl.pallas_call(
        functools.partial(_rope_table_kernel, ranges=tuple((base, count) for base, count, _ in specs)),
        out_shape=shapes,
        name="rope_table",
    )(inv_lanes)
    return [tuple(tabs[2 * n:2 * n + 2]) for n in range(len(specs))]


def _in_proj(x_ref, ln_mix_ref, w_in_ref, widths):
    hn = _rms(x_ref[...], ln_mix_ref[...]).astype(_BF16)
    outs, start = [], 0
    for w in widths:
        outs.append(_dot(hn, w_in_ref[:, start:start + w]))
        start += w
    return outs


def _query(cq, g_q_ref, w_uq_ref, q_tab, q_ref):
    q = _dot(_rms(cq, g_q_ref[...]).astype(_BF16), w_uq_ref[...])
    for h in range(MLA_HEADS):
        q_ref[h] = _rope(q[:, h * LANES:(h + 1) * LANES], q_tab, None).astype(q_ref.dtype)


def _depthwise_conv(ubuf, obuf, w_dw_ref, b_dw_ref):
    groups, n, _ = obuf.shape

    def block(g, base):
        lanes = slice(g * LANES, (g + 1) * LANES)
        taps = [ubuf[g, pl.ds(base + HALO_PAD + s, SUBLANES, stride=CONV_STRIDE), :]
                for s in range(CONV_STRIDE + CONV_W - 1)]
        for c in range(CONV_STRIDE):
            acc = jnp.broadcast_to(b_dw_ref[:, lanes], (SUBLANES, LANES))
            for k in range(CONV_W):
                acc = acc + taps[c + k] * w_dw_ref[k:k + 1, lanes]
            obuf[g, pl.ds(base + c, SUBLANES, stride=CONV_STRIDE), :] = acc

    return [functools.partial(block, g, base) for g in range(groups) for base in range(0, n, CONV_BLOCK)]


def _interleave(stages, fillers):
    fillers = list(fillers)
    share = -(-len(fillers) // len(stages))
    for stage in stages:
        stage()
        for filler in fillers[:share]:
            filler()
        fillers = fillers[share:]


def _split_lanes(x):
    return jnp.stack([x[:, c:c + LANES] for c in range(0, x.shape[1], LANES)])


def _merge_lanes(x):
    return jnp.concatenate(list(x), axis=1)


def _conv_branch(dw, g_cn_ref, b_cn_ref, g_oc_ref):
    mu = jnp.mean(dw, axis=-1, keepdims=True)
    xc = dw - mu
    y = xc * lax.rsqrt(jnp.mean(xc * xc, axis=-1, keepdims=True) + EPS) * g_cn_ref[...] + b_cn_ref[...]
    return _rms(y * jax.nn.sigmoid(y), g_oc_ref[...])


def _prompt_proj_kernel(x_ref, q_tab_ref, k_tab_ref, ln_mix_ref, w_in_ref, g_q_ref, w_uq_ref,
                        g_kv_ref, w_uk_ref, w_uvt_ref, v_one_ref, w_dw_ref, b_dw_ref, g_cn_ref, b_cn_ref,
                        g_oc_ref,
                        q_ref, k_ref, vt_ref, ckv_ref, kpe_t_ref, cn_ref, ncv_ref, ubuf, obuf, *, tiles_per_seq):
    rows = x_ref.shape[0]
    q_lora, kv_lora, conv_ch = g_q_ref.shape[1], g_kv_ref.shape[1], g_cn_ref.shape[1]
    hn = _rms(x_ref[...], ln_mix_ref[...]).astype(_BF16)
    w_cq, w_ckv, w_kpe = 0, q_lora, q_lora + kv_lora
    w_a, w_gate = w_kpe + LANES, w_kpe + LANES + conv_ch

    @pl.when(lax.rem(pl.program_id(0), tiles_per_seq) == 0)
    def _():
        ubuf[:, 0:HALO_ROWS, :] = jnp.zeros((ubuf.shape[0], HALO_ROWS, LANES), _F32)

    def glu(c0):
        cols = slice(c0, c0 + GLU_PIECE)
        u = (_dot(hn, w_in_ref[:, w_a + c0:w_a + c0 + GLU_PIECE])
             * jax.nn.sigmoid(_dot(hn, w_in_ref[:, w_gate + c0:w_gate + c0 + GLU_PIECE])))
        ubuf[c0 // LANES:(c0 + GLU_PIECE) // LANES, HALO_ROWS:HALO_ROWS + rows, :] = _split_lanes(u)
        ncv_ref[0, :, cols] = u[rows - CONV_STATE:, :]

    state = {}

    def query_latent():
        cq = _dot(hn, w_in_ref[:, w_cq:w_cq + q_lora])
        state["cq"] = _rms(cq, g_q_ref[...]).astype(_BF16)

    def key_latent():
        ckv = _rms(_dot(hn, w_in_ref[:, w_ckv:w_ckv + kv_lora]), g_kv_ref[...])
        ckv_ref[...] = ckv
        state["ckv"] = ckv.astype(_BF16)
        kpe = _rope(_dot(hn, w_in_ref[:, w_kpe:w_kpe + LANES]), k_tab_ref[...], 0.0)
        kpe_t_ref[0] = kpe.T[ROPE_LO:ROPE_HI, :]
        state["kpe"] = kpe

    def query_heads(h0):
        q = _dot(state["cq"], w_uq_ref[:, h0 * LANES:(h0 + HEADS_PER_PIECE) * LANES])
        for j in range(HEADS_PER_PIECE):
            q_ref[h0 + j] = _rope(q[:, j * LANES:(j + 1) * LANES], q_tab_ref[...], None).astype(q_ref.dtype)

    def key_heads(h0):
        k_nope = _dot(state["ckv"], w_uk_ref[:, h0 * LANES:(h0 + HEADS_PER_PIECE) * LANES])
        for j in range(HEADS_PER_PIECE):
            k_ref[h0 + j] = (k_nope[:, j * LANES:(j + 1) * LANES] + state["kpe"]).astype(k_ref.dtype)

    def values():
        vt = _dot_nt(w_uvt_ref[...], state["ckv"]) + v_one_ref[...]
        key_tile = vt_ref.shape[3]
        for h in range(MLA_HEADS):
            for t in range(rows // key_tile):
                vt_ref[h, t] = vt[h * VT_ROWS:(h + 1) * VT_ROWS,
                                  t * key_tile:(t + 1) * key_tile].astype(vt_ref.dtype)

    stages = [query_latent, key_latent]
    for h0 in range(0, MLA_HEADS, HEADS_PER_PIECE):
        stages += [functools.partial(query_heads, h0), functools.partial(key_heads, h0)]
    stages.append(values)
    glu(0)
    later_glu = [functools.partial(glu, c0) for c0 in range(GLU_PIECE, conv_ch, GLU_PIECE)]
    _interleave(stages[:2] + later_glu + stages[2:], _depthwise_conv(ubuf, obuf, w_dw_ref, b_dw_ref))

    cn_ref[...] = _conv_branch(_merge_lanes(obuf[...]), g_cn_ref, b_cn_ref, g_oc_ref).astype(cn_ref.dtype)
    ubuf[:, 0:HALO_ROWS, :] = ubuf[:, rows:HALO_ROWS + rows, :]


def _sample_proj_kernel(x_ref, q_tab_ref, k_tab_ref, state_ref, ln_mix_ref, w_in_ref, g_q_ref,
                        w_uq_ref, g_kv_ref, w_dw_ref, b_dw_ref, g_cn_ref, b_cn_ref, g_oc_ref,
                        q_ref, ckv_ref, kpe_ref, cn_ref, ncv_ref, ubuf, obuf):
    segs = state_ref.shape[0]
    seq = x_ref.shape[0] // segs
    stride = HALO_ROWS + seq
    q_lora, kv_lora, conv_ch = g_q_ref.shape[1], g_kv_ref.shape[1], g_cn_ref.shape[1]
    cq, ckv, kpe, a, gate = _in_proj(x_ref, ln_mix_ref, w_in_ref, (q_lora, kv_lora, LANES, conv_ch, conv_ch))
    _query(cq, g_q_ref, w_uq_ref, q_tab_ref[...], q_ref)
    ckv_ref[...] = _rms(ckv, g_kv_ref[...])
    kpe_ref[...] = _rope(kpe, k_tab_ref[...], 0.0)

    u = a * jax.nn.sigmoid(gate)
    ubuf[...] = jnp.zeros(ubuf.shape, _F32)
    for s in range(segs):
        ubuf[:, s * stride + HALO_PAD:s * stride + HALO_ROWS, :] = _split_lanes(state_ref[s])
        ubuf[:, s * stride + HALO_ROWS:(s + 1) * stride, :] = _split_lanes(u[s * seq:(s + 1) * seq, :])
    for group in _depthwise_conv(ubuf, obuf, w_dw_ref, b_dw_ref):
        group()
    dw = jnp.concatenate([_merge_lanes(obuf[:, s * stride:s * stride + seq, :]) for s in range(segs)], axis=0)
    cn_ref[...] = _conv_branch(dw, g_cn_ref, b_cn_ref, g_oc_ref).astype(cn_ref.dtype)
    for s in range(segs):
        ncv_ref[s] = jnp.concatenate([state_ref[s][seq:, :], u[s * seq:(s + 1) * seq, :]], axis=0)


def _whole(shape):
    zeros = (0,) * len(shape)
    return pl.BlockSpec(shape, lambda *_: zeros)


def _prompt_proj(x, tabs, wts, seq):
    n, d = x.shape
    tm = min(2 * ROW_TILE, seq)
    key_tile = min(ATTN_TILE, seq)
    assert tm % key_tile == 0 and seq % tm == 0 and tm % CONV_BLOCK == 0
    tiles_per_seq = seq // tm
    conv_ch = wts["g_cn"].shape[1]
    kv_lora = wts["g_kv"].shape[1]
    row_block = lambda w: pl.BlockSpec((tm, w), lambda i: (i, 0))
    head_block = pl.BlockSpec((MLA_HEADS, tm, LANES), lambda i: (0, i, 0))
    vt_block = pl.BlockSpec((MLA_HEADS, tm // key_tile, VT_ROWS, key_tile), lambda i: (0, i, 0, 0))
    tab_block = pl.BlockSpec((tm, LANES), lambda i: (lax.rem(i, tiles_per_seq), 0))
    names = ("ln_mix", "w_in", "g_q", "w_uq", "g_kv", "w_uk", "w_uvt", "v_one", "w_dw", "b_dw", "g_cn",
             "b_cn", "g_oc")
    head_shape = jax.ShapeDtypeStruct((MLA_HEADS, n, LANES), _BF16)
    return pl.pallas_call(
        functools.partial(_prompt_proj_kernel, tiles_per_seq=tiles_per_seq),
        grid=(n // tm,),
        in_specs=[row_block(d), tab_block, tab_block] + [_whole(wts[k].shape) for k in names],
        out_specs=(head_block, head_block, vt_block, row_block(kv_lora),
                   pl.BlockSpec((1, ROPE_DIM, tm), lambda i: (i // tiles_per_seq, 0, lax.rem(i, tiles_per_seq))),
                   row_block(conv_ch), pl.BlockSpec((1, CONV_STATE, conv_ch), lambda i: (i // tiles_per_seq, 0, 0))),
        out_shape=(head_shape, head_shape,
                   jax.ShapeDtypeStruct((MLA_HEADS, n // key_tile, VT_ROWS, key_tile), _BF16),
                   jax.ShapeDtypeStruct((n, kv_lora), _F32), jax.ShapeDtypeStruct((n // seq, ROPE_DIM, seq), _F32),
                   jax.ShapeDtypeStruct((n, conv_ch), _BF16),
                   jax.ShapeDtypeStruct((n // seq, CONV_STATE, conv_ch), _F32)),
        scratch_shapes=[pltpu.VMEM((conv_ch // LANES, HALO_ROWS + tm, LANES), _F32),
                        pltpu.VMEM((conv_ch // LANES, tm, LANES), _F32)],
        compiler_params=pltpu.CompilerParams(dimension_semantics=("arbitrary",),
                                             vmem_limit_bytes=VMEM_LIMIT_BYTES),
        name="prompt_proj",
    )(x, *tabs, *[wts[k] for k in names])


def _sample_proj(x, tabs, state, wts, seq):
    n, d = x.shape
    conv_ch = wts["g_cn"].shape[1]
    kv_lora = wts["g_kv"].shape[1]
    names = ("ln_mix", "w_in", "g_q", "w_uq", "g_kv", "w_dw", "b_dw", "g_cn", "b_cn", "g_oc")
    conv_rows = (n // seq) * (HALO_ROWS + seq)
    assert conv_rows % CONV_BLOCK == 0
    return pl.pallas_call(
        _sample_proj_kernel,
        out_shape=(jax.ShapeDtypeStruct((MLA_HEADS, n, LANES), _BF16),
                   jax.ShapeDtypeStruct((n, kv_lora), _F32), jax.ShapeDtypeStruct((n, LANES), _F32),
                   jax.ShapeDtypeStruct((n, conv_ch), _BF16),
                   jax.ShapeDtypeStruct((n // seq, CONV_STATE, conv_ch), _F32)),
        scratch_shapes=[pltpu.VMEM((conv_ch // LANES, conv_rows + HALO_ROWS, LANES), _F32),
                        pltpu.VMEM((conv_ch // LANES, conv_rows, LANES), _F32)],
        compiler_params=pltpu.CompilerParams(vmem_limit_bytes=VMEM_LIMIT_BYTES),
        name="sample_proj",
    )(x, *tabs, state, *[wts[k] for k in names])


def _software_pipeline(items, issue, finish):
    items = list(items)
    issued, done = [], []
    for t in range(len(items) + SCORE_LOOKAHEAD):
        if t < len(items):
            issued.append(issue(items[t]))
        if t >= SCORE_LOOKAHEAD:
            done.append(finish(items[t - SCORE_LOOKAHEAD], issued[t - SCORE_LOOKAHEAD]))
    return done


def _prompt_attn_kernel(q_ref, k_ref, vt_ref, g_om_ref, *refs, sub):
    n_cast = (len(refs) - 1) // 2
    o_ref = refs[n_cast]
    for src_ref, dst_ref in zip(refs[:n_cast], refs[n_cast + 1:]):
        dst_ref[...] = src_ref[...].astype(dst_ref.dtype)
    tile = q_ref.shape[1]
    i = pl.program_id(1)
    chunk_of = lambda t: lax.shift_right_logical(t, CHUNK.bit_length() - 1)

    def key_rows(j):
        return pl.ds(pl.multiple_of(j * tile, tile), tile)

    spans = []
    for r in range(tile // sub):
        n_keys = (r + 1) * sub
        key = lax.broadcasted_iota(jnp.int32, (n_keys, sub), 0)
        query = lax.broadcasted_iota(jnp.int32, (n_keys, sub), 1) + r * sub
        spans.append((slice(r * sub, (r + 1) * sub), n_keys, chunk_of(key) <= chunk_of(query)))

    def diag_scores(h, span):
        q_rows, n_keys, mask = span
        keys = pl.ds(pl.multiple_of(i * tile, tile), n_keys)
        return jnp.where(mask, _dot_nt(k_ref[h, keys, :], q_ref[h, q_rows, :]), MASK_VALUE)

    def diag_finish(h, span, s):
        m = jnp.max(s, axis=0, keepdims=True)
        return m, _dot(vt_ref[h, i, :, 0:span[1]], jnp.exp2(s - m).astype(_BF16))

    items = [(h, span) for h in range(MLA_HEADS) for span in spans]
    done = _software_pipeline(items, lambda it: diag_scores(*it), lambda it, s: diag_finish(*it, s))
    per_head = len(spans)
    maxes = [jnp.concatenate([m for m, _ in done[h * per_head:(h + 1) * per_head]], axis=1)
             for h in range(MLA_HEADS)]
    accs = [jnp.concatenate([a for _, a in done[h * per_head:(h + 1) * per_head]], axis=1)
            for h in range(MLA_HEADS)]

    def body(j, carry):
        state = [list(carry[0]), list(carry[1])]
        part = tile // KEY_PARTS

        def issue(item):
            h, t = item
            rows = pl.ds(pl.multiple_of(j * tile + t * part, part), part)
            return _dot_nt(k_ref[h, rows, :], q_ref[h])

        def finish(item, s):
            h, t = item
            m_old, acc = state[0][h], state[1][h]
            m_new = jnp.maximum(m_old, jnp.max(s, axis=0, keepdims=True))
            p = jnp.exp2(s - m_new).astype(_BF16)
            state[0][h] = m_new
            state[1][h] = acc * jnp.exp2(m_old - m_new) + _dot(vt_ref[h, j, :, t * part:(t + 1) * part], p)

        _software_pipeline([(h, t) for h in range(MLA_HEADS) for t in range(KEY_PARTS)], issue, finish)
        return tuple(state[0]), tuple(state[1])

    _, accs = lax.fori_loop(0, i, body, (tuple(maxes), tuple(accs)))

    o_t = jnp.concatenate([acc[0:V_DIM] / acc[V_DIM:V_DIM + 1] for acc in accs], axis=0)
    o_ref[...] = _rms(o_t.T, g_om_ref[...]).astype(o_ref.dtype)


def _cast_block_rows(rows, steps):
    block = -(-rows // (steps * CAST_ROW_ALIGN)) * CAST_ROW_ALIGN
    while rows % block:
        block += CAST_ROW_ALIGN
    return block


def _prompt_attn(q, k, vt, g_om, to_cast, batch, seq):
    tile = vt.shape[3]
    nq = seq // tile
    width = g_om.shape[1]
    cast_specs = []
    for w in to_cast:
        block = _cast_block_rows(w.shape[0], batch * nq)
        cast_specs.append(pl.BlockSpec(
            (block, w.shape[1]),
            lambda b, i, last=w.shape[0] // block - 1: (jnp.minimum(b * nq + i, last), 0)))
    outs = pl.pallas_call(
        functools.partial(_prompt_attn_kernel, sub=min(ATTN_DIAG_TILE, tile)),
        grid=(batch, nq),
        in_specs=[pl.BlockSpec((MLA_HEADS, tile, LANES), lambda b, i: (0, b * nq + i, 0)),
                  pl.BlockSpec((MLA_HEADS, seq, LANES), lambda b, i: (0, b, 0)),
                  pl.BlockSpec((MLA_HEADS, nq, VT_ROWS, tile), lambda b, i: (0, b, 0, 0)),
                  _whole(g_om.shape)] + cast_specs,
        out_specs=[pl.BlockSpec((tile, width), lambda b, i: (b * nq + i, 0))] + cast_specs,
        out_shape=[jax.ShapeDtypeStruct((batch * seq, width), _BF16)]
                  + [jax.ShapeDtypeStruct(w.shape, _BF16) for w in to_cast],
        compiler_params=pltpu.CompilerParams(dimension_semantics=("arbitrary", "arbitrary"),
                                             vmem_limit_bytes=VMEM_LIMIT_BYTES),
        name="prompt_attn",
    )(q, k, vt, g_om, *to_cast)
    return outs[0], outs[1:]


def _sample_attn_kernel(q_ref, ckv_new_ref, kpe_new_ref, ckv_past_ref, kpe_past_t_ref, w_ukt_ref, w_uvh_ref,
                        g_om_ref, o_ref):
    seq = q_ref.shape[1]
    q_all = jnp.concatenate([q_ref[h] for h in range(MLA_HEADS)], axis=0)
    q_lat = jnp.concatenate([_dot(q_ref[h], w_ukt_ref[h]) for h in range(MLA_HEADS)], axis=0).astype(_BF16)
    q_pe = q_all[:, ROPE_LO:ROPE_HI]
    past = ckv_past_ref.shape[1]
    chunk = past // SAMPLE_KEY_CHUNKS

    def issue(t):
        if t == SAMPLE_KEY_CHUNKS:
            c = ckv_new_ref[...].astype(_BF16)
            return c, _dot_nt(q_lat, c) + _dot_nt(q_all, kpe_new_ref[...].astype(_BF16))
        rows = slice(t * chunk, (t + 1) * chunk)
        c = ckv_past_ref[0, rows, :].astype(_BF16)
        return c, _dot_nt(q_lat, c) + _dot(q_pe, kpe_past_t_ref[0, :, rows].astype(_BF16))

    state = {}

    def finish(t, issued):
        c, s = issued
        tile_max = jnp.max(s, axis=-1, keepdims=True)
        if not state:
            m_new = tile_max
            p = jnp.exp2(s - m_new)
            state["l"] = jnp.sum(p, axis=-1, keepdims=True)
            state["acc"] = _dot(p.astype(_BF16), c)
        else:
            m_new = jnp.maximum(state["m"], tile_max)
            alpha = jnp.exp2(state["m"] - m_new)
            p = jnp.exp2(s - m_new)
            state["l"] = state["l"] * alpha + jnp.sum(p, axis=-1, keepdims=True)
            state["acc"] = state["acc"] * alpha + _dot(p.astype(_BF16), c)
        state["m"] = m_new

    _software_pipeline(range(SAMPLE_KEY_CHUNKS + 1), issue, finish)
    o_lat = (state["acc"] / state["l"]).astype(_BF16)
    o = sum(_dot(o_lat[h * seq:(h + 1) * seq], w_uvh_ref[h]) for h in range(MLA_HEADS))
    o_ref[...] = _rms(o, g_om_ref[...]).astype(o_ref.dtype)


def _sample_attn(q, ckv_new, kpe_new, ckv_past, kpe_past_t, w_ukt, w_uvh, g_om, batch, seq):
    past, kv_lora = ckv_past.shape[1:]
    width = g_om.shape[1]
    return pl.pallas_call(
        _sample_attn_kernel,
        grid=(batch,),
        in_specs=[pl.BlockSpec((MLA_HEADS, seq, LANES), lambda b: (0, b, 0)),
                  pl.BlockSpec((seq, kv_lora), lambda b: (b, 0)),
                  pl.BlockSpec((seq, LANES), lambda b: (b, 0)),
                  pl.BlockSpec((1, past, kv_lora), lambda b: (b, 0, 0)),
                  pl.BlockSpec((1, ROPE_DIM, past), lambda b: (b, 0, 0)),
                  _whole(w_ukt.shape), _whole(w_uvh.shape), _whole(g_om.shape)],
        out_specs=pl.BlockSpec((seq, width), lambda b: (b, 0)),
        out_shape=jax.ShapeDtypeStruct((batch * seq, width), _BF16),
        compiler_params=pltpu.CompilerParams(dimension_semantics=("arbitrary",),
                                             vmem_limit_bytes=VMEM_LIMIT_BYTES),
        name="sample_attn",
    )(q, ckv_new, kpe_new, ckv_past, kpe_past_t, w_ukt, w_uvh, g_om)


def _mix_ffn(x_ref, an_ref, cn_ref, weight_refs, y_ref):
    w_out_a_ref, w_out_c_ref, ln_ffn_ref, w_gate_ref, w_up_ref, w_down_ref, g_final_ref = weight_refs
    h = x_ref[...] + _dot(an_ref[...], w_out_a_ref[...]) + _dot(cn_ref[...], w_out_c_ref[...])
    f = _rms(h, ln_ffn_ref[...]).astype(_BF16)
    gate = _dot(f, w_gate_ref[...])
    act = (gate * jax.nn.sigmoid(gate) * _dot(f, w_up_ref[...])).astype(_BF16)
    h = h + _dot(act, w_down_ref[...])
    y_ref[...] = _rms(h, g_final_ref[...])


def _output_kernel(x_ref, an_ref, cn_ref, xs_ref, ans_ref, cns_ref, *refs):
    weight_refs, (y_ref, ys_ref) = refs[:-2], refs[-2:]
    last = pl.num_programs(0) - 1

    @pl.when(pl.program_id(0) < last)
    def _():
        _mix_ffn(x_ref, an_ref, cn_ref, weight_refs, y_ref)

    @pl.when(pl.program_id(0) == last)
    def _():
        _mix_ffn(xs_ref, ans_ref, cns_ref, weight_refs, ys_ref)


def _output(x, an, cn, xs, ans, cns, wts):
    n, d = x.shape
    tm = min(ROW_TILE, n)
    n_tiles = n // tm
    names = ("w_out_a", "w_out_c", "ln_ffn", "w_gate", "w_up", "w_down", "g_final")
    row_block = lambda w: pl.BlockSpec((tm, w), lambda i: (jnp.minimum(i, n_tiles - 1), 0))
    resident = lambda shape: pl.BlockSpec(shape, lambda i: (0,) * len(shape), pipeline_mode=pl.Buffered(1))
    return pl.pallas_call(
        _output_kernel,
        grid=(n_tiles + 1,),
        in_specs=[row_block(d), row_block(an.shape[1]), row_block(cn.shape[1]),
                  resident(xs.shape), resident(ans.shape), resident(cns.shape)]
                 + [resident(wts[k].shape) for k in names],
        out_specs=(row_block(d), _whole(xs.shape)),
        out_shape=(jax.ShapeDtypeStruct((n, d), _F32), jax.ShapeDtypeStruct(xs.shape, _F32)),
        compiler_params=pltpu.CompilerParams(dimension_semantics=("arbitrary",),
                                             vmem_limit_bytes=VMEM_LIMIT_BYTES),
        name="output",
    )(x, an, cn, xs, ans, cns, *[wts[k] for k in names])


def _pad_lanes(w, left):
    return jnp.pad(w, [(0, 0)] * (w.ndim - 1) + [(left, LANES - left - w.shape[-1])])


def _prepare_weights(ln_mix, w_in, g_q, w_uq, g_kv, w_uk, w_uv, w_dw, b_dw, g_cn, b_cn, g_om, g_oc, w_out,
                     ln_ffn, g_final):
    q_lora, kv_lora = g_q.shape[0], g_kv.shape[0]
    mla_width = g_om.shape[0]
    row = lambda v: v.reshape(1, -1)
    c1, c2 = q_lora + kv_lora, q_lora + kv_lora + ROPE_DIM
    with_swap = lambda w: jnp.concatenate([w, w[..., HALF_ROPE:], w[..., :HALF_ROPE]], axis=-1)
    w_in_p = jnp.concatenate([w_in[:, :c1], _pad_lanes(with_swap(w_in[:, c1:c2]), ROPE_LO), w_in[:, c2:]], axis=1)
    w_uq_h = w_uq.reshape(q_lora, MLA_HEADS, NOPE_DIM + ROPE_DIM)
    w_uq_p = jnp.concatenate([w_uq_h[..., :NOPE_DIM], with_swap(w_uq_h[..., NOPE_DIM:])], axis=-1)
    w_uq_p = w_uq_p.reshape(q_lora, -1)
    w_uk_p = _pad_lanes(w_uk, 0).reshape(kv_lora, -1)
    w_uvt = jnp.pad(jnp.transpose(w_uv, (1, 2, 0)), ((0, 0), (0, VT_ROWS - V_DIM), (0, 0)))
    v_one = jnp.zeros((MLA_HEADS, VT_ROWS, 1), _F32).at[:, V_DIM].set(1.0)
    w_ukt = jnp.pad(jnp.transpose(w_uk, (1, 2, 0)), ((0, 0), (0, LANES - NOPE_DIM), (0, 0)))
    w_uvh = jnp.stack([jnp.pad(w_uv[:, h], ((0, 0), (h * V_DIM, (MLA_HEADS - 1 - h) * V_DIM)))
                       for h in range(MLA_HEADS)])
    return {
        "ln_mix": row(ln_mix), "w_in": w_in_p.astype(_BF16), "g_q": row(g_q), "w_uq": w_uq_p.astype(_BF16),
        "g_kv": row(g_kv), "w_uk": w_uk_p.astype(_BF16), "w_uvt": w_uvt.reshape(-1, kv_lora).astype(_BF16),
        "v_one": v_one.reshape(-1, 1), "w_dw": w_dw, "b_dw": row(b_dw), "g_cn": row(g_cn), "b_cn": row(b_cn),
        "g_om": row(g_om), "g_oc": row(g_oc),
        "w_ukt": w_ukt.astype(_BF16), "w_uvh": w_uvh.astype(_BF16),
        "w_out_a": w_out[:mla_width].astype(_BF16), "w_out_c": w_out[mla_width:].astype(_BF16),
        "ln_ffn": row(ln_ffn), "g_final": row(g_final),
    }


def _layer(x_prompt, x_sample, ckv_past, kpe_past, conv_past, wts, ffn_f32):
    batch, seq, d = x_prompt.shape
    dec_batch, dec_seq, _ = x_sample.shape
    past = ckv_past.shape[1]

    xp = x_prompt.reshape(batch * seq, d)
    prompt_tabs, tabs = _rope_tables([(0, seq, 1), (past, dec_seq, dec_batch)])
    q, k, vt, kv_p, kr_p_t, cn, cv_p = _prompt_proj(xp, prompt_tabs, wts, seq)
    an, ffn_bf16 = _prompt_attn(q, k, vt, wts["g_om"], list(ffn_f32.values()), batch, seq)
    wts = dict(wts, **dict(zip(ffn_f32, ffn_bf16)))

    xs = x_sample.reshape(dec_batch * dec_seq, d)
    q, kv_s, kr_s, cn_s, cv_s = _sample_proj(xs, tabs, conv_past, wts, dec_seq)
    an_s = _sample_attn(q, kv_s, kr_s, ckv_past, jnp.swapaxes(kpe_past, 1, 2), wts["w_ukt"], wts["w_uvh"],
                        wts["g_om"], dec_batch, dec_seq)
    y_p, y_s = _output(xp, an, cn, xs, an_s, cn_s, wts)

    return (y_p.reshape(batch, seq, d), y_s.reshape(dec_batch, dec_seq, d),
            kv_p.reshape(batch, seq, -1), jnp.swapaxes(kr_p_t, 1, 2), cv_p,
            kv_s.reshape(dec_batch, dec_seq, -1),
            kr_s[:, ROPE_LO:ROPE_HI].reshape(dec_batch, dec_seq, ROPE_DIM), cv_s)


def kernel(x_prompt, x_sample, cache_kv_latent, cache_k_rope, state_conv, ln_mix, w_in, g_q, w_uq, g_kv, w_uk, w_uv, w_dw, b_dw, g_cn, b_cn, g_om, g_oc, w_out, ln_ffn, w_gate, w_up, w_down, g_final):
    depth = w_in.shape[0]
    assert depth == 1, "the kernel implements the single-layer model of the problem"
    wts = _prepare_weights(ln_mix[0], w_in[0], g_q[0], w_uq[0], g_kv[0], w_uk[0], w_uv[0], w_dw[0], b_dw[0],
                           g_cn[0], b_cn[0], g_om[0], g_oc[0], w_out[0], ln_ffn[0], g_final)
    ffn_f32 = {"w_gate": w_gate[0], "w_up": w_up[0], "w_down": w_down[0]}
    outs = _layer(x_prompt, x_sample, cache_kv_latent[0], cache_k_rope[0], state_conv[0], wts, ffn_f32)
    y_p, y_s = outs[0], outs[1]
    return (y_p, y_s) + tuple(o[None] for o in outs[2:])
```
